```python
import numpy as np
import jax
import jax.numpy as jnp
from jax import lax

D_MODEL = 2048
BATCH = 4
SEQ = 2048
DEPTH = 2

GRID_W = 64
CTX_LEN = 256
NORM_EPS = 1e-6
ROPE_BASE = 10000.0
Q_BLOCK = 128
CHUNK = 64

MLA_HEADS = 8
MLA_Q_RANK = 512
MLA_KV_RANK = 256
MLA_NOPE = 128
MLA_ROPE = 64
MLA_V = 128

GLA_HEADS = 4
GLA_DK = 64
GLA_DV = 128
GLA_GATE_RANK = 16
GLA_TAU = 16.0

RET_HEADS = 4
RET_DK = 128
RET_DV = 128
RET_DECAY_EXP0 = 5.0

N_EXPERTS = 16
EXPERT_FF = 2048
EC_CAPACITY = 2

MIX_WIDTH = MLA_HEADS * MLA_V + GLA_HEADS * GLA_DV + RET_HEADS * RET_DV
IN_SPLITS = (MLA_Q_RANK, MLA_KV_RANK, MLA_ROPE,
             GLA_HEADS * GLA_DK, GLA_HEADS * GLA_DK, GLA_HEADS * GLA_DV, 2 * GLA_GATE_RANK, GLA_HEADS * GLA_DV,
             RET_HEADS * RET_DK, RET_HEADS * RET_DK, RET_HEADS * RET_DV, RET_HEADS * RET_DV)
IN_COLS = sum(IN_SPLITS)

kernel_name = 'hybrid_mla_gla_retention_ecmoe_dit'


def rms_norm(x, gain=None):
    xf = x.astype(jnp.float32)
    y = xf * lax.rsqrt(jnp.mean(xf * xf, axis=-1, keepdims=True) + NORM_EPS)
    if gain is not None:
        y = y * gain.astype(jnp.float32)
    return y.astype(x.dtype)


def heads(t, n_heads):
    b, n, _ = t.shape
    return t.reshape(b, n, n_heads, -1).transpose(0, 2, 1, 3)


def merge_heads(t):
    b, h, n, d = t.shape
    return t.transpose(0, 2, 1, 3).reshape(b, n, h * d)


def rotate_pairs(x, cos, sin):
    cos = cos.astype(x.dtype)
    sin = sin.astype(x.dtype)
    x1 = x[..., 0::2]
    x2 = x[..., 1::2]
    return jnp.stack([x1 * cos - x2 * sin, x1 * sin + x2 * cos], axis=-1).reshape(x.shape)


def axial_rope_angles(rows):
    row = jnp.repeat(jnp.arange(rows, dtype=jnp.float32), GRID_W)
    col = jnp.tile(jnp.arange(GRID_W, dtype=jnp.float32), rows)
    n_freq = MLA_ROPE // 4
    inv = jnp.power(ROPE_BASE, -jnp.arange(n_freq, dtype=jnp.float32) / n_freq)
    ang = jnp.concatenate([row[:, None] * inv, col[:, None] * inv], axis=-1)
    return jnp.cos(ang), jnp.sin(ang)


def retention_angles(n_tokens):
    inv = 1.0 / jnp.power(ROPE_BASE, jnp.linspace(0.0, 1.0, RET_DK // 2, dtype=jnp.float32))
    ang = jnp.arange(n_tokens, dtype=jnp.float32)[:, None] * inv
    return jnp.cos(ang), jnp.sin(ang)


def retention_log_decay(direction):
    e = RET_DECAY_EXP0 + direction + 2.0 * jnp.arange(RET_HEADS, dtype=jnp.float32)
    return jnp.log1p(-jnp.exp2(-e))


def block_attention(q, k, v):
    b, h, n, dq = q.shape
    nb = n // Q_BLOCK
    qb = jnp.moveaxis(q.reshape(b, h, nb, Q_BLOCK, dq), 2, 0)

    def one_block(qblk):
        s = jnp.einsum('bhqd,bhkd->bhqk', qblk, k, preferred_element_type=jnp.float32)
        p = jax.nn.softmax(s, axis=-1).astype(v.dtype)
        return jnp.einsum('bhqk,bhkd->bhqd', p, v)

    o = lax.map(one_block, qb)
    return jnp.moveaxis(o, 0, 2).reshape(b, h, n, v.shape[-1])


def chunk_scan(q, k, v, log_a, s0, strict):
    b, h, n, kd = q.shape
    vd = v.shape[-1]
    nc = n // CHUNK

    def split(t):
        return jnp.moveaxis(t.astype(jnp.float32).reshape(b, h, nc, CHUNK, t.shape[-1]), 2, 0)

    mask = jnp.tril(jnp.ones((CHUNK, CHUNK), dtype=bool), -1 if strict else 0)

    def step(state, inp):
        qc, kc, vc, ac = inp
        cum = jnp.cumsum(ac, axis=2)
        q_dec = qc * jnp.exp(cum)
        k_inv = kc * jnp.exp(-cum)
        att = jnp.where(mask, jnp.einsum('bhik,bhjk->bhij', q_dec, k_inv), 0.0)
        o = jnp.einsum('bhij,bhjv->bhiv', att, vc) + jnp.einsum('bhik,bhkv->bhiv', q_dec, state)
        cum_end = cum[:, :, -1:, :]
        k_end = kc * jnp.exp(cum_end - cum)
        new_state = jnp.exp(cum_end[:, :, 0, :])[..., None] * state + jnp.einsum('bhjk,bhjv->bhkv', k_end, vc)
        return new_state, o

    final, o = lax.scan(step, s0.astype(jnp.float32), (split(q), split(k), split(v), split(log_a)))
    o = jnp.moveaxis(o, 0, 2).reshape(b, h, n, vd)
    return o.astype(v.dtype), final


def bidirectional_recurrence(q, k, v, la_f, la_b, qc, kc, vc, lac_f, lac_b):
    b, h, _, kd = q.shape
    s0 = jnp.zeros((b, h, kd, v.shape[-1]), jnp.float32)

    def flip(t):
        return jnp.flip(t, axis=2)

    oc_f, sc_f = chunk_scan(qc, kc, vc, lac_f, s0, False)
    oc_b, sc_b = chunk_scan(flip(qc), flip(kc), flip(vc), flip(lac_b), s0, True)
    o_f, _ = chunk_scan(q, k, v, la_f, sc_f, False)
    o_b, _ = chunk_scan(flip(q), flip(k), flip(v), flip(la_b), sc_b, True)
    return o_f + flip(o_b), oc_f + flip(oc_b)


def mla_mixer(lat, ctx, q_norm, w_uq, kv_norm, w_ukv, cos_a, sin_a, need_ctx):
    cq, ckv, kr = lat
    cq_c, ckv_c, kr_c = ctx
    scale = (MLA_NOPE + MLA_ROPE) ** -0.5

    def project_q(cq_):
        return heads(rms_norm(cq_, q_norm) @ w_uq, MLA_HEADS)

    def project_kv(ckv_):
        kv = heads(rms_norm(ckv_, kv_norm) @ w_ukv, MLA_HEADS)
        return kv[..., :MLA_NOPE], kv[..., MLA_NOPE:]

    def full_key(k_nope, k_rope):
        return jnp.concatenate([k_nope, jnp.broadcast_to(k_rope[:, None], k_nope.shape[:3] + (MLA_ROPE,))], axis=-1)

    q = project_q(cq)
    q = jnp.concatenate([q[..., :MLA_NOPE], rotate_pairs(q[..., MLA_NOPE:], cos_a, sin_a)], axis=-1) * scale
    k_nope, v = project_kv(ckv)
    k = full_key(k_nope, rotate_pairs(kr, cos_a, sin_a))
    kc_nope, vc = project_kv(ckv_c)
    kc = full_key(kc_nope, kr_c)
    o = block_attention(q, jnp.concatenate([kc, k], axis=2), jnp.concatenate([vc, v], axis=2))
    y = merge_heads(o)
    yc = None
    if need_ctx:
        qc = project_q(cq_c) * scale
        yc = merge_heads(block_attention(qc, kc, vc))
    return y, yc


def gla_mixer(lat, ctx, w2, b2, need_ctx):
    def prep(q, k, v, z):
        b, n, _ = z.shape
        z = z.reshape(b, n, 2, GLA_GATE_RANK)
        logits = jnp.einsum('bndr,drk->bndk', z, w2) + b2
        log_a = jax.nn.log_sigmoid(logits.astype(jnp.float32)) / GLA_TAU
        return (heads(q, GLA_HEADS) * GLA_DK ** -0.5, heads(k, GLA_HEADS), heads(v, GLA_HEADS),
                heads(log_a[:, :, 0], GLA_HEADS), heads(log_a[:, :, 1], GLA_HEADS))

    q, k, v, z, og = lat
    qc, kc, vc, zc, ogc = ctx
    o, oc = bidirectional_recurrence(*prep(q, k, v, z), *prep(qc, kc, vc, zc))
    y = merge_heads(rms_norm(o)) * jax.nn.silu(og)
    yc = merge_heads(rms_norm(oc)) * jax.nn.silu(ogc) if need_ctx else None
    return y, yc


def retention_mixer(lat, ctx, cos_r, sin_r, need_ctx):
    q, k, v, g = lat
    qc, kc, vc, gc = ctx
    sk = RET_DK ** -0.5
    q = rotate_pairs(heads(q, RET_HEADS), cos_r, sin_r)
    k = rotate_pairs(heads(k, RET_HEADS), cos_r, sin_r) * sk
    v = heads(v, RET_HEADS)
    qc = heads(qc, RET_HEADS)
    kc = heads(kc, RET_HEADS) * sk
    vc = heads(vc, RET_HEADS)
    ld_f = retention_log_decay(0.0)
    ld_b = retention_log_decay(1.0)

    def decay(ld, t):
        return jnp.broadcast_to(ld[None, :, None, None], t.shape)

    o, oc = bidirectional_recurrence(q, k, v, decay(ld_f, q), decay(ld_b, q),
                                     qc, kc, vc, decay(ld_f, qc), decay(ld_b, qc))
    y = merge_heads(rms_norm(o)) * jax.nn.silu(g)
    yc = merge_heads(rms_norm(oc)) * jax.nn.silu(gc) if need_ctx else None
    return y, yc


def token_mixers(h, hc, w_in, q_norm, w_uq, kv_norm, w_ukv, gla_w2, gla_b, cos_a, sin_a, cos_r, sin_r, need_ctx):
    cuts = np.cumsum(IN_SPLITS)[:-1].tolist()
    lat = jnp.split(h @ w_in, cuts, axis=-1)
    ctx = jnp.split(hc @ w_in, cuts, axis=-1)
    y_a, yc_a = mla_mixer(lat[0:3], ctx[0:3], q_norm, w_uq, kv_norm, w_ukv, cos_a, sin_a, need_ctx)
    y_b, yc_b = gla_mixer(lat[3:8], ctx[3:8], gla_w2, gla_b, need_ctx)
    y_c, yc_c = retention_mixer(lat[8:12], ctx[8:12], cos_r, sin_r, need_ctx)
    y = jnp.concatenate([y_a, y_b, y_c], axis=-1)
    yc = jnp.concatenate([yc_a, yc_b, yc_c], axis=-1) if need_ctx else None
    return y, yc


def expert_choice_ffn(h, router_w, w_gate, w_up, w_down):
    b, n, _ = h.shape
    cap = EC_CAPACITY * n // N_EXPERTS
    aff = jax.nn.softmax((h @ router_w).astype(jnp.float32), axis=-1)
    gate, idx = lax.top_k(jnp.swapaxes(aff, 1, 2), cap)
    bidx = jnp.arange(b)[:, None, None]
    xs = h[bidx, idx]
    a = jnp.einsum('becd,edf->becf', xs, w_gate)
    u = jnp.einsum('becd,edf->becf', xs, w_up)
    y = jnp.einsum('becf,efd->becd', jax.nn.silu(a) * u, w_down) * gate[..., None].astype(h.dtype)
    return jnp.zeros_like(h).at[bidx, idx].add(y)


def setup_inputs(seed: int = 0) -> dict:
    key = jax.random.key(seed)
    ks = jax.random.split(key, 19)

    def nrm(k, shape, s):
        return jax.random.normal(k, shape, jnp.float32) * s

    L = DEPTH
    D = D_MODEL
    return {
        'x': nrm(ks[0], (BATCH, SEQ, D), 1.0),
        'c': nrm(ks[1], (BATCH, D), 1.0),
        'ctx': nrm(ks[2], (BATCH, CTX_LEN, D), 1.0),
        'c_ctx': nrm(ks[3], (D,), 1.0),
        'ada_w': nrm(ks[4], (L, D, 6 * D), 0.5 * D ** -0.5),
        'ada_b': nrm(ks[5], (L, 6 * D), 0.01),
        'w_in': nrm(ks[6], (L, D, IN_COLS), D ** -0.5),
        'mla_q_norm': 1.0 + nrm(ks[7], (L, MLA_Q_RANK), 0.02),
        'mla_w_uq': nrm(ks[8], (L, MLA_Q_RANK, MLA_HEADS * (MLA_NOPE + MLA_ROPE)), MLA_Q_RANK ** -0.5),
        'mla_kv_norm': 1.0 + nrm(ks[9], (L, MLA_KV_RANK), 0.02),
        'mla_w_ukv': nrm(ks[10], (L, MLA_KV_RANK, MLA_HEADS * (MLA_NOPE + MLA_V)), MLA_KV_RANK ** -0.5),
        'gla_gate_w2': nrm(ks[11], (L, 2, GLA_GATE_RANK, GLA_HEADS * GLA_DK), GLA_GATE_RANK ** -0.5),
        'gla_gate_b': nrm(ks[12], (L, 2, GLA_HEADS * GLA_DK), 0.1),
        'w_out': nrm(ks[13], (L, MIX_WIDTH, D), MIX_WIDTH ** -0.5),
        'router_w': nrm(ks[14], (L, D, N_EXPERTS), D ** -0.5),
        'exp_w_gate': nrm(ks[15], (L, N_EXPERTS, D, EXPERT_FF), D ** -0.5),
        'exp_w_up': nrm(ks[16], (L, N_EXPERTS, D, EXPERT_FF), D ** -0.5),
        'exp_w_down': nrm(ks[17], (L, N_EXPERTS, EXPERT_FF, D), EXPERT_FF ** -0.5),
        'final_norm': 1.0 + nrm(ks[18], (D,), 0.02),
    }


def reference(x, c, ctx, c_ctx, ada_w, ada_b, w_in, mla_q_norm, mla_w_uq, mla_kv_norm, mla_w_ukv,
              gla_gate_w2, gla_gate_b, w_out, router_w, exp_w_gate, exp_w_up, exp_w_down, final_norm):
    n_lat = x.shape[1]
    rows = n_lat // GRID_W
    cos_a, sin_a = axial_rope_angles(rows)
    cos_r, sin_r = retention_angles(n_lat)
    silu_c = jax.nn.silu(c)
    silu_cc = jax.nn.silu(c_ctx)
    for l in range(DEPTH):
        need_ctx = l < DEPTH - 1
        mod = (silu_c @ ada_w[l] + ada_b[l])[:, None, :]
        mod_c = silu_cc @ ada_w[l] + ada_b[l]
        sh_a, sc_a, g_a, sh_f, sc_f, g_f = jnp.split(mod, 6, axis=-1)
        csh_a, csc_a, cg_a, csh_f, csc_f, cg_f = jnp.split(mod_c, 6, axis=-1)

        h = rms_norm(x) * (1.0 + sc_a) + sh_a
        hc = rms_norm(ctx) * (1.0 + csc_a) + csh_a
        y, yc = token_mixers(h, hc, w_in[l], mla_q_norm[l], mla_w_uq[l], mla_kv_norm[l], mla_w_ukv[l],
                             gla_gate_w2[l], gla_gate_b[l], cos_a, sin_a, cos_r, sin_r, need_ctx)
        x = x + g_a * (y @ w_out[l])
        h = rms_norm(x) * (1.0 + sc_f) + sh_f
        x = x + g_f * expert_choice_ffn(h, router_w[l], exp_w_gate[l], exp_w_up[l], exp_w_down[l])
        if need_ctx:
            ctx = ctx + cg_a * (yc @ w_out[l])
            hc = rms_norm(ctx) * (1.0 + csc_f) + csh_f
            ctx = ctx + cg_f * expert_choice_ffn(hc, router_w[l], exp_w_gate[l], exp_w_up[l], exp_w_down[l])
    return rms_norm(x, final_norm)
```

```python
import functools

import numpy as np
import jax
import jax.numpy as jnp
from jax import lax
from jax.experimental import pallas as pl
from jax.experimental.pallas import tpu as pltpu

f32 = jnp.float32
bf16 = jnp.bfloat16

D = 2048
BATCH = 4
N_LAT = 2048
N_CTX = 256
S = N_CTX + N_LAT
R = BATCH * S
DEPTH = 2
GRID_W = 64
EPS = 1e-6
ROPE_BASE = 10000.0
CHUNK = 64

MLA_H, MLA_QR, MLA_KVR, MLA_NOPE, MLA_ROPE, MLA_V = 8, 512, 256, 128, 64, 128
GLA_H, GLA_DK, GLA_DV, GLA_RANK, GLA_TAU = 4, 64, 128, 16, 16.0
RET_H, RET_DK, RET_DV, RET_EXP0 = 4, 128, 128, 5.0
N_EXP, EXP_FF, EC_CAP = 16, 2048, 2
CAP_LAT = EC_CAP * N_LAT // N_EXP
CAP_CTX = EC_CAP * N_CTX // N_EXP
SLOTS = CAP_LAT + CAP_CTX

TM = 256
TILES = S // TM
HP = 256

C_RQ, C_RK, C_RV, C_RG = 0, 512, 1024, 1536
C_GV, C_GOG, C_CQ, C_CKV, C_GQ, C_GK, C_KRZ = 2048, 2560, 3072, 3584, 3840, 4096, 4352
NC = 4608
TN_IN = 1536

VMEM_LIMIT = 56 * 1024 * 1024


def _cp(sem):
    return pltpu.CompilerParams(dimension_semantics=sem, vmem_limit_bytes=VMEM_LIMIT)


def _nt(a, b):
    return lax.dot_general(a, b, (((1,), (1,)), ((), ())), preferred_element_type=f32)


def _tn(a, b):
    return lax.dot_general(a, b, (((0,), (0,)), ((), ())), preferred_element_type=f32)


def _dot(a, b):
    return jnp.dot(a, b, preferred_element_type=f32)


def _rms(x):
    return x * lax.rsqrt(jnp.mean(x * x, axis=-1, keepdims=True) + EPS)


def _silu(x):
    return x * (1.0 / (1.0 + jnp.exp(-x)))


def _mod_row(i):
    return jnp.where(i % TILES == 0, BATCH, i // TILES)


def _mod_kernel(s_ref, w_ref, b_ref, o_ref):
    s = _silu(s_ref[...]).astype(bf16)
    o_ref[...] = _dot(s, w_ref[...].astype(bf16)) + b_ref[...]


def _modulation(cc, ada_w, ada_b):
    tn = 1024
    return pl.pallas_call(
        _mod_kernel,
        grid=(DEPTH, 6 * D // tn),
        in_specs=[pl.BlockSpec((8, D), lambda l, j: (0, 0)),
                  pl.BlockSpec((None, D, tn), lambda l, j: (l, 0, j)),
                  pl.BlockSpec((None, 1, tn), lambda l, j: (l, 0, j))],
        out_specs=pl.BlockSpec((None, 8, tn), lambda l, j: (l, 0, j)),
        out_shape=jax.ShapeDtypeStruct((DEPTH, 8, 6 * D), f32),
        compiler_params=_cp(("arbitrary", "arbitrary")),
        name="modulation",
    )(cc, ada_w, ada_b.reshape(DEPTH, 1, 6 * D))


def _in_kernel(x_ref, mod_ref, w_ref, o_ref):
    h = _rms(x_ref[...]) * (1.0 + mod_ref[1:2, :]) + mod_ref[0:1, :]
    o_ref[...] = _dot(h.astype(bf16), w_ref[...]).astype(bf16)


def _in_proj(xa, mods, w_in_p, layer):
    return pl.pallas_call(
        _in_kernel,
        grid=(NC // TN_IN, R // TM),
        in_specs=[pl.BlockSpec((TM, D), lambda j, i: (i, 0)),
                  pl.BlockSpec((None, None, 6, D), lambda j, i: (layer, _mod_row(i), 0, 0)),
                  pl.BlockSpec((D, TN_IN), lambda j, i: (0, j))],
        out_specs=pl.BlockSpec((TM, TN_IN), lambda j, i: (i, j)),
        out_shape=jax.ShapeDtypeStruct((R, NC), bf16),
        compiler_params=_cp(("arbitrary", "arbitrary")),
        name="in_proj",
    )(xa, mods, w_in_p)


def _mla_prep_kernel(cq_ref, ckv_ref, krz_ref, tc_ref, ts_ref, qn_ref, kvn_ref, wq_ref, wkv_ref, e2_ref,
                     q_ref, k_ref, v_ref):
    tc = tc_ref[...]
    ts = ts_ref[...]
    scale = (MLA_NOPE + MLA_ROPE) ** -0.5
    hq = (_rms(cq_ref[...].astype(f32)) * qn_ref[...]).astype(bf16)
    q2 = _dot(hq, wq_ref[...])
    hkv = (_rms(ckv_ref[...].astype(f32)) * kvn_ref[...]).astype(bf16)
    kv = _dot(hkv, wkv_ref[...])
    kr2 = _dot(krz_ref[...], e2_ref[...])
    nq = MLA_H * HP
    for h in range(MLA_H):
        sl = slice(h * HP, (h + 1) * HP)
        sl2 = slice(nq + h * HP, nq + (h + 1) * HP)
        q_ref[:, sl] = ((q2[:, sl] * tc + q2[:, sl2] * ts) * scale).astype(bf16)
        k_ref[:, sl] = (kv[:, sl] + kr2[:, sl] * tc + kr2[:, sl2] * ts).astype(bf16)
    v_ref[...] = kv[:, nq:].astype(bf16)


def _mla_prep(p, tc, ts, qn, kvn, wq2, wkv, e2):
    nq = MLA_H * HP
    nv = MLA_H * MLA_V
    const = lambda shape: pl.BlockSpec(shape, lambda i: (0, 0))
    return pl.pallas_call(
        _mla_prep_kernel,
        grid=(R // TM,),
        in_specs=[pl.BlockSpec((TM, MLA_QR), lambda i: (i, C_CQ // MLA_QR)),
                  pl.BlockSpec((TM, MLA_KVR), lambda i: (i, C_CKV // MLA_KVR)),
                  pl.BlockSpec((TM, 128), lambda i: (i, C_KRZ // 128)),
                  pl.BlockSpec((TM, HP), lambda i: (i % TILES, 0)),
                  pl.BlockSpec((TM, HP), lambda i: (i % TILES, 0)),
                  const((1, MLA_QR)), const((1, MLA_KVR)),
                  const((MLA_QR, 2 * nq)), const((MLA_KVR, nq + nv)), const((128, 2 * nq))],
        out_specs=[pl.BlockSpec((TM, nq), lambda i: (i, 0)),
                   pl.BlockSpec((TM, nq), lambda i: (i, 0)),
                   pl.BlockSpec((TM, nv), lambda i: (i, 0))],
        out_shape=[jax.ShapeDtypeStruct((R, nq), bf16), jax.ShapeDtypeStruct((R, nq), bf16),
                   jax.ShapeDtypeStruct((R, nv), bf16)],
        compiler_params=_cp(("arbitrary",)),
        name="mla_prep",
    )(p, p, p, tc, ts, qn, kvn, wq2, wkv, e2)


def _attn_kernel(q_ref, k_ref, v_ref, o_ref):
    t = pl.program_id(2)

    def run(nk):
        s = _nt(q_ref[...], k_ref[0:nk, :])
        m = jnp.max(s, axis=-1, keepdims=True)
        p = jnp.exp(s - m)
        l = jnp.sum(p, axis=-1, keepdims=True)
        o = _dot(p.astype(bf16), v_ref[0:nk, :])
        o_ref[...] = (o * (1.0 / l)).astype(bf16)

    @pl.when(t == 0)
    def _():
        run(N_CTX)

    @pl.when(t > 0)
    def _():
        run(S)


def _attention(q, k, v):
    nq = MLA_H * HP
    nv = MLA_H * MLA_V
    return pl.pallas_call(
        _attn_kernel,
        grid=(BATCH, MLA_H, TILES),
        in_specs=[pl.BlockSpec((TM, HP), lambda b, h, t: (b * TILES + t, h)),
                  pl.BlockSpec((None, S, HP), lambda b, h, t: (b, 0, h)),
                  pl.BlockSpec((None, S, MLA_V), lambda b, h, t: (b, 0, h))],
        out_specs=pl.BlockSpec((TM, MLA_V), lambda b, h, t: (b * TILES + t, h)),
        out_shape=jax.ShapeDtypeStruct((R, nv), bf16),
        compiler_params=_cp(("arbitrary", "arbitrary", "arbitrary")),
        name="mla_attention",
    )(q, k.reshape(BATCH, S, nq), v.reshape(BATCH, S, nv))


N_CHUNK = S // CHUNK
N_CCH = N_CTX // CHUNK
SCAN_H = 4
SCAN_V = 128
SCAN_W = SCAN_H * SCAN_V


def _scan_consts(kh):
    dk = kh // SCAN_H
    row = lax.broadcasted_iota(jnp.int32, (CHUNK, SCAN_H * CHUNK), 0)
    col = lax.broadcasted_iota(jnp.int32, (CHUNK, SCAN_H * CHUNK), 1) % CHUNK
    incl = row >= col
    strict = col > row
    krow = lax.broadcasted_iota(jnp.int32, (SCAN_H * CHUNK, kh), 0) // CHUNK
    kcol = lax.broadcasted_iota(jnp.int32, (SCAN_H * CHUNK, kh), 1) // dk
    kmask = krow == kcol
    vrow = lax.broadcasted_iota(jnp.int32, (SCAN_H * CHUNK, SCAN_W), 0) // CHUNK
    vcol = lax.broadcasted_iota(jnp.int32, (SCAN_H * CHUNK, SCAN_W), 1) // SCAN_V
    vmask = vrow == vcol
    srow = lax.broadcasted_iota(jnp.int32, (SCAN_W, kh), 0) // SCAN_V
    scol = lax.broadcasted_iota(jnp.int32, (SCAN_W, kh), 1) // dk
    smask = srow == scol
    return incl, strict, kmask, vmask, smask


def _chunk_step(q, k, v, cum, cend, st_ref, amask, kmask, vmask, smask):
    qd = (q * jnp.exp(cum)).astype(bf16)
    ki = k * jnp.exp(-cum)
    kend = (k * jnp.exp(cend - cum)).astype(bf16)
    dec = jnp.exp(cend)
    kst = jnp.where(kmask, jnp.concatenate([ki] * SCAN_H, axis=0), 0.0).astype(bf16)
    att = jnp.where(amask, _nt(qd, kst), 0.0).astype(bf16)
    vbd = jnp.where(vmask, jnp.concatenate([v] * SCAN_H, axis=0), jnp.zeros((), bf16))
    st = st_ref[...]
    o = _dot(att, vbd) + _nt(qd, st.astype(bf16))
    st_ref[...] = st * dec + jnp.where(smask, _tn(v, kend), 0.0)
    return o


def _bwd_chunk(i):
    return jnp.where(i < N_CCH, N_CCH - 1 - i, N_CHUNK + N_CCH - 1 - i)


def _scan_finish(of_sc, ob_sc, g_ref, y_ref):
    def fin(i, carry):
        r0 = pl.multiple_of(i * TM, TM)
        o = of_sc[pl.ds(r0, TM), :] + ob_sc[pl.ds(r0, TM), :]
        g = g_ref[pl.ds(r0, TM), :].astype(f32)
        for h in range(SCAN_H):
            sl = slice(h * SCAN_V, (h + 1) * SCAN_V)
            y_ref[pl.ds(r0, TM), sl] = (_rms(o[:, sl]) * _silu(g[:, sl])).astype(bf16)
        return carry

    lax.fori_loop(0, TILES, fin, 0)


def _gla_kernel(q_ref, k_ref, v_ref, krz_ref, og_ref, w2_ref, b2_ref, y_ref, cum_sc, of_sc, ob_sc, stf_sc, stb_sc):
    kh = GLA_H * GLA_DK
    ri = lax.broadcasted_iota(jnp.int32, (TM, TM), 0)
    ci = lax.broadcasted_iota(jnp.int32, (TM, TM), 1)
    same = (ri // CHUNK) == (ci // CHUNK)
    pre = jnp.where(same & (ci <= ri), 1.0, 0.0).astype(bf16)
    suf = jnp.where(same & (ci >= ri), 1.0, 0.0).astype(bf16)

    def exact_sum(m, x):
        hi = x.astype(bf16)
        r1 = x - hi.astype(f32)
        mid = r1.astype(bf16)
        lo = (r1 - mid.astype(f32)).astype(bf16)
        return _dot(m, hi) + _dot(m, mid) + _dot(m, lo)

    def gates(i, carry):
        r0 = pl.multiple_of(i * TM, TM)
        lg = _dot(krz_ref[pl.ds(r0, TM), :], w2_ref[...]) + b2_ref[...]
        la = (jnp.minimum(lg, 0.0) - jnp.log1p(jnp.exp(-jnp.abs(lg)))) * (1.0 / GLA_TAU)
        cum_sc[pl.ds(r0, TM), 0:kh] = exact_sum(pre, la[:, 0:kh])
        cum_sc[pl.ds(r0, TM), kh:2 * kh] = exact_sum(suf, la[:, kh:2 * kh])
        return carry

    lax.fori_loop(0, TILES, gates, 0)

    incl, strict, kmask, vmask, smask = _scan_consts(kh)
    stf_sc[...] = jnp.zeros_like(stf_sc)
    stb_sc[...] = jnp.zeros_like(stb_sc)
    qscale = GLA_DK ** -0.5

    def body(i, carry):
        rf = pl.multiple_of(i * CHUNK, CHUNK)
        cum = cum_sc[pl.ds(rf, CHUNK), 0:kh]
        of_sc[pl.ds(rf, CHUNK), :] = _chunk_step(
            q_ref[pl.ds(rf, CHUNK), :].astype(f32) * qscale, k_ref[pl.ds(rf, CHUNK), :].astype(f32),
            v_ref[pl.ds(rf, CHUNK), :], cum, cum[CHUNK - 1:CHUNK, :], stf_sc, incl, kmask, vmask, smask)
        rb = pl.multiple_of(_bwd_chunk(i) * CHUNK, CHUNK)
        rc = cum_sc[pl.ds(rb, CHUNK), kh:2 * kh]
        ob_sc[pl.ds(rb, CHUNK), :] = _chunk_step(
            q_ref[pl.ds(rb, CHUNK), :].astype(f32) * qscale, k_ref[pl.ds(rb, CHUNK), :].astype(f32),
            v_ref[pl.ds(rb, CHUNK), :], rc, rc[0:1, :], stb_sc, strict, kmask, vmask, smask)
        return carry

    lax.fori_loop(0, N_CHUNK, body, 0)
    _scan_finish(of_sc, ob_sc, og_ref, y_ref)


def _gla(p, w2p, b2):
    kh = GLA_H * GLA_DK
    p3 = p.reshape(BATCH, S, NC)
    col = lambda w, c: pl.BlockSpec((None, S, w), lambda b: (b, 0, c // w))
    return pl.pallas_call(
        _gla_kernel,
        grid=(BATCH,),
        in_specs=[col(kh, C_GQ), col(kh, C_GK), col(SCAN_W, C_GV), col(128, C_KRZ), col(SCAN_W, C_GOG),
                  pl.BlockSpec((128, 2 * kh), lambda b: (0, 0)), pl.BlockSpec((1, 2 * kh), lambda b: (0, 0))],
        out_specs=pl.BlockSpec((None, S, SCAN_W), lambda b: (b, 0, 0)),
        out_shape=jax.ShapeDtypeStruct((BATCH, S, SCAN_W), bf16),
        scratch_shapes=[pltpu.VMEM((S, 2 * kh), f32), pltpu.VMEM((S, SCAN_W), f32), pltpu.VMEM((S, SCAN_W), f32),
                        pltpu.VMEM((SCAN_W, kh), f32), pltpu.VMEM((SCAN_W, kh), f32)],
        compiler_params=_cp(("arbitrary",)),
        name="gla",
    )(p3, p3, p3, p3, p3, w2p, b2).reshape(R, SCAN_W)


def _ret_kernel(q_ref, k_ref, v_ref, g_ref, cos_ref, sin_ref, cumf_ref, cumb_ref, y_ref, of_sc, ob_sc, stf_sc, stb_sc):
    kh = RET_H * RET_DK
    incl, strict, kmask, vmask, smask = _scan_consts(kh)
    stf_sc[...] = jnp.zeros_like(stf_sc)
    stb_sc[...] = jnp.zeros_like(stb_sc)
    kscale = RET_DK ** -0.5
    cumf = cumf_ref[...]
    cumb = cumb_ref[...]

    def rotate(x, r0):
        cos = jnp.concatenate([cos_ref[pl.ds(r0, CHUNK), :]] * RET_H, axis=1)
        sin = jnp.concatenate([sin_ref[pl.ds(r0, CHUNK), :]] * RET_H, axis=1)
        partner = jnp.concatenate(
            [pltpu.roll(x[:, h * RET_DK:(h + 1) * RET_DK], RET_DK // 2, axis=1) for h in range(RET_H)], axis=1)
        return x * cos + partner * sin

    def body(i, carry):
        rf = pl.multiple_of(i * CHUNK, CHUNK)
        of_sc[pl.ds(rf, CHUNK), :] = _chunk_step(
            rotate(q_ref[pl.ds(rf, CHUNK), :].astype(f32), rf),
            rotate(k_ref[pl.ds(rf, CHUNK), :].astype(f32), rf) * kscale,
            v_ref[pl.ds(rf, CHUNK), :], cumf, cumf[CHUNK - 1:CHUNK, :], stf_sc, incl, kmask, vmask, smask)
        rb = pl.multiple_of(_bwd_chunk(i) * CHUNK, CHUNK)
        ob_sc[pl.ds(rb, CHUNK), :] = _chunk_step(
            rotate(q_ref[pl.ds(rb, CHUNK), :].astype(f32), rb),
            rotate(k_ref[pl.ds(rb, CHUNK), :].astype(f32), rb) * kscale,
            v_ref[pl.ds(rb, CHUNK), :], cumb, cumb[0:1, :], stb_sc, strict, kmask, vmask, smask)
        return carry

    lax.fori_loop(0, N_CHUNK, body, 0)
    _scan_finish(of_sc, ob_sc, g_ref, y_ref)


def _retention(p, cos_r, sin_r, cumf, cumb):
    kh = RET_H * RET_DK
    p3 = p.reshape(BATCH, S, NC)
    col = lambda w, c: pl.BlockSpec((None, S, w), lambda b: (b, 0, c // w))
    const = lambda shape: pl.BlockSpec(shape, lambda b: (0, 0))
    return pl.pallas_call(
        _ret_kernel,
        grid=(BATCH,),
        in_specs=[col(kh, C_RQ), col(kh, C_RK), col(SCAN_W, C_RV), col(SCAN_W, C_RG),
                  const((S, RET_DK)), const((S, RET_DK)), const((CHUNK, kh)), const((CHUNK, kh))],
        out_specs=pl.BlockSpec((None, S, SCAN_W), lambda b: (b, 0, 0)),
        out_shape=jax.ShapeDtypeStruct((BATCH, S, SCAN_W), bf16),
        scratch_shapes=[pltpu.VMEM((S, SCAN_W), f32), pltpu.VMEM((S, SCAN_W), f32),
                        pltpu.VMEM((SCAN_W, kh), f32), pltpu.VMEM((SCAN_W, kh), f32)],
        compiler_params=_cp(("arbitrary",)),
        name="retention",
    )(p3, p3, p3, p3, cos_r, sin_r, cumf, cumb).reshape(R, SCAN_W)


def _out_kernel(x_ref, ya_ref, yb_ref, yc_ref, mod_ref, w_ref, rwt_ref, xo_ref, h2_ref, aff_ref):
    na = MLA_H * MLA_V
    acc = _dot(ya_ref[...], w_ref[0:na, :])
    acc += _dot(yb_ref[...], w_ref[na:na + SCAN_W, :])
    acc += _dot(yc_ref[...], w_ref[na + SCAN_W:, :])
    x = x_ref[...] + mod_ref[2:3, :] * acc
    xo_ref[...] = x
    hb = (_rms(x) * (1.0 + mod_ref[4:5, :]) + mod_ref[3:4, :]).astype(bf16)
    h2_ref[...] = hb
    lg = _nt(rwt_ref[...], hb)
    e = jnp.exp(lg - jnp.max(lg, axis=0, keepdims=True))
    aff_ref[...] = e / jnp.sum(e, axis=0, keepdims=True)


def _out_proj(xa, ya, yb, yc, mods, w_out, rwt, layer):
    na = MLA_H * MLA_V
    row = lambda w: pl.BlockSpec((TM, w), lambda i: (i, 0))
    return pl.pallas_call(
        _out_kernel,
        grid=(R // TM,),
        in_specs=[row(D), row(na), row(SCAN_W), row(SCAN_W),
                  pl.BlockSpec((None, None, 6, D), lambda i: (layer, _mod_row(i), 0, 0)),
                  pl.BlockSpec((D, D), lambda i: (0, 0)),
                  pl.BlockSpec((N_EXP, D), lambda i: (0, 0))],
        out_specs=[row(D), row(D), pl.BlockSpec((N_EXP, TM), lambda i: (0, i))],
        out_shape=[jax.ShapeDtypeStruct((R, D), f32), jax.ShapeDtypeStruct((R, D), bf16),
                   jax.ShapeDtypeStruct((N_EXP, R), f32)],
        compiler_params=_cp(("arbitrary",)),
        name="out_proj",
    )(xa, ya, yb, yc, mods, w_out, rwt)


def _topk_kernel(aff_ref, pos_ref, post_ref):
    ri = lax.broadcasted_iota(jnp.int32, (TM, TM), 0)
    ci = lax.broadcasted_iota(jnp.int32, (TM, TM), 1)
    before = jnp.where(ri < ci, 1.0, 0.0).astype(bf16)

    def prefix_count(m):
        out = []
        off = jnp.zeros((N_EXP, 1), f32)
        for blk in range(m.shape[1] // TM):
            mb = m[:, blk * TM:(blk + 1) * TM]
            out.append(_dot(mb.astype(bf16), before) + off)
            off = off + jnp.sum(mb, axis=1, keepdims=True)
        return jnp.concatenate(out, axis=1) if len(out) > 1 else out[0]

    def select(a, cap, base):
        bits = pltpu.bitcast(a, jnp.int32)
        capf = float(cap)

        def step(i, thr):
            cand = thr | jnp.left_shift(jnp.int32(1), 30 - i)
            cnt = jnp.sum(jnp.where(bits >= cand, 1.0, 0.0), axis=1, keepdims=True)
            return jnp.where(cnt >= capf, cand, thr)

        thr = lax.fori_loop(0, 31, step, jnp.zeros((N_EXP, 1), jnp.int32))
        gt = jnp.where(bits > thr, 1.0, 0.0)
        eq = jnp.where(bits == thr, 1.0, 0.0)
        need = capf - jnp.sum(gt, axis=1, keepdims=True)
        keep = gt + eq * jnp.where(prefix_count(eq) < need, 1.0, 0.0)
        return jnp.where(keep > 0.5, prefix_count(keep) + float(base), -1.0)

    a = aff_ref[...]
    pos = jnp.concatenate([select(a[:, 0:N_CTX], CAP_CTX, CAP_LAT), select(a[:, N_CTX:], CAP_LAT, 0)], axis=1)
    pos_ref[...] = pos
    post_ref[...] = jnp.concatenate([pos, jnp.full((128 - N_EXP, S), -1.0, f32)], axis=0).T


def _topk(aff):
    return pl.pallas_call(
        _topk_kernel,
        grid=(BATCH,),
        in_specs=[pl.BlockSpec((N_EXP, S), lambda b: (0, b))],
        out_specs=[pl.BlockSpec((N_EXP, S), lambda b: (0, b)), pl.BlockSpec((S, 128), lambda b: (b, 0))],
        out_shape=[jax.ShapeDtypeStruct((N_EXP, R), f32), jax.ShapeDtypeStruct((R, 128), f32)],
        compiler_params=_cp(("arbitrary",)),
        name="route_topk",
    )(aff)


def _gather_kernel(pos_ref, aff_ref, h_ref, x_ref, g_ref):
    e = pl.program_id(1)
    prow = pos_ref[pl.ds(e, 1), :]
    arow = aff_ref[pl.ds(e, 1), :]
    slot = lax.broadcasted_iota(jnp.int32, (SLOTS, S), 0).astype(f32)
    hit = prow == slot
    onehot = jnp.where(hit, 1.0, 0.0).astype(bf16)
    x_ref[...] = _dot(onehot, h_ref[...]).astype(bf16)
    g = jnp.sum(jnp.where(hit, arow, 0.0), axis=1, keepdims=True)
    g_ref[...] = jnp.broadcast_to(g, (SLOTS, 128))


def _gather(pos, aff, h2):
    return pl.pallas_call(
        _gather_kernel,
        grid=(BATCH, N_EXP),
        in_specs=[pl.BlockSpec((N_EXP, S), lambda b, e: (0, b)),
                  pl.BlockSpec((N_EXP, S), lambda b, e: (0, b)),
                  pl.BlockSpec((S, D), lambda b, e: (b, 0))],
        out_specs=[pl.BlockSpec((None, SLOTS, D), lambda b, e: (e, b, 0)),
                   pl.BlockSpec((None, SLOTS, 128), lambda b, e: (e, b, 0))],
        out_shape=[jax.ShapeDtypeStruct((N_EXP, BATCH * SLOTS, D), bf16),
                   jax.ShapeDtypeStruct((N_EXP, BATCH * SLOTS, 128), f32)],
        compiler_params=_cp(("arbitrary", "arbitrary")),
        name="moe_gather",
    )(pos, aff, h2)


TF = 256


def _moe_kernel(x_ref, wg_ref, wu_ref, wd_ref, g_ref, y_ref, acc_ref):
    f = pl.program_id(1)
    x = x_ref[...]
    a = _dot(x, wg_ref[...].astype(bf16))
    u = _dot(x, wu_ref[...].astype(bf16))
    hm = (_silu(a) * u).astype(bf16)
    part = _dot(hm, wd_ref[...].astype(bf16))

    @pl.when(f == 0)
    def _():
        acc_ref[...] = part

    @pl.when(f > 0)
    def _():
        acc_ref[...] += part

    @pl.when(f == EXP_FF // TF - 1)
    def _():
        y_ref[...] = (acc_ref[...] * g_ref[:, 0:1]).astype(bf16)


def _moe_ffn(xg, gs, w_gate, w_up, w_down, layer):
    rows = BATCH * SLOTS
    return pl.pallas_call(
        _moe_kernel,
        grid=(N_EXP, EXP_FF // TF),
        in_specs=[pl.BlockSpec((None, rows, D), lambda e, f: (e, 0, 0)),
                  pl.BlockSpec((None, None, D, TF), lambda e, f: (layer, e, 0, f)),
                  pl.BlockSpec((None, None, D, TF), lambda e, f: (layer, e, 0, f)),
                  pl.BlockSpec((None, None, TF, D), lambda e, f: (layer, e, f, 0)),
                  pl.BlockSpec((None, rows, 128), lambda e, f: (e, 0, 0))],
        out_specs=pl.BlockSpec((None, rows, D), lambda e, f: (e, 0, 0)),
        out_shape=jax.ShapeDtypeStruct((N_EXP, rows, D), bf16),
        scratch_shapes=[pltpu.VMEM((rows, D), f32)],
        compiler_params=_cp(("arbitrary", "arbitrary")),
        name="moe_ffn",
    )(xg, w_gate, w_up, w_down, gs)


TN_C = 1024


def _combine_kernel(post_ref, ys_ref, x_ref, mod_ref, o_ref):
    t = pl.program_id(2)
    pb = post_ref[...]

    def scatter(cap, base, ys):
        pc = pb - float(base)
        pc = jnp.where((pc >= 0.0) & (pc < float(cap)), pc, -1.0).astype(bf16)
        er = lax.broadcasted_iota(jnp.int32, (128, N_EXP * cap), 0)
        ec = lax.broadcasted_iota(jnp.int32, (128, N_EXP * cap), 1) // cap
        rep = jnp.where(er == ec, 1.0, 0.0).astype(bf16)
        slot = (lax.broadcasted_iota(jnp.int32, (TM, N_EXP * cap), 1) % cap).astype(f32)
        onehot = jnp.where(_dot(pc, rep) == slot, 1.0, 0.0).astype(bf16)
        o_ref[...] = x_ref[...] + mod_ref[5:6, :] * _dot(onehot, ys)

    @pl.when(t == 0)
    def _():
        scatter(CAP_CTX, CAP_LAT, ys_ref[:, CAP_LAT:SLOTS, :].reshape(N_EXP * CAP_CTX, TN_C))

    @pl.when(t > 0)
    def _():
        scatter(CAP_LAT, 0, ys_ref[:, 0:CAP_LAT, :].reshape(N_EXP * CAP_LAT, TN_C))


def _combine(post, ys, xa, mods, layer):
    return pl.pallas_call(
        _combine_kernel,
        grid=(BATCH, D // TN_C, TILES),
        in_specs=[pl.BlockSpec((TM, 128), lambda b, n, t: (b * TILES + t, 0)),
                  pl.BlockSpec((N_EXP, None, SLOTS, TN_C), lambda b, n, t: (0, b, 0, n)),
                  pl.BlockSpec((TM, TN_C), lambda b, n, t: (b * TILES + t, n)),
                  pl.BlockSpec((None, None, 6, TN_C), lambda b, n, t: (layer, jnp.where(t == 0, BATCH, b), 0, n))],
        out_specs=pl.BlockSpec((TM, TN_C), lambda b, n, t: (b * TILES + t, n)),
        out_shape=jax.ShapeDtypeStruct((R, D), f32),
        compiler_params=_cp(("arbitrary", "arbitrary", "arbitrary")),
        name="moe_combine",
    )(post, ys.reshape(N_EXP, BATCH, SLOTS, D), xa, mods)


def _final_kernel(x_ref, g_ref, o_ref):
    o_ref[...] = _rms(x_ref[...]) * g_ref[...]


def _final_norm(xa, gain):
    lt = N_LAT // TM
    return pl.pallas_call(
        _final_kernel,
        grid=(BATCH, lt),
        in_specs=[pl.BlockSpec((TM, D), lambda b, t: (b * TILES + 1 + t, 0)),
                  pl.BlockSpec((1, D), lambda b, t: (0, 0))],
        out_specs=pl.BlockSpec((None, TM, D), lambda b, t: (b, t, 0)),
        out_shape=jax.ShapeDtypeStruct((BATCH, N_LAT, D), f32),
        compiler_params=_cp(("arbitrary", "arbitrary")),
        name="final_norm",
    )(xa, gain.reshape(1, D))


def _take_cols(w, src):
    src = np.asarray(src)
    return jnp.where(src >= 0, jnp.take(w, np.maximum(src, 0), axis=1), 0.0)


def _deinterleave(n):
    return np.concatenate([np.arange(0, n, 2), np.arange(1, n, 2)])


def _in_cols():
    o = np.cumsum((0, MLA_QR, MLA_KVR, MLA_ROPE, 256, 256, 512, 32, 512, 512, 512, 512))
    cq, ckv, kr, gq, gk, gv, gz, gog, rq, rk, rv, rg = o[:12]
    head_perm = np.concatenate([h * RET_DK + _deinterleave(RET_DK) for h in range(RET_H)])
    src = np.full((NC,), -1, np.int64)
    src[C_RQ:C_RQ + 512] = rq + head_perm
    src[C_RK:C_RK + 512] = rk + head_perm
    src[C_RV:C_RV + 512] = rv + np.arange(512)
    src[C_RG:C_RG + 512] = rg + np.arange(512)
    src[C_GV:C_GV + 512] = gv + np.arange(512)
    src[C_GOG:C_GOG + 512] = gog + np.arange(512)
    src[C_CQ:C_CQ + 512] = cq + np.arange(512)
    src[C_CKV:C_CKV + 256] = ckv + np.arange(256)
    src[C_GQ:C_GQ + 256] = gq + np.arange(256)
    src[C_GK:C_GK + 256] = gk + np.arange(256)
    src[C_KRZ:C_KRZ + 64] = kr + _deinterleave(MLA_ROPE)
    src[C_KRZ + 64:C_KRZ + 96] = gz + np.arange(32)
    return src


def _mla_cols():
    dq = MLA_NOPE + MLA_ROPE
    half = MLA_ROPE // 2
    nq = MLA_H * HP
    q_src = np.full((2 * nq,), -1, np.int64)
    for h in range(MLA_H):
        rope = h * dq + MLA_NOPE + _deinterleave(MLA_ROPE)
        q_src[h * HP:h * HP + MLA_NOPE] = h * dq + np.arange(MLA_NOPE)
        q_src[h * HP + MLA_NOPE:h * HP + dq] = rope
        q_src[nq + h * HP + MLA_NOPE:nq + h * HP + dq] = np.concatenate([rope[half:], rope[:half]])
    kv_src = np.full((nq + MLA_H * MLA_V,), -1, np.int64)
    for h in range(MLA_H):
        kv_src[h * HP:h * HP + MLA_NOPE] = h * (MLA_NOPE + MLA_V) + np.arange(MLA_NOPE)
        kv_src[nq + h * MLA_V:nq + (h + 1) * MLA_V] = h * (MLA_NOPE + MLA_V) + MLA_NOPE + np.arange(MLA_V)
    e2 = np.zeros((128, 2 * nq), np.float32)
    for h in range(MLA_H):
        for j in range(MLA_ROPE):
            e2[j, h * HP + MLA_NOPE + j] = 1.0
            e2[(j + half) % MLA_ROPE, nq + h * HP + MLA_NOPE + j] = 1.0
    return q_src, kv_src, e2


def _tables():
    rows = N_LAT // GRID_W
    row = np.repeat(np.arange(rows, dtype=np.float32), GRID_W)
    colp = np.tile(np.arange(GRID_W, dtype=np.float32), rows)
    n_freq = MLA_ROPE // 4
    inv = jnp.power(ROPE_BASE, -jnp.arange(n_freq, dtype=f32) / n_freq)
    ang = jnp.concatenate([row[:, None] * inv, colp[:, None] * inv], axis=-1)
    cos_a, sin_a = jnp.cos(ang), jnp.sin(ang)
    one = jnp.ones((N_LAT, MLA_NOPE), f32)
    zpad = jnp.zeros((N_LAT, HP - MLA_NOPE - MLA_ROPE), f32)
    tc_lat = jnp.concatenate([one, cos_a, cos_a, zpad], axis=1)
    ts_lat = jnp.concatenate([0 * one, -sin_a, sin_a, zpad], axis=1)
    tc_ctx = jnp.concatenate([jnp.ones((N_CTX, MLA_NOPE + MLA_ROPE), f32), jnp.zeros((N_CTX, HP - MLA_NOPE - MLA_ROPE), f32)], axis=1)
    tc = jnp.concatenate([tc_ctx, tc_lat], axis=0)
    ts = jnp.concatenate([jnp.zeros((N_CTX, HP), f32), ts_lat], axis=0)

    inv_r = 1.0 / jnp.power(ROPE_BASE, jnp.linspace(0.0, 1.0, RET_DK // 2, dtype=f32))
    ang_r = jnp.arange(N_LAT, dtype=f32)[:, None] * inv_r
    cos_r = jnp.concatenate([jnp.ones((N_CTX, RET_DK), f32), jnp.concatenate([jnp.cos(ang_r)] * 2, axis=1)], axis=0)
    sin_r = jnp.concatenate([jnp.zeros((N_CTX, RET_DK), f32),
                             jnp.concatenate([-jnp.sin(ang_r), jnp.sin(ang_r)], axis=1)], axis=0)

    def log_decay(direction):
        e = RET_EXP0 + direction + 2.0 * jnp.arange(RET_H, dtype=f32)
        return jnp.repeat(jnp.log1p(-jnp.exp2(-e)), RET_DK)[None, :]

    steps = jnp.arange(1, CHUNK + 1, dtype=f32)[:, None]
    cumf = steps * log_decay(0.0)
    cumb = steps[::-1] * log_decay(1.0)
    return tc, ts, cos_r, sin_r, cumf, cumb


def kernel(x, c, ctx, c_ctx, ada_w, ada_b, w_in, mla_q_norm, mla_w_uq, mla_kv_norm, mla_w_ukv, gla_gate_w2,
           gla_gate_b, w_out, router_w, exp_w_gate, exp_w_up, exp_w_down, final_norm):
    xa = jnp.concatenate([ctx, x], axis=1).reshape(R, D)
    cc = jnp.concatenate([c, c_ctx[None, :], jnp.zeros((8 - BATCH - 1, D), f32)], axis=0)
    mods = _modulation(cc, ada_w, ada_b).reshape(DEPTH, 8, 6, D)
    tc, ts, cos_r, sin_r, cumf, cumb = _tables()
    in_src = _in_cols()
    q_src, kv_src, e2 = _mla_cols()
    e2 = jnp.asarray(e2, bf16)
    kh = GLA_H * GLA_DK

    for l in range(DEPTH):
        w_in_p = _take_cols(w_in[l], in_src).astype(bf16)
        wq2 = _take_cols(mla_w_uq[l], q_src).astype(bf16)
        wkv = _take_cols(mla_w_ukv[l], kv_src).astype(bf16)
        w2p = jnp.zeros((128, 2 * kh), f32)
        w2p = w2p.at[64:64 + GLA_RANK, 0:kh].set(gla_gate_w2[l, 0]).at[64 + GLA_RANK:64 + 2 * GLA_RANK, kh:].set(gla_gate_w2[l, 1])
        b2 = gla_gate_b[l].reshape(1, 2 * kh)

        p = _in_proj(xa, mods, w_in_p, l)
        q, k, v = _mla_prep(p, tc, ts, mla_q_norm[l].reshape(1, -1), mla_kv_norm[l].reshape(1, -1), wq2, wkv, e2)
        ya = _attention(q, k, v)
        yb = _gla(p, w2p.astype(bf16), b2)
        yc = _retention(p, cos_r, sin_r, cumf, cumb)
        xa, h2, aff = _out_proj(xa, ya, yb, yc, mods, w_out[l].astype(bf16), router_w[l].T.astype(bf16), l)
        pos, post = _topk(aff)
        xg, gs = _gather(pos, aff, h2)
        ys = _moe_ffn(xg, gs, exp_w_gate, exp_w_up, exp_w_down, l)
        xa = _combine(post, ys, xa, mods, l)
    return _final_norm(xa, final_norm)
```

```python
import functools

import numpy as np
import jax
import jax.numpy as jnp
from jax import lax
from jax.experimental import pallas as pl
from jax.experimental.pallas import tpu as pltpu

f32 = jnp.float32
bf16 = jnp.bfloat16

D = 2048
BATCH = 4
N_LAT = 2048
N_CTX = 256
S = N_CTX + N_LAT
R = BATCH * S
DEPTH = 2
GRID_W = 64
EPS = 1e-6
LOG2E = 1.4426950408889634
ROPE_BASE = 10000.0
CHUNK = 64

MLA_H, MLA_QR, MLA_KVR, MLA_NOPE, MLA_ROPE, MLA_V = 8, 512, 256, 128, 64, 128
GLA_H, GLA_DK, GLA_DV, GLA_RANK, GLA_TAU = 4, 64, 128, 16, 16.0
RET_H, RET_DK, RET_DV, RET_EXP0 = 4, 128, 128, 5.0
N_EXP, EXP_FF, EC_CAP = 16, 2048, 2
CAP_LAT = EC_CAP * N_LAT // N_EXP
CAP_CTX = EC_CAP * N_CTX // N_EXP
SLOTS = CAP_LAT + CAP_CTX

TM = 256
TILES = S // TM
HP = 256

C_RQ, C_RK, C_RV, C_RG = 0, 512, 1024, 1536
C_GV, C_GOG, C_CQ, C_CKV, C_GQ, C_GK, C_KRZ = 2048, 2560, 3072, 3584, 3840, 4096, 4352
NC = 4608
TN_IN = 1536

VMEM_LIMIT = 56 * 1024 * 1024


def _cp(sem):
    return pltpu.CompilerParams(dimension_semantics=sem, vmem_limit_bytes=VMEM_LIMIT)


def _nt(a, b):
    return lax.dot_general(a, b, (((1,), (1,)), ((), ())), preferred_element_type=f32)


def _tn(a, b):
    return lax.dot_general(a, b, (((0,), (0,)), ((), ())), preferred_element_type=f32)


def _dot(a, b):
    return jnp.dot(a, b, preferred_element_type=f32)


def _rms(x):
    return x * lax.rsqrt(jnp.mean(x * x, axis=-1, keepdims=True) + EPS)


def _silu(x):
    return x * (1.0 / (1.0 + jnp.exp(-x)))


def _mod_row(i):
    return jnp.where(i % TILES == 0, BATCH, i // TILES)


def _lat_first(i):
    return (i // TILES) * TILES + (i % TILES + TILES - 1) % TILES


def _mod_kernel(s_ref, w_ref, b_ref, o_ref):
    s = _silu(s_ref[...]).astype(bf16)
    o_ref[...] = _dot(s, w_ref[...].astype(bf16)) + b_ref[...]


def _modulation(cc, ada_w, ada_b):
    tn = 1024
    return pl.pallas_call(
        _mod_kernel,
        grid=(DEPTH, 6 * D // tn),
        in_specs=[pl.BlockSpec((8, D), lambda l, j: (0, 0)),
                  pl.BlockSpec((None, D, tn), lambda l, j: (l, 0, j)),
                  pl.BlockSpec((None, 1, tn), lambda l, j: (l, 0, j))],
        out_specs=pl.BlockSpec((None, 8, tn), lambda l, j: (l, 0, j)),
        out_shape=jax.ShapeDtypeStruct((DEPTH, 8, 6 * D), f32),
        compiler_params=_cp(("arbitrary", "arbitrary")),
        name="modulation",
    )(cc, ada_w, ada_b.reshape(DEPTH, 1, 6 * D))


def _in_kernel(x_ref, mod_ref, w_ref, o_ref):
    h = _rms(x_ref[...]) * (1.0 + mod_ref[1:2, :]) + mod_ref[0:1, :]
    o_ref[...] = _dot(h.astype(bf16), w_ref[...]).astype(bf16)


def _in_proj(xa, mods, w_in_p, layer):
    return pl.pallas_call(
        _in_kernel,
        grid=(NC // TN_IN, R // TM),
        in_specs=[pl.BlockSpec((TM, D), lambda j, i: (i, 0)),
                  pl.BlockSpec((None, None, 6, D), lambda j, i: (layer, _mod_row(i), 0, 0)),
                  pl.BlockSpec((D, TN_IN), lambda j, i: (0, j))],
        out_specs=pl.BlockSpec((TM, TN_IN), lambda j, i: (i, j)),
        out_shape=jax.ShapeDtypeStruct((R, NC), bf16),
        compiler_params=_cp(("arbitrary", "arbitrary")),
        name="in_proj",
    )(xa, mods, w_in_p)


def _mla_prep_kernel(cq_ref, ckv_ref, krz_ref, tc_ref, ts_ref, qn_ref, kvn_ref, wq_ref, wkv_ref, e2_ref,
                     q_ref, k_ref, vt_ref):
    tc = tc_ref[...]
    ts = ts_ref[...]
    scale = (MLA_NOPE + MLA_ROPE) ** -0.5 * LOG2E
    hq = (_rms(cq_ref[...].astype(f32)) * qn_ref[...]).astype(bf16)
    q2 = _dot(hq, wq_ref[...])
    hkv = (_rms(ckv_ref[...].astype(f32)) * kvn_ref[...]).astype(bf16)
    kv = _dot(hkv, wkv_ref[...])
    kr2 = _dot(krz_ref[...], e2_ref[...])
    nq = MLA_H * HP
    for h in range(MLA_H):
        sl = slice(h * HP, (h + 1) * HP)
        sl2 = slice(nq + h * HP, nq + (h + 1) * HP)
        q_ref[:, sl] = ((q2[:, sl] * tc + q2[:, sl2] * ts) * scale).astype(bf16)
        k_ref[:, sl] = (kv[:, sl] + kr2[:, sl] * tc + kr2[:, sl2] * ts).astype(bf16)
    vt_ref[...] = kv[:, nq:].T.astype(bf16)


def _mla_prep(p, tc, ts, qn, kvn, wq2, wkv, e2):
    nq = MLA_H * HP
    nv = MLA_H * MLA_V
    const = lambda shape: pl.BlockSpec(shape, lambda i: (0, 0))
    return pl.pallas_call(
        _mla_prep_kernel,
        grid=(R // TM,),
        in_specs=[pl.BlockSpec((TM, MLA_QR), lambda i: (i, C_CQ // MLA_QR)),
                  pl.BlockSpec((TM, MLA_KVR), lambda i: (i, C_CKV // MLA_KVR)),
                  pl.BlockSpec((TM, 128), lambda i: (i, C_KRZ // 128)),
                  pl.BlockSpec((TM, HP), lambda i: (i % TILES, 0)),
                  pl.BlockSpec((TM, HP), lambda i: (i % TILES, 0)),
                  const((1, MLA_QR)), const((1, MLA_KVR)),
                  const((MLA_QR, 2 * nq)), const((MLA_KVR, nq + nv)), const((128, 2 * nq))],
        out_specs=[pl.BlockSpec((TM, nq), lambda i: (_lat_first(i), 0)),
                   pl.BlockSpec((TM, nq), lambda i: (i, 0)),
                   pl.BlockSpec((None, nv, TM), lambda i: (i // TILES, 0, i % TILES))],
        out_shape=[jax.ShapeDtypeStruct((R, nq), bf16), jax.ShapeDtypeStruct((R, nq), bf16),
                   jax.ShapeDtypeStruct((BATCH, nv, S), bf16)],
        compiler_params=_cp(("arbitrary",)),
        name="mla_prep",
    )(p, p, p, tc, ts, qn, kvn, wq2, wkv, e2)


TQ = 512


def _attn_body(q_ref, k_ref, vt_ref, o_ref, s_sc, n_chunks):
    q = q_ref[...]
    nq = q.shape[0]
    m = None
    for j in range(n_chunks):
        s = _nt(k_ref[j * TM:(j + 1) * TM, :], q)
        s_sc[j] = s
        cm = jnp.max(s, axis=0, keepdims=True)
        m = cm if m is None else jnp.maximum(m, cm)
    l = jnp.zeros((1, nq), f32)
    acc = jnp.zeros((MLA_V, nq), f32)
    for j in range(n_chunks):
        p = jnp.exp2(s_sc[j] - m)
        l = l + jnp.sum(p, axis=0, keepdims=True)
        acc = acc + _dot(vt_ref[:, j * TM:(j + 1) * TM], p.astype(bf16))
    o_ref[...] = (acc * (1.0 / l)).T.astype(bf16)


def _attn_ctx_kernel(q_ref, k_ref, vt_ref, ya_hbm_ref, o_ref, s_sc):
    del ya_hbm_ref
    _attn_body(q_ref, k_ref, vt_ref, o_ref, s_sc, 1)


def _attention(q, k, vt):
    nq = MLA_H * HP
    nv = MLA_H * MLA_V
    q3 = q.reshape(BATCH, S, nq)
    k3 = k.reshape(BATCH, S, nq)
    ya = pl.pallas_call(
        functools.partial(_attn_body, n_chunks=S // TM),
        grid=(BATCH, MLA_H, N_LAT // TQ),
        in_specs=[pl.BlockSpec((None, TQ, HP), lambda b, h, t: (b, t, h)),
                  pl.BlockSpec((None, S, HP), lambda b, h, t: (b, 0, h)),
                  pl.BlockSpec((None, MLA_V, S), lambda b, h, t: (b, h, 0))],
        out_specs=pl.BlockSpec((None, TQ, MLA_V), lambda b, h, t: (b, t, h)),
        out_shape=jax.ShapeDtypeStruct((BATCH, S, nv), bf16),
        scratch_shapes=[pltpu.VMEM((S // TM, TM, TQ), f32)],
        compiler_params=_cp(("arbitrary", "arbitrary", "arbitrary")),
        name="mla_attention",
    )(q3, k3, vt)
    ctx_blk = N_LAT // N_CTX
    ya = pl.pallas_call(
        _attn_ctx_kernel,
        grid=(BATCH, MLA_H),
        in_specs=[pl.BlockSpec((None, N_CTX, HP), lambda b, h: (b, ctx_blk, h)),
                  pl.BlockSpec((None, N_CTX, HP), lambda b, h: (b, 0, h)),
                  pl.BlockSpec((None, MLA_V, N_CTX), lambda b, h: (b, h, 0)),
                  pl.BlockSpec(memory_space=pl.ANY)],
        out_specs=pl.BlockSpec((None, N_CTX, MLA_V), lambda b, h: (b, ctx_blk, h)),
        out_shape=jax.ShapeDtypeStruct((BATCH, S, nv), bf16),
        scratch_shapes=[pltpu.VMEM((1, TM, N_CTX), f32)],
        input_output_aliases={3: 0},
        compiler_params=_cp(("arbitrary", "arbitrary")),
        name="mla_attention_ctx",
    )(q3, k3, vt, ya)
    return ya.reshape(R, nv)


N_CHUNK = S // CHUNK
N_CCH = N_CTX // CHUNK
SCAN_H = 4
SCAN_V = 128
SCAN_W = SCAN_H * SCAN_V


def _scan_consts(kh):
    dk = kh // SCAN_H
    row = lax.broadcasted_iota(jnp.int32, (CHUNK, SCAN_H * CHUNK), 0)
    col = lax.broadcasted_iota(jnp.int32, (CHUNK, SCAN_H * CHUNK), 1) % CHUNK
    incl = row >= col
    strict = col > row
    krow = lax.broadcasted_iota(jnp.int32, (SCAN_H * CHUNK, kh), 0) // CHUNK
    kcol = lax.broadcasted_iota(jnp.int32, (SCAN_H * CHUNK, kh), 1) // dk
    kmask = krow == kcol
    vrow = lax.broadcasted_iota(jnp.int32, (SCAN_H * CHUNK, SCAN_W), 0) // CHUNK
    vcol = lax.broadcasted_iota(jnp.int32, (SCAN_H * CHUNK, SCAN_W), 1) // SCAN_V
    vmask = vrow == vcol
    srow = lax.broadcasted_iota(jnp.int32, (SCAN_W, kh), 0) // SCAN_V
    scol = lax.broadcasted_iota(jnp.int32, (SCAN_W, kh), 1) // dk
    smask = srow == scol
    return incl, strict, kmask, vmask, smask


def _chunk_step(q, k, v, cum, cend, st_ref, amask, kmask, vmask, smask):
    qd = (q * jnp.exp(cum)).astype(bf16)
    ki = k * jnp.exp(-cum)
    kend = (k * jnp.exp(cend - cum)).astype(bf16)
    dec = jnp.exp(cend)
    kst = jnp.where(kmask, jnp.concatenate([ki] * SCAN_H, axis=0), 0.0).astype(bf16)
    att = jnp.where(amask, _nt(qd, kst), 0.0).astype(bf16)
    vbd = jnp.where(vmask, jnp.concatenate([v] * SCAN_H, axis=0), jnp.zeros((), bf16))
    st = st_ref[...]
    o = _dot(att, vbd) + _nt(qd, st.astype(bf16))
    st_ref[...] = st * dec + jnp.where(smask, _tn(v, kend), 0.0)
    return o


def _bwd_chunk(i):
    return jnp.where(i < N_CCH, N_CCH - 1 - i, N_CHUNK + N_CCH - 1 - i)


def _scan_finish(of_sc, ob_sc, g_ref, y_ref):
    def fin(i, carry):
        r0 = pl.multiple_of(i * TM, TM)
        o = of_sc[pl.ds(r0, TM), :] + ob_sc[pl.ds(r0, TM), :]
        g = g_ref[pl.ds(r0, TM), :].astype(f32)
        for h in range(SCAN_H):
            sl = slice(h * SCAN_V, (h + 1) * SCAN_V)
            y_ref[pl.ds(r0, TM), sl] = (_rms(o[:, sl]) * _silu(g[:, sl])).astype(bf16)
        return carry

    lax.fori_loop(0, TILES, fin, 0)


def _gla_kernel(q_ref, k_ref, v_ref, krz_ref, og_ref, w2_ref, b2_ref, y_ref, cum_sc, of_sc, ob_sc, stf_sc, stb_sc):
    kh = GLA_H * GLA_DK
    ri = lax.broadcasted_iota(jnp.int32, (TM, TM), 0)
    ci = lax.broadcasted_iota(jnp.int32, (TM, TM), 1)
    same = (ri // CHUNK) == (ci // CHUNK)
    pre = jnp.where(same & (ci <= ri), 1.0, 0.0).astype(bf16)
    suf = jnp.where(same & (ci >= ri), 1.0, 0.0).astype(bf16)

    def exact_sum(m, x):
        hi = x.astype(bf16)
        r1 = x - hi.astype(f32)
        mid = r1.astype(bf16)
        lo = (r1 - mid.astype(f32)).astype(bf16)
        return _dot(m, hi) + _dot(m, mid) + _dot(m, lo)

    def gates(i, carry):
        r0 = pl.multiple_of(i * TM, TM)
        lg = _dot(krz_ref[pl.ds(r0, TM), :], w2_ref[...]) + b2_ref[...]
        la = (jnp.minimum(lg, 0.0) - jnp.log1p(jnp.exp(-jnp.abs(lg)))) * (1.0 / GLA_TAU)
        cum_sc[pl.ds(r0, TM), 0:kh] = exact_sum(pre, la[:, 0:kh])
        cum_sc[pl.ds(r0, TM), kh:2 * kh] = exact_sum(suf, la[:, kh:2 * kh])
        return carry

    lax.fori_loop(0, TILES, gates, 0)

    incl, strict, kmask, vmask, smask = _scan_consts(kh)
    stf_sc[...] = jnp.zeros_like(stf_sc)
    stb_sc[...] = jnp.zeros_like(stb_sc)
    qscale = GLA_DK ** -0.5

    def body(i, carry):
        rf = pl.multiple_of(i * CHUNK, CHUNK)
        cum = cum_sc[pl.ds(rf, CHUNK), 0:kh]
        of_sc[pl.ds(rf, CHUNK), :] = _chunk_step(
            q_ref[pl.ds(rf, CHUNK), :].astype(f32) * qscale, k_ref[pl.ds(rf, CHUNK), :].astype(f32),
            v_ref[pl.ds(rf, CHUNK), :], cum, cum[CHUNK - 1:CHUNK, :], stf_sc, incl, kmask, vmask, smask)
        rb = pl.multiple_of(_bwd_chunk(i) * CHUNK, CHUNK)
        rc = cum_sc[pl.ds(rb, CHUNK), kh:2 * kh]
        ob_sc[pl.ds(rb, CHUNK), :] = _chunk_step(
            q_ref[pl.ds(rb, CHUNK), :].astype(f32) * qscale, k_ref[pl.ds(rb, CHUNK), :].astype(f32),
            v_ref[pl.ds(rb, CHUNK), :], rc, rc[0:1, :], stb_sc, strict, kmask, vmask, smask)
        return carry

    lax.fori_loop(0, N_CHUNK, body, 0, unroll=4)
    _scan_finish(of_sc, ob_sc, og_ref, y_ref)


def _gla(p, w2p, b2):
    kh = GLA_H * GLA_DK
    p3 = p.reshape(BATCH, S, NC)
    col = lambda w, c: pl.BlockSpec((None, S, w), lambda b: (b, 0, c // w))
    return pl.pallas_call(
        _gla_kernel,
        grid=(BATCH,),
        in_specs=[col(kh, C_GQ), col(kh, C_GK), col(SCAN_W, C_GV), col(128, C_KRZ), col(SCAN_W, C_GOG),
                  pl.BlockSpec((128, 2 * kh), lambda b: (0, 0)), pl.BlockSpec((1, 2 * kh), lambda b: (0, 0))],
        out_specs=pl.BlockSpec((None, S, SCAN_W), lambda b: (b, 0, 0)),
        out_shape=jax.ShapeDtypeStruct((BATCH, S, SCAN_W), bf16),
        scratch_shapes=[pltpu.VMEM((S, 2 * kh), f32), pltpu.VMEM((S, SCAN_W), f32), pltpu.VMEM((S, SCAN_W), f32),
                        pltpu.VMEM((SCAN_W, kh), f32), pltpu.VMEM((SCAN_W, kh), f32)],
        compiler_params=_cp(("arbitrary",)),
        name="gla",
    )(p3, p3, p3, p3, p3, w2p, b2).reshape(R, SCAN_W)


def _ret_kernel(q_ref, k_ref, v_ref, g_ref, cos_ref, sin_ref, cumf_ref, cumb_ref, y_ref, of_sc, ob_sc, stf_sc, stb_sc):
    kh = RET_H * RET_DK
    incl, strict, kmask, vmask, smask = _scan_consts(kh)
    stf_sc[...] = jnp.zeros_like(stf_sc)
    stb_sc[...] = jnp.zeros_like(stb_sc)
    kscale = RET_DK ** -0.5
    cumf = cumf_ref[...]
    cumb = cumb_ref[...]

    def rotate(x, r0):
        cos = jnp.concatenate([cos_ref[pl.ds(r0, CHUNK), :]] * RET_H, axis=1)
        sin = jnp.concatenate([sin_ref[pl.ds(r0, CHUNK), :]] * RET_H, axis=1)
        partner = jnp.concatenate(
            [pltpu.roll(x[:, h * RET_DK:(h + 1) * RET_DK], RET_DK // 2, axis=1) for h in range(RET_H)], axis=1)
        return x * cos + partner * sin

    def body(i, carry):
        rf = pl.multiple_of(i * CHUNK, CHUNK)
        of_sc[pl.ds(rf, CHUNK), :] = _chunk_step(
            rotate(q_ref[pl.ds(rf, CHUNK), :].astype(f32), rf),
            rotate(k_ref[pl.ds(rf, CHUNK), :].astype(f32), rf) * kscale,
            v_ref[pl.ds(rf, CHUNK), :], cumf, cumf[CHUNK - 1:CHUNK, :], stf_sc, incl, kmask, vmask, smask)
        rb = pl.multiple_of(_bwd_chunk(i) * CHUNK, CHUNK)
        ob_sc[pl.ds(rb, CHUNK), :] = _chunk_step(
            rotate(q_ref[pl.ds(rb, CHUNK), :].astype(f32), rb),
            rotate(k_ref[pl.ds(rb, CHUNK), :].astype(f32), rb) * kscale,
            v_ref[pl.ds(rb, CHUNK), :], cumb, cumb[0:1, :], stb_sc, strict, kmask, vmask, smask)
        return carry

    lax.fori_loop(0, N_CHUNK, body, 0, unroll=4)
    _scan_finish(of_sc, ob_sc, g_ref, y_ref)


def _retention(p, cos_r, sin_r, cumf, cumb):
    kh = RET_H * RET_DK
    p3 = p.reshape(BATCH, S, NC)
    col = lambda w, c: pl.BlockSpec((None, S, w), lambda b: (b, 0, c // w))
    const = lambda shape: pl.BlockSpec(shape, lambda b: (0, 0))
    return pl.pallas_call(
        _ret_kernel,
        grid=(BATCH,),
        in_specs=[col(kh, C_RQ), col(kh, C_RK), col(SCAN_W, C_RV), col(SCAN_W, C_RG),
                  const((S, RET_DK)), const((S, RET_DK)), const((CHUNK, kh)), const((CHUNK, kh))],
        out_specs=pl.BlockSpec((None, S, SCAN_W), lambda b: (b, 0, 0)),
        out_shape=jax.ShapeDtypeStruct((BATCH, S, SCAN_W), bf16),
        scratch_shapes=[pltpu.VMEM((S, SCAN_W), f32), pltpu.VMEM((S, SCAN_W), f32),
                        pltpu.VMEM((SCAN_W, kh), f32), pltpu.VMEM((SCAN_W, kh), f32)],
        compiler_params=_cp(("arbitrary",)),
        name="retention",
    )(p3, p3, p3, p3, cos_r, sin_r, cumf, cumb).reshape(R, SCAN_W)


def _out_kernel(x_ref, ya_ref, yb_ref, yc_ref, mod_ref, w_ref, rwt_ref, xo_ref, h2_ref, aff_ref):
    na = MLA_H * MLA_V
    acc = _dot(ya_ref[...], w_ref[0:na, :])
    acc += _dot(yb_ref[...], w_ref[na:na + SCAN_W, :])
    acc += _dot(yc_ref[...], w_ref[na + SCAN_W:, :])
    x = x_ref[...] + mod_ref[2:3, :] * acc
    xo_ref[...] = x
    hb = (_rms(x) * (1.0 + mod_ref[4:5, :]) + mod_ref[3:4, :]).astype(bf16)
    h2_ref[...] = hb
    lg = _nt(rwt_ref[...], hb)
    e = jnp.exp(lg - jnp.max(lg, axis=0, keepdims=True))
    aff_ref[...] = e / jnp.sum(e, axis=0, keepdims=True)


def _out_proj(xa, ya, yb, yc, mods, w_out, rwt, layer):
    na = MLA_H * MLA_V
    row = lambda w: pl.BlockSpec((TM, w), lambda i: (i, 0))
    return pl.pallas_call(
        _out_kernel,
        grid=(R // TM,),
        in_specs=[row(D), pl.BlockSpec((TM, na), lambda i: (_lat_first(i), 0)), row(SCAN_W), row(SCAN_W),
                  pl.BlockSpec((None, None, 6, D), lambda i: (layer, _mod_row(i), 0, 0)),
                  pl.BlockSpec((D, D), lambda i: (0, 0)),
                  pl.BlockSpec((N_EXP, D), lambda i: (0, 0))],
        out_specs=[row(D), row(D), pl.BlockSpec((N_EXP, TM), lambda i: (0, i))],
        out_shape=[jax.ShapeDtypeStruct((R, D), f32), jax.ShapeDtypeStruct((R, D), bf16),
                   jax.ShapeDtypeStruct((N_EXP, R), f32)],
        compiler_params=_cp(("arbitrary",)),
        name="out_proj",
    )(xa, ya, yb, yc, mods, w_out, rwt)


def _topk_kernel(aff_ref, pos_ref, post_ref):
    ri = lax.broadcasted_iota(jnp.int32, (TM, TM), 0)
    ci = lax.broadcasted_iota(jnp.int32, (TM, TM), 1)
    before = jnp.where(ri < ci, 1.0, 0.0).astype(bf16)

    def prefix_count(m):
        out = []
        off = jnp.zeros((N_EXP, 1), f32)
        for blk in range(m.shape[1] // TM):
            mb = m[:, blk * TM:(blk + 1) * TM]
            out.append(_dot(mb.astype(bf16), before) + off)
            off = off + jnp.sum(mb, axis=1, keepdims=True)
        return jnp.concatenate(out, axis=1) if len(out) > 1 else out[0]

    def select(a, cap, base):
        bits = pltpu.bitcast(a, jnp.int32)
        capf = float(cap)

        def step(i, thr):
            cand = thr | jnp.left_shift(jnp.int32(1), 30 - i)
            cnt = jnp.sum(jnp.where(bits >= cand, 1.0, 0.0), axis=1, keepdims=True)
            return jnp.where(cnt >= capf, cand, thr)

        thr = lax.fori_loop(0, 31, step, jnp.zeros((N_EXP, 1), jnp.int32))
        gt = jnp.where(bits > thr, 1.0, 0.0)
        eq = jnp.where(bits == thr, 1.0, 0.0)
        need = capf - jnp.sum(gt, axis=1, keepdims=True)
        keep = gt + eq * jnp.where(prefix_count(eq) < need, 1.0, 0.0)
        return jnp.where(keep > 0.5, prefix_count(keep) + float(base), -1.0)

    a = aff_ref[...]
    pos = jnp.concatenate([select(a[:, 0:N_CTX], CAP_CTX, CAP_LAT), select(a[:, N_CTX:], CAP_LAT, 0)], axis=1)
    pos_ref[...] = pos
    post_ref[...] = jnp.concatenate([pos, jnp.full((128 - N_EXP, S), -1.0, f32)], axis=0).T


def _topk(aff):
    return pl.pallas_call(
        _topk_kernel,
        grid=(BATCH,),
        in_specs=[pl.BlockSpec((N_EXP, S), lambda b: (0, b))],
        out_specs=[pl.BlockSpec((N_EXP, S), lambda b: (0, b)), pl.BlockSpec((S, 128), lambda b: (b, 0))],
        out_shape=[jax.ShapeDtypeStruct((N_EXP, R), f32), jax.ShapeDtypeStruct((R, 128), f32)],
        compiler_params=_cp(("arbitrary",)),
        name="route_topk",
    )(aff)


def _gather_kernel(pos_ref, aff_ref, h_ref, x_ref, g_ref):
    e = pl.program_id(1)
    prow = pos_ref[pl.ds(e, 1), :]
    arow = aff_ref[pl.ds(e, 1), :]
    slot = lax.broadcasted_iota(jnp.int32, (SLOTS, S), 0).astype(f32)
    hit = prow == slot
    onehot = jnp.where(hit, 1.0, 0.0).astype(bf16)
    x_ref[...] = _dot(onehot, h_ref[...]).astype(bf16)
    g = jnp.sum(jnp.where(hit, arow, 0.0), axis=1, keepdims=True)
    g_ref[...] = jnp.broadcast_to(g, (SLOTS, 128))


def _gather(pos, aff, h2):
    return pl.pallas_call(
        _gather_kernel,
        grid=(BATCH, N_EXP),
        in_specs=[pl.BlockSpec((N_EXP, S), lambda b, e: (0, b)),
                  pl.BlockSpec((N_EXP, S), lambda b, e: (0, b)),
                  pl.BlockSpec((S, D), lambda b, e: (b, 0))],
        out_specs=[pl.BlockSpec((None, SLOTS, D), lambda b, e: (e, b, 0)),
                   pl.BlockSpec((None, SLOTS, 128), lambda b, e: (e, b, 0))],
        out_shape=[jax.ShapeDtypeStruct((N_EXP, BATCH * SLOTS, D), bf16),
                   jax.ShapeDtypeStruct((N_EXP, BATCH * SLOTS, 128), f32)],
        compiler_params=_cp(("arbitrary", "arbitrary")),
        name="moe_gather",
    )(pos, aff, h2)


TF = 256
N_UP = EXP_FF // TF
N_DOWN = D // TF


def _moe_kernel(x_ref, wg_ref, wu_ref, wd_ref, g_ref, y_ref, hm_ref):
    s = pl.program_id(1)

    @pl.when(s < N_UP)
    def _():
        x = x_ref[...]
        a = _dot(x, wg_ref[...].astype(bf16))
        u = _dot(x, wu_ref[...].astype(bf16))
        hm_ref[s] = (_silu(a) * u).astype(bf16)

    @pl.when(s >= N_UP)
    def _():
        wd = wd_ref[...].astype(bf16)
        acc = _dot(hm_ref[0], wd[0:TF, :])
        for c in range(1, N_UP):
            acc += _dot(hm_ref[c], wd[c * TF:(c + 1) * TF, :])
        y_ref[...] = (acc * g_ref[:, 0:1]).astype(bf16)


def _moe_ffn(xg, gs, w_gate, w_up, w_down, layer):
    rows = BATCH * SLOTS
    up = lambda s: jnp.minimum(s, N_UP - 1)
    down = lambda s: jnp.maximum(s - N_UP, 0)
    return pl.pallas_call(
        _moe_kernel,
        grid=(N_EXP, N_UP + N_DOWN),
        in_specs=[pl.BlockSpec((None, rows, D), lambda e, s: (e, 0, 0)),
                  pl.BlockSpec((None, None, D, TF), lambda e, s: (layer, e, 0, up(s))),
                  pl.BlockSpec((None, None, D, TF), lambda e, s: (layer, e, 0, up(s))),
                  pl.BlockSpec((None, None, EXP_FF, TF), lambda e, s: (layer, e, 0, down(s))),
                  pl.BlockSpec((None, rows, 128), lambda e, s: (e, 0, 0))],
        out_specs=pl.BlockSpec((None, rows, TF), lambda e, s: (e, 0, down(s))),
        out_shape=jax.ShapeDtypeStruct((N_EXP, rows, D), bf16),
        scratch_shapes=[pltpu.VMEM((N_UP, rows, TF), bf16)],
        compiler_params=_cp(("arbitrary", "arbitrary")),
        name="moe_ffn",
    )(xg, w_gate, w_up, w_down, gs)


TN_C = 1024


def _combine_kernel(post_ref, ys_ref, x_ref, mod_ref, o_ref):
    t = pl.program_id(2)
    pb = post_ref[...]

    def scatter(cap, base, ys):
        pc = pb - float(base)
        pc = jnp.where((pc >= 0.0) & (pc < float(cap)), pc, -1.0).astype(bf16)
        er = lax.broadcasted_iota(jnp.int32, (128, N_EXP * cap), 0)
        ec = lax.broadcasted_iota(jnp.int32, (128, N_EXP * cap), 1) // cap
        rep = jnp.where(er == ec, 1.0, 0.0).astype(bf16)
        slot = (lax.broadcasted_iota(jnp.int32, (TM, N_EXP * cap), 1) % cap).astype(f32)
        onehot = jnp.where(_dot(pc, rep) == slot, 1.0, 0.0).astype(bf16)
        o_ref[...] = x_ref[...] + mod_ref[5:6, :] * _dot(onehot, ys)

    @pl.when(t == 0)
    def _():
        scatter(CAP_CTX, CAP_LAT, ys_ref[:, CAP_LAT:SLOTS, :].reshape(N_EXP * CAP_CTX, TN_C))

    @pl.when(t > 0)
    def _():
        scatter(CAP_LAT, 0, ys_ref[:, 0:CAP_LAT, :].reshape(N_EXP * CAP_LAT, TN_C))


def _combine(post, ys, xa, mods, layer):
    return pl.pallas_call(
        _combine_kernel,
        grid=(BATCH, D // TN_C, TILES),
        in_specs=[pl.BlockSpec((TM, 128), lambda b, n, t: (b * TILES + t, 0)),
                  pl.BlockSpec((N_EXP, None, SLOTS, TN_C), lambda b, n, t: (0, b, 0, n)),
                  pl.BlockSpec((TM, TN_C), lambda b, n, t: (b * TILES + t, n)),
                  pl.BlockSpec((None, None, 6, TN_C), lambda b, n, t: (layer, jnp.where(t == 0, BATCH, b), 0, n))],
        out_specs=pl.BlockSpec((TM, TN_C), lambda b, n, t: (b * TILES + t, n)),
        out_shape=jax.ShapeDtypeStruct((R, D), f32),
        compiler_params=_cp(("arbitrary", "arbitrary", "arbitrary")),
        name="moe_combine",
    )(post, ys.reshape(N_EXP, BATCH, SLOTS, D), xa, mods)


def _final_kernel(x_ref, g_ref, o_ref):
    o_ref[...] = _rms(x_ref[...]) * g_ref[...]


def _final_norm(xa, gain):
    lt = N_LAT // TM
    return pl.pallas_call(
        _final_kernel,
        grid=(BATCH, lt),
        in_specs=[pl.BlockSpec((TM, D), lambda b, t: (b * TILES + 1 + t, 0)),
                  pl.BlockSpec((1, D), lambda b, t: (0, 0))],
        out_specs=pl.BlockSpec((None, TM, D), lambda b, t: (b, t, 0)),
        out_shape=jax.ShapeDtypeStruct((BATCH, N_LAT, D), f32),
        compiler_params=_cp(("arbitrary", "arbitrary")),
        name="final_norm",
    )(xa, gain.reshape(1, D))


def _split_pairs(w, heads):
    rows, cols = w.shape
    return w.reshape(rows, heads, cols // heads // 2, 2).swapaxes(2, 3).reshape(rows, cols)


def _in_weights(w):
    o = np.cumsum((0, MLA_QR, MLA_KVR, MLA_ROPE, 256, 256, 512, 32, 512, 512, 512, 512, 512))
    piece = lambda i: w[:, o[i]:o[i + 1]]
    cq, ckv, kr, gq, gk, gv, gz, gog, rq, rk, rv, rg = (piece(i) for i in range(12))
    pad = jnp.zeros((D, NC - (C_KRZ + MLA_ROPE + 2 * GLA_RANK)), w.dtype)
    return jnp.concatenate([_split_pairs(rq, RET_H), _split_pairs(rk, RET_H), rv, rg, gv, gog, cq, ckv, gq, gk,
                            _split_pairs(kr, 1), gz, pad], axis=1).astype(bf16)


def _mla_weights(w_uq, w_ukv):
    half = MLA_ROPE // 2
    wq = w_uq.reshape(MLA_QR, MLA_H, MLA_NOPE + MLA_ROPE)
    nope = wq[:, :, :MLA_NOPE]
    rope = wq[:, :, MLA_NOPE:].reshape(MLA_QR, MLA_H, half, 2)
    ev, od = rope[..., 0], rope[..., 1]
    zpad = jnp.zeros((MLA_QR, MLA_H, HP - MLA_NOPE - MLA_ROPE), w_uq.dtype)
    q_main = jnp.concatenate([nope, ev, od, zpad], axis=-1).reshape(MLA_QR, MLA_H * HP)
    q_part = jnp.concatenate([jnp.zeros_like(nope), od, ev, zpad], axis=-1).reshape(MLA_QR, MLA_H * HP)
    wkv = w_ukv.reshape(MLA_KVR, MLA_H, MLA_NOPE + MLA_V)
    k_main = jnp.concatenate([wkv[:, :, :MLA_NOPE], jnp.zeros((MLA_KVR, MLA_H, HP - MLA_NOPE), w_ukv.dtype)], axis=-1)
    v_main = wkv[:, :, MLA_NOPE:]
    return (jnp.concatenate([q_main, q_part], axis=1).astype(bf16),
            jnp.concatenate([k_main.reshape(MLA_KVR, MLA_H * HP), v_main.reshape(MLA_KVR, MLA_H * MLA_V)], axis=1).astype(bf16))


def _rope_key_placement():
    half = MLA_ROPE // 2
    nq = MLA_H * HP
    e2 = np.zeros((128, 2 * nq), np.float32)
    for h in range(MLA_H):
        for j in range(MLA_ROPE):
            e2[j, h * HP + MLA_NOPE + j] = 1.0
            e2[(j + half) % MLA_ROPE, nq + h * HP + MLA_NOPE + j] = 1.0
    return jnp.asarray(e2, bf16)


def _tables():
    rows = N_LAT // GRID_W
    row = np.repeat(np.arange(rows, dtype=np.float32), GRID_W)
    colp = np.tile(np.arange(GRID_W, dtype=np.float32), rows)
    n_freq = MLA_ROPE // 4
    inv = jnp.power(ROPE_BASE, -jnp.arange(n_freq, dtype=f32) / n_freq)
    ang = jnp.concatenate([row[:, None] * inv, colp[:, None] * inv], axis=-1)
    cos_a, sin_a = jnp.cos(ang), jnp.sin(ang)
    one = jnp.ones((N_LAT, MLA_NOPE), f32)
    zpad = jnp.zeros((N_LAT, HP - MLA_NOPE - MLA_ROPE), f32)
    tc_lat = jnp.concatenate([one, cos_a, cos_a, zpad], axis=1)
    ts_lat = jnp.concatenate([0 * one, -sin_a, sin_a, zpad], axis=1)
    tc_ctx = jnp.concatenate([jnp.ones((N_CTX, MLA_NOPE + MLA_ROPE), f32), jnp.zeros((N_CTX, HP - MLA_NOPE - MLA_ROPE), f32)], axis=1)
    tc = jnp.concatenate([tc_ctx, tc_lat], axis=0)
    ts = jnp.concatenate([jnp.zeros((N_CTX, HP), f32), ts_lat], axis=0)

    inv_r = 1.0 / jnp.power(ROPE_BASE, jnp.linspace(0.0, 1.0, RET_DK // 2, dtype=f32))
    ang_r = jnp.arange(N_LAT, dtype=f32)[:, None] * inv_r
    cos_r = jnp.concatenate([jnp.ones((N_CTX, RET_DK), f32), jnp.concatenate([jnp.cos(ang_r)] * 2, axis=1)], axis=0)
    sin_r = jnp.concatenate([jnp.zeros((N_CTX, RET_DK), f32),
                             jnp.concatenate([-jnp.sin(ang_r), jnp.sin(ang_r)], axis=1)], axis=0)

    def log_decay(direction):
        e = RET_EXP0 + direction + 2.0 * jnp.arange(RET_H, dtype=f32)
        return jnp.repeat(jnp.log1p(-jnp.exp2(-e)), RET_DK)[None, :]

    steps = jnp.arange(1, CHUNK + 1, dtype=f32)[:, None]
    cumf = steps * log_decay(0.0)
    cumb = steps[::-1] * log_decay(1.0)
    return tc, ts, cos_r, sin_r, cumf, cumb


def kernel(x, c, ctx, c_ctx, ada_w, ada_b, w_in, mla_q_norm, mla_w_uq, mla_kv_norm, mla_w_ukv, gla_gate_w2,
           gla_gate_b, w_out, router_w, exp_w_gate, exp_w_up, exp_w_down, final_norm):
    xa = jnp.concatenate([ctx, x], axis=1).reshape(R, D)
    cc = jnp.concatenate([c, c_ctx[None, :], jnp.zeros((8 - BATCH - 1, D), f32)], axis=0)
    mods = _modulation(cc, ada_w, ada_b).reshape(DEPTH, 8, 6, D)
    tc, ts, cos_r, sin_r, cumf, cumb = _tables()
    e2 = _rope_key_placement()
    kh = GLA_H * GLA_DK

    for l in range(DEPTH):
        w_in_p = _in_weights(w_in[l])
        wq2, wkv = _mla_weights(mla_w_uq[l], mla_w_ukv[l])
        w2p = jnp.zeros((128, 2 * kh), f32)
        w2p = w2p.at[64:64 + GLA_RANK, 0:kh].set(gla_gate_w2[l, 0]).at[64 + GLA_RANK:64 + 2 * GLA_RANK, kh:].set(gla_gate_w2[l, 1])
        b2 = gla_gate_b[l].reshape(1, 2 * kh)

        p = _in_proj(xa, mods, w_in_p, l)
        q, k, vt = _mla_prep(p, tc, ts, mla_q_norm[l].reshape(1, -1), mla_kv_norm[l].reshape(1, -1), wq2, wkv, e2)
        ya = _attention(q, k, vt)
        yb = _gla(p, w2p.astype(bf16), b2)
        yc = _retention(p, cos_r, sin_r, cumf, cumb)
        xa, h2, aff = _out_proj(xa, ya, yb, yc, mods, w_out[l].astype(bf16), router_w[l].T.astype(bf16), l)
        pos, post = _topk(aff)
        xg, gs = _gather(pos, aff, h2)
        ys = _moe_ffn(xg, gs, exp_w_gate, exp_w_up, exp_w_down, l)
        xa = _combine(post, ys, xa, mods, l)
    return _final_norm(xa, final_norm)
```

```python
import functools

import numpy as np
import jax
import jax.numpy as jnp
from jax import lax
from jax.experimental import pallas as pl
from jax.experimental.pallas import tpu as pltpu

f32 = jnp.float32
bf16 = jnp.bfloat16

D = 2048
BATCH = 4
N_LAT = 2048
N_CTX = 256
S = N_CTX + N_LAT
R = BATCH * S
DEPTH = 2
GRID_W = 64
EPS = 1e-6
LOG2E = 1.4426950408889634
ROPE_BASE = 10000.0
CHUNK = 64

MLA_H, MLA_QR, MLA_KVR, MLA_NOPE, MLA_ROPE, MLA_V = 8, 512, 256, 128, 64, 128
GLA_H, GLA_DK, GLA_DV, GLA_RANK, GLA_TAU = 4, 64, 128, 16, 16.0
RET_H, RET_DK, RET_DV, RET_EXP0 = 4, 128, 128, 5.0
N_EXP, EXP_FF, EC_CAP = 16, 2048, 2
CAP_LAT = EC_CAP * N_LAT // N_EXP
CAP_CTX = EC_CAP * N_CTX // N_EXP
SLOTS = CAP_LAT + CAP_CTX

TM = 256
TILES = S // TM
HP = 256

C_RQ, C_RK, C_RV, C_RG = 0, 512, 1024, 1536
C_GV, C_GOG, C_CQ, C_CKV, C_GQ, C_GK, C_KRZ = 2048, 2560, 3072, 3584, 3840, 4096, 4352
NC = 4608
TN_IN = 1536

VMEM_LIMIT = 56 * 1024 * 1024


def _cp(sem):
    return pltpu.CompilerParams(dimension_semantics=sem, vmem_limit_bytes=VMEM_LIMIT)


def _nt(a, b):
    return lax.dot_general(a, b, (((1,), (1,)), ((), ())), preferred_element_type=f32)


def _tn(a, b):
    return lax.dot_general(a, b, (((0,), (0,)), ((), ())), preferred_element_type=f32)


def _dot(a, b):
    return jnp.dot(a, b, preferred_element_type=f32)


def _rms(x):
    return x * lax.rsqrt(jnp.mean(x * x, axis=-1, keepdims=True) + EPS)


def _silu(x):
    return x * (1.0 / (1.0 + jnp.exp(-x)))


def _mod_row(i):
    return jnp.where(i % TILES == 0, BATCH, i // TILES)


def _lat_first(i):
    return (i // TILES) * TILES + (i % TILES + TILES - 1) % TILES


def _mod_kernel(s_ref, w_ref, b_ref, o_ref):
    s = _silu(s_ref[...]).astype(bf16)
    o_ref[...] = _dot(s, w_ref[...].astype(bf16)) + b_ref[...]


def _modulation(cc, ada_w, ada_b):
    tn = 1024
    return pl.pallas_call(
        _mod_kernel,
        grid=(DEPTH, 6 * D // tn),
        in_specs=[pl.BlockSpec((8, D), lambda l, j: (0, 0)),
                  pl.BlockSpec((None, D, tn), lambda l, j: (l, 0, j)),
                  pl.BlockSpec((None, 1, tn), lambda l, j: (l, 0, j))],
        out_specs=pl.BlockSpec((None, 8, tn), lambda l, j: (l, 0, j)),
        out_shape=jax.ShapeDtypeStruct((DEPTH, 8, 6 * D), f32),
        compiler_params=_cp(("arbitrary", "arbitrary")),
        name="modulation",
    )(cc, ada_w, ada_b.reshape(DEPTH, 1, 6 * D))


def _resid_specs(resid, tile_of):
    if len(resid) == 1:
        return [pl.BlockSpec((TM, D), lambda *g: (tile_of(*g), 0))]
    lat = pl.BlockSpec((None, TM, D), lambda *g: (tile_of(*g) // TILES, jnp.maximum(tile_of(*g) % TILES - 1, 0), 0))
    ctx = pl.BlockSpec((None, TM, D), lambda *g: (tile_of(*g) // TILES, 0, 0))
    return [lat, ctx]


def _resid_tile(refs, is_ctx):
    if len(refs) == 1:
        return refs[0][...]
    return jnp.where(is_ctx, refs[1][...], refs[0][...])


def _in_kernel(*refs):
    *resid, mod_ref, w_ref, o_ref = refs
    x = _resid_tile(resid, pl.program_id(1) % TILES == 0)
    h = _rms(x) * (1.0 + mod_ref[1:2, :]) + mod_ref[0:1, :]
    o_ref[...] = _dot(h.astype(bf16), w_ref[...]).astype(bf16)


def _in_proj(resid, mods, w_in_p, layer):
    return pl.pallas_call(
        _in_kernel,
        grid=(NC // TN_IN, R // TM),
        in_specs=_resid_specs(resid, lambda j, i: i) + [
            pl.BlockSpec((None, None, 6, D), lambda j, i: (layer, _mod_row(i), 0, 0)),
            pl.BlockSpec((D, TN_IN), lambda j, i: (0, j))],
        out_specs=pl.BlockSpec((TM, TN_IN), lambda j, i: (i, j)),
        out_shape=jax.ShapeDtypeStruct((R, NC), bf16),
        compiler_params=_cp(("arbitrary", "arbitrary")),
        name="in_proj",
    )(*resid, mods, w_in_p)


def _mla_prep_kernel(cq_ref, ckv_ref, krz_ref, tc_ref, ts_ref, qn_ref, kvn_ref, wq_ref, wkv_ref, e2_ref,
                     q_ref, k_ref, vt_ref):
    tc = tc_ref[...]
    ts = ts_ref[...]
    scale = (MLA_NOPE + MLA_ROPE) ** -0.5 * LOG2E
    hq = (_rms(cq_ref[...].astype(f32)) * qn_ref[...]).astype(bf16)
    q2 = _dot(hq, wq_ref[...])
    hkv = (_rms(ckv_ref[...].astype(f32)) * kvn_ref[...]).astype(bf16)
    kv = _dot(hkv, wkv_ref[...])
    kr2 = _dot(krz_ref[...], e2_ref[...])
    nq = MLA_H * HP
    for h in range(MLA_H):
        sl = slice(h * HP, (h + 1) * HP)
        sl2 = slice(nq + h * HP, nq + (h + 1) * HP)
        q_ref[:, sl] = ((q2[:, sl] * tc + q2[:, sl2] * ts) * scale).astype(bf16)
        k_ref[:, sl] = (kv[:, sl] + kr2[:, sl] * tc + kr2[:, sl2] * ts).astype(bf16)
    vt_ref[...] = kv[:, nq:].T.astype(bf16)


def _mla_prep(p, tc, ts, qn, kvn, wq2, wkv, e2):
    nq = MLA_H * HP
    nv = MLA_H * MLA_V
    const = lambda shape: pl.BlockSpec(shape, lambda i: (0, 0))
    return pl.pallas_call(
        _mla_prep_kernel,
        grid=(R // TM,),
        in_specs=[pl.BlockSpec((TM, MLA_QR), lambda i: (i, C_CQ // MLA_QR)),
                  pl.BlockSpec((TM, MLA_KVR), lambda i: (i, C_CKV // MLA_KVR)),
                  pl.BlockSpec((TM, 128), lambda i: (i, C_KRZ // 128)),
                  pl.BlockSpec((TM, HP), lambda i: (i % TILES, 0)),
                  pl.BlockSpec((TM, HP), lambda i: (i % TILES, 0)),
                  const((1, MLA_QR)), const((1, MLA_KVR)),
                  const((MLA_QR, 2 * nq)), const((MLA_KVR, nq + nv)), const((128, 2 * nq))],
        out_specs=[pl.BlockSpec((TM, nq), lambda i: (_lat_first(i), 0)),
                   pl.BlockSpec((TM, nq), lambda i: (i, 0)),
                   pl.BlockSpec((None, nv, TM), lambda i: (i // TILES, 0, i % TILES))],
        out_shape=[jax.ShapeDtypeStruct((R, nq), bf16), jax.ShapeDtypeStruct((R, nq), bf16),
                   jax.ShapeDtypeStruct((BATCH, nv, S), bf16)],
        compiler_params=_cp(("arbitrary",)),
        name="mla_prep",
    )(p, p, p, tc, ts, qn, kvn, wq2, wkv, e2)


TQ = 512


def _attn_body(q_ref, k_ref, vt_ref, o_ref, s_sc, n_chunks):
    q = q_ref[...]
    nq = q.shape[0]
    m = None
    for j in range(n_chunks):
        s = _nt(k_ref[j * TM:(j + 1) * TM, :], q)
        s_sc[j] = s
        cm = jnp.max(s, axis=0, keepdims=True)
        m = cm if m is None else jnp.maximum(m, cm)
    l = jnp.zeros((1, nq), f32)
    acc = jnp.zeros((MLA_V, nq), f32)
    for j in range(n_chunks):
        p = jnp.exp2(s_sc[j] - m)
        l = l + jnp.sum(p, axis=0, keepdims=True)
        acc = acc + _dot(vt_ref[:, j * TM:(j + 1) * TM], p.astype(bf16))
    o_ref[...] = (acc * (1.0 / l)).T.astype(bf16)


def _attn_ctx_kernel(q_ref, k_ref, vt_ref, ya_hbm_ref, o_ref, s_sc):
    del ya_hbm_ref
    _attn_body(q_ref, k_ref, vt_ref, o_ref, s_sc, 1)


def _attention(q, k, vt, has_ctx):
    nq = MLA_H * HP
    nv = MLA_H * MLA_V
    q3 = q.reshape(BATCH, S, nq)
    k3 = k.reshape(BATCH, S, nq)
    ya = pl.pallas_call(
        functools.partial(_attn_body, n_chunks=S // TM),
        grid=(BATCH, MLA_H, N_LAT // TQ),
        in_specs=[pl.BlockSpec((None, TQ, HP), lambda b, h, t: (b, t, h)),
                  pl.BlockSpec((None, S, HP), lambda b, h, t: (b, 0, h)),
                  pl.BlockSpec((None, MLA_V, S), lambda b, h, t: (b, h, 0))],
        out_specs=pl.BlockSpec((None, TQ, MLA_V), lambda b, h, t: (b, t, h)),
        out_shape=jax.ShapeDtypeStruct((BATCH, S, nv), bf16),
        scratch_shapes=[pltpu.VMEM((S // TM, TM, TQ), f32)],
        compiler_params=_cp(("arbitrary", "arbitrary", "arbitrary")),
        name="mla_attention",
    )(q3, k3, vt)
    if not has_ctx:
        return ya.reshape(R, nv)
    ctx_blk = N_LAT // N_CTX
    ya = pl.pallas_call(
        _attn_ctx_kernel,
        grid=(BATCH, MLA_H),
        in_specs=[pl.BlockSpec((None, N_CTX, HP), lambda b, h: (b, ctx_blk, h)),
                  pl.BlockSpec((None, N_CTX, HP), lambda b, h: (b, 0, h)),
                  pl.BlockSpec((None, MLA_V, N_CTX), lambda b, h: (b, h, 0)),
                  pl.BlockSpec(memory_space=pl.ANY)],
        out_specs=pl.BlockSpec((None, N_CTX, MLA_V), lambda b, h: (b, ctx_blk, h)),
        out_shape=jax.ShapeDtypeStruct((BATCH, S, nv), bf16),
        scratch_shapes=[pltpu.VMEM((1, TM, N_CTX), f32)],
        input_output_aliases={3: 0},
        compiler_params=_cp(("arbitrary", "arbitrary")),
        name="mla_attention_ctx",
    )(q3, k3, vt, ya)
    return ya.reshape(R, nv)


N_CHUNK = S // CHUNK
N_CCH = N_CTX // CHUNK
SCAN_H = 4
SCAN_V = 128
SCAN_W = SCAN_H * SCAN_V


def _scan_consts(kh):
    dk = kh // SCAN_H
    row = lax.broadcasted_iota(jnp.int32, (CHUNK, SCAN_H * CHUNK), 0)
    col = lax.broadcasted_iota(jnp.int32, (CHUNK, SCAN_H * CHUNK), 1) % CHUNK
    incl = row >= col
    strict = col > row
    krow = lax.broadcasted_iota(jnp.int32, (SCAN_H * CHUNK, kh), 0) // CHUNK
    kcol = lax.broadcasted_iota(jnp.int32, (SCAN_H * CHUNK, kh), 1) // dk
    kmask = krow == kcol
    vrow = lax.broadcasted_iota(jnp.int32, (SCAN_H * CHUNK, SCAN_W), 0) // CHUNK
    vcol = lax.broadcasted_iota(jnp.int32, (SCAN_H * CHUNK, SCAN_W), 1) // SCAN_V
    vmask = vrow == vcol
    srow = lax.broadcasted_iota(jnp.int32, (SCAN_W, kh), 0) // SCAN_V
    scol = lax.broadcasted_iota(jnp.int32, (SCAN_W, kh), 1) // dk
    smask = srow == scol
    return incl, strict, kmask, vmask, smask


def _chunk_step(q, k, v, cum, cend, st_ref, amask, kmask, vmask, smask):
    qd = (q * jnp.exp(cum)).astype(bf16)
    ki = k * jnp.exp(-cum)
    kend = (k * jnp.exp(cend - cum)).astype(bf16)
    dec = jnp.exp(cend)
    kst = jnp.where(kmask, jnp.concatenate([ki] * SCAN_H, axis=0), 0.0).astype(bf16)
    att = jnp.where(amask, _nt(qd, kst), 0.0).astype(bf16)
    vbd = jnp.where(vmask, jnp.concatenate([v] * SCAN_H, axis=0), jnp.zeros((), bf16))
    st = st_ref[...]
    o = _dot(att, vbd) + _nt(qd, st.astype(bf16))
    st_ref[...] = st * dec + jnp.where(smask, _tn(v, kend), 0.0)
    return o


def _bwd_chunk(i):
    return jnp.where(i < N_CCH, N_CCH - 1 - i, N_CHUNK + N_CCH - 1 - i)


def _scan_finish(of_sc, ob_sc, g_ref, y_ref):
    def fin(i, carry):
        r0 = pl.multiple_of(i * TM, TM)
        o = of_sc[pl.ds(r0, TM), :] + ob_sc[pl.ds(r0, TM), :]
        g = g_ref[pl.ds(r0, TM), :].astype(f32)
        for h in range(SCAN_H):
            sl = slice(h * SCAN_V, (h + 1) * SCAN_V)
            y_ref[pl.ds(r0, TM), sl] = (_rms(o[:, sl]) * _silu(g[:, sl])).astype(bf16)
        return carry

    lax.fori_loop(0, TILES, fin, 0)


def _gla_kernel(q_ref, k_ref, v_ref, krz_ref, og_ref, w2_ref, b2_ref, y_ref, cum_sc, of_sc, ob_sc, stf_sc, stb_sc):
    kh = GLA_H * GLA_DK
    ri = lax.broadcasted_iota(jnp.int32, (TM, TM), 0)
    ci = lax.broadcasted_iota(jnp.int32, (TM, TM), 1)
    same = (ri // CHUNK) == (ci // CHUNK)
    pre = jnp.where(same & (ci <= ri), 1.0, 0.0).astype(bf16)
    suf = jnp.where(same & (ci >= ri), 1.0, 0.0).astype(bf16)

    def exact_sum(m, x):
        hi = x.astype(bf16)
        r1 = x - hi.astype(f32)
        mid = r1.astype(bf16)
        lo = (r1 - mid.astype(f32)).astype(bf16)
        return _dot(m, hi) + _dot(m, mid) + _dot(m, lo)

    def gates(i, carry):
        r0 = pl.multiple_of(i * TM, TM)
        lg = _dot(krz_ref[pl.ds(r0, TM), :], w2_ref[...]) + b2_ref[...]
        la = (jnp.minimum(lg, 0.0) - jnp.log1p(jnp.exp(-jnp.abs(lg)))) * (1.0 / GLA_TAU)
        cum_sc[pl.ds(r0, TM), 0:kh] = exact_sum(pre, la[:, 0:kh])
        cum_sc[pl.ds(r0, TM), kh:2 * kh] = exact_sum(suf, la[:, kh:2 * kh])
        return carry

    lax.fori_loop(0, TILES, gates, 0)

    incl, strict, kmask, vmask, smask = _scan_consts(kh)
    stf_sc[...] = jnp.zeros_like(stf_sc)
    stb_sc[...] = jnp.zeros_like(stb_sc)
    qscale = GLA_DK ** -0.5

    def body(i, carry):
        rf = pl.multiple_of(i * CHUNK, CHUNK)
        cum = cum_sc[pl.ds(rf, CHUNK), 0:kh]
        of_sc[pl.ds(rf, CHUNK), :] = _chunk_step(
            q_ref[pl.ds(rf, CHUNK), :].astype(f32) * qscale, k_ref[pl.ds(rf, CHUNK), :].astype(f32),
            v_ref[pl.ds(rf, CHUNK), :], cum, cum[CHUNK - 1:CHUNK, :], stf_sc, incl, kmask, vmask, smask)
        rb = pl.multiple_of(_bwd_chunk(i) * CHUNK, CHUNK)
        rc = cum_sc[pl.ds(rb, CHUNK), kh:2 * kh]
        ob_sc[pl.ds(rb, CHUNK), :] = _chunk_step(
            q_ref[pl.ds(rb, CHUNK), :].astype(f32) * qscale, k_ref[pl.ds(rb, CHUNK), :].astype(f32),
            v_ref[pl.ds(rb, CHUNK), :], rc, rc[0:1, :], stb_sc, strict, kmask, vmask, smask)
        return carry

    lax.fori_loop(0, N_CHUNK, body, 0, unroll=4)
    _scan_finish(of_sc, ob_sc, og_ref, y_ref)


def _gla(p, w2p, b2):
    kh = GLA_H * GLA_DK
    p3 = p.reshape(BATCH, S, NC)
    col = lambda w, c: pl.BlockSpec((None, S, w), lambda b: (b, 0, c // w))
    return pl.pallas_call(
        _gla_kernel,
        grid=(BATCH,),
        in_specs=[col(kh, C_GQ), col(kh, C_GK), col(SCAN_W, C_GV), col(128, C_KRZ), col(SCAN_W, C_GOG),
                  pl.BlockSpec((128, 2 * kh), lambda b: (0, 0)), pl.BlockSpec((1, 2 * kh), lambda b: (0, 0))],
        out_specs=pl.BlockSpec((None, S, SCAN_W), lambda b: (b, 0, 0)),
        out_shape=jax.ShapeDtypeStruct((BATCH, S, SCAN_W), bf16),
        scratch_shapes=[pltpu.VMEM((S, 2 * kh), f32), pltpu.VMEM((S, SCAN_W), f32), pltpu.VMEM((S, SCAN_W), f32),
                        pltpu.VMEM((SCAN_W, kh), f32), pltpu.VMEM((SCAN_W, kh), f32)],
        compiler_params=_cp(("arbitrary",)),
        name="gla",
    )(p3, p3, p3, p3, p3, w2p, b2).reshape(R, SCAN_W)


def _ret_kernel(q_ref, k_ref, v_ref, g_ref, cos_ref, sin_ref, cumf_ref, cumb_ref, y_ref, of_sc, ob_sc, stf_sc, stb_sc):
    kh = RET_H * RET_DK
    incl, strict, kmask, vmask, smask = _scan_consts(kh)
    stf_sc[...] = jnp.zeros_like(stf_sc)
    stb_sc[...] = jnp.zeros_like(stb_sc)
    kscale = RET_DK ** -0.5
    cumf = cumf_ref[...]
    cumb = cumb_ref[...]

    even = lax.broadcasted_iota(jnp.int32, (CHUNK, RET_DK), 1) % 2 == 0

    def rotate(x, r0):
        cos = jnp.concatenate([cos_ref[pl.ds(r0, CHUNK), :]] * RET_H, axis=1)
        sin = jnp.concatenate([sin_ref[pl.ds(r0, CHUNK), :]] * RET_H, axis=1)
        parts = []
        for h in range(RET_H):
            xh = x[:, h * RET_DK:(h + 1) * RET_DK]
            parts.append(jnp.where(even, pltpu.roll(xh, RET_DK - 1, axis=1), pltpu.roll(xh, 1, axis=1)))
        return x * cos + jnp.concatenate(parts, axis=1) * sin

    def body(i, carry):
        rf = pl.multiple_of(i * CHUNK, CHUNK)
        of_sc[pl.ds(rf, CHUNK), :] = _chunk_step(
            rotate(q_ref[pl.ds(rf, CHUNK), :].astype(f32), rf),
            rotate(k_ref[pl.ds(rf, CHUNK), :].astype(f32), rf) * kscale,
            v_ref[pl.ds(rf, CHUNK), :], cumf, cumf[CHUNK - 1:CHUNK, :], stf_sc, incl, kmask, vmask, smask)
        rb = pl.multiple_of(_bwd_chunk(i) * CHUNK, CHUNK)
        ob_sc[pl.ds(rb, CHUNK), :] = _chunk_step(
            rotate(q_ref[pl.ds(rb, CHUNK), :].astype(f32), rb),
            rotate(k_ref[pl.ds(rb, CHUNK), :].astype(f32), rb) * kscale,
            v_ref[pl.ds(rb, CHUNK), :], cumb, cumb[0:1, :], stb_sc, strict, kmask, vmask, smask)
        return carry

    lax.fori_loop(0, N_CHUNK, body, 0, unroll=4)
    _scan_finish(of_sc, ob_sc, g_ref, y_ref)


def _retention(p, cos_r, sin_r, cumf, cumb):
    kh = RET_H * RET_DK
    p3 = p.reshape(BATCH, S, NC)
    col = lambda w, c: pl.BlockSpec((None, S, w), lambda b: (b, 0, c // w))
    const = lambda shape: pl.BlockSpec(shape, lambda b: (0, 0))
    return pl.pallas_call(
        _ret_kernel,
        grid=(BATCH,),
        in_specs=[col(kh, C_RQ), col(kh, C_RK), col(SCAN_W, C_RV), col(SCAN_W, C_RG),
                  const((S, RET_DK)), const((S, RET_DK)), const((CHUNK, kh)), const((CHUNK, kh))],
        out_specs=pl.BlockSpec((None, S, SCAN_W), lambda b: (b, 0, 0)),
        out_shape=jax.ShapeDtypeStruct((BATCH, S, SCAN_W), bf16),
        scratch_shapes=[pltpu.VMEM((S, SCAN_W), f32), pltpu.VMEM((S, SCAN_W), f32),
                        pltpu.VMEM((SCAN_W, kh), f32), pltpu.VMEM((SCAN_W, kh), f32)],
        compiler_params=_cp(("arbitrary",)),
        name="retention",
    )(p3, p3, p3, p3, cos_r, sin_r, cumf, cumb).reshape(R, SCAN_W)


LAT_TILES = N_LAT // TM


def _lat_tile(g):
    return (g // LAT_TILES) * TILES + 1 + g % LAT_TILES


def _out_kernel(*refs):
    *resid, ya_ref, yb_ref, yc_ref, mod_ref, w_ref, rwt_ref, xo_ref, h2_ref, aff_ref = refs
    na = MLA_H * MLA_V
    acc = _dot(ya_ref[...], w_ref[0:na, :])
    acc += _dot(yb_ref[...], w_ref[na:na + SCAN_W, :])
    acc += _dot(yc_ref[...], w_ref[na + SCAN_W:, :])
    x = _resid_tile(resid, pl.program_id(0) % TILES == 0) + mod_ref[2:3, :] * acc
    xo_ref[...] = x
    hb = (_rms(x) * (1.0 + mod_ref[4:5, :]) + mod_ref[3:4, :]).astype(bf16)
    h2_ref[...] = hb
    lg = _nt(rwt_ref[...], hb)
    e = jnp.exp(lg - jnp.max(lg, axis=0, keepdims=True))
    aff_ref[...] = e / jnp.sum(e, axis=0, keepdims=True)


def _out_proj(resid, ya, yb, yc, mods, w_out, rwt, layer, has_ctx):
    na = MLA_H * MLA_V
    tile = (lambda g: g) if has_ctx else _lat_tile
    row = lambda w: pl.BlockSpec((TM, w), lambda g: (tile(g), 0))
    return pl.pallas_call(
        _out_kernel,
        grid=(R // TM if has_ctx else BATCH * LAT_TILES,),
        in_specs=_resid_specs(resid, tile) + [
            pl.BlockSpec((TM, na), lambda g: (_lat_first(tile(g)), 0)), row(SCAN_W), row(SCAN_W),
            pl.BlockSpec((None, None, 6, D), lambda g: (layer, _mod_row(tile(g)), 0, 0)),
            pl.BlockSpec((D, D), lambda g: (0, 0)),
            pl.BlockSpec((N_EXP, D), lambda g: (0, 0))],
        out_specs=[row(D), row(D), pl.BlockSpec((N_EXP, TM), lambda g: (0, tile(g)))],
        out_shape=[jax.ShapeDtypeStruct((R, D), f32), jax.ShapeDtypeStruct((R, D), bf16),
                   jax.ShapeDtypeStruct((N_EXP, R), f32)],
        compiler_params=_cp(("arbitrary",)),
        name="out_proj",
    )(*resid, ya, yb, yc, mods, w_out, rwt)


def _topk_kernel(aff_ref, pos_ref, post_ref, *, has_ctx):
    ri = lax.broadcasted_iota(jnp.int32, (TM, TM), 0)
    ci = lax.broadcasted_iota(jnp.int32, (TM, TM), 1)
    before = jnp.where(ri < ci, 1.0, 0.0).astype(bf16)

    def prefix_count(m):
        out = []
        off = jnp.zeros((N_EXP, 1), f32)
        for blk in range(m.shape[1] // TM):
            mb = m[:, blk * TM:(blk + 1) * TM]
            out.append(_dot(mb.astype(bf16), before) + off)
            off = off + jnp.sum(mb, axis=1, keepdims=True)
        return jnp.concatenate(out, axis=1) if len(out) > 1 else out[0]

    def select(a, cap, base):
        bits = pltpu.bitcast(a, jnp.int32)
        capf = float(cap)

        def step(i, thr):
            cand = thr | jnp.left_shift(jnp.int32(1), 30 - i)
            cnt = jnp.sum(jnp.where(bits >= cand, 1.0, 0.0), axis=1, keepdims=True)
            return jnp.where(cnt >= capf, cand, thr)

        thr = lax.fori_loop(0, 31, step, jnp.zeros((N_EXP, 1), jnp.int32))
        gt = jnp.where(bits > thr, 1.0, 0.0)
        eq = jnp.where(bits == thr, 1.0, 0.0)
        need = capf - jnp.sum(gt, axis=1, keepdims=True)
        keep = gt + eq * jnp.where(prefix_count(eq) < need, 1.0, 0.0)
        return jnp.where(keep > 0.5, prefix_count(keep) + float(base), -1.0)

    pos_ctx = select(aff_ref[:, 0:N_CTX], CAP_CTX, CAP_LAT) if has_ctx else jnp.full((N_EXP, N_CTX), -1.0, f32)
    pos = jnp.concatenate([pos_ctx, select(aff_ref[:, N_CTX:], CAP_LAT, 0)], axis=1)
    pos_ref[...] = pos
    post_ref[...] = jnp.concatenate([pos, jnp.full((128 - N_EXP, S), -1.0, f32)], axis=0).T


def _topk(aff, has_ctx):
    return pl.pallas_call(
        functools.partial(_topk_kernel, has_ctx=has_ctx),
        grid=(BATCH,),
        in_specs=[pl.BlockSpec((N_EXP, S), lambda b: (0, b))],
        out_specs=[pl.BlockSpec((N_EXP, S), lambda b: (0, b)), pl.BlockSpec((S, 128), lambda b: (b, 0))],
        out_shape=[jax.ShapeDtypeStruct((N_EXP, R), f32), jax.ShapeDtypeStruct((R, 128), f32)],
        compiler_params=_cp(("arbitrary",)),
        name="route_topk",
    )(aff)


def _gather_kernel(pos_ref, aff_ref, h_ref, x_ref, g_ref, *, has_ctx):
    e = pl.program_id(1)
    t0 = 0 if has_ctx else N_CTX
    slots = x_ref.shape[0]
    prow = pos_ref[pl.ds(e, 1), t0:]
    arow = aff_ref[pl.ds(e, 1), t0:]
    slot = lax.broadcasted_iota(jnp.int32, (slots, S - t0), 0).astype(f32)
    hit = prow == slot
    onehot = jnp.where(hit, 1.0, 0.0).astype(bf16)
    x_ref[...] = _dot(onehot, h_ref[t0:, :]).astype(bf16)
    g = jnp.sum(jnp.where(hit, arow, 0.0), axis=1, keepdims=True)
    g_ref[...] = jnp.broadcast_to(g, (slots, 128))


def _gather(pos, aff, h2, has_ctx):
    slots = SLOTS if has_ctx else CAP_LAT
    return pl.pallas_call(
        functools.partial(_gather_kernel, has_ctx=has_ctx),
        grid=(BATCH, N_EXP),
        in_specs=[pl.BlockSpec((N_EXP, S), lambda b, e: (0, b)),
                  pl.BlockSpec((N_EXP, S), lambda b, e: (0, b)),
                  pl.BlockSpec((S, D), lambda b, e: (b, 0))],
        out_specs=[pl.BlockSpec((None, slots, D), lambda b, e: (e, b, 0)),
                   pl.BlockSpec((None, slots, 128), lambda b, e: (e, b, 0))],
        out_shape=[jax.ShapeDtypeStruct((N_EXP, BATCH * slots, D), bf16),
                   jax.ShapeDtypeStruct((N_EXP, BATCH * slots, 128), f32)],
        compiler_params=_cp(("arbitrary", "arbitrary")),
        name="moe_gather",
    )(pos, aff, h2)


TF = 256
N_UP = EXP_FF // TF
N_DOWN = D // TF


def _moe_kernel(x_ref, wg_ref, wu_ref, wd_ref, g_ref, y_ref, hm_ref):
    s = pl.program_id(1)

    @pl.when(s < N_UP)
    def _():
        x = x_ref[...]
        a = _dot(x, wg_ref[...].astype(bf16))
        u = _dot(x, wu_ref[...].astype(bf16))
        hm_ref[s] = (_silu(a) * u).astype(bf16)

    @pl.when(s >= N_UP)
    def _():
        wd = wd_ref[...].astype(bf16)
        acc = _dot(hm_ref[0], wd[0:TF, :])
        for c in range(1, N_UP):
            acc += _dot(hm_ref[c], wd[c * TF:(c + 1) * TF, :])
        y_ref[...] = (acc * g_ref[:, 0:1]).astype(bf16)


def _moe_ffn(xg, gs, w_gate, w_up, w_down, layer):
    rows = xg.shape[1]
    up = lambda s: jnp.minimum(s, N_UP - 1)
    down = lambda s: jnp.maximum(s - N_UP, 0)
    return pl.pallas_call(
        _moe_kernel,
        grid=(N_EXP, N_UP + N_DOWN),
        in_specs=[pl.BlockSpec((None, rows, D), lambda e, s: (e, 0, 0)),
                  pl.BlockSpec((None, None, D, TF), lambda e, s: (layer, e, 0, up(s))),
                  pl.BlockSpec((None, None, D, TF), lambda e, s: (layer, e, 0, up(s))),
                  pl.BlockSpec((None, None, EXP_FF, TF), lambda e, s: (layer, e, 0, down(s))),
                  pl.BlockSpec((None, rows, 128), lambda e, s: (e, 0, 0))],
        out_specs=pl.BlockSpec((None, rows, TF), lambda e, s: (e, 0, down(s))),
        out_shape=jax.ShapeDtypeStruct((N_EXP, rows, D), bf16),
        scratch_shapes=[pltpu.VMEM((N_UP, rows, TF), bf16)],
        compiler_params=_cp(("arbitrary", "arbitrary")),
        name="moe_ffn",
    )(xg, w_gate, w_up, w_down, gs)


TN_C = 1024


def _combine_kernel(post_ref, ys_ref, x_ref, mod_ref, o_ref, *, has_ctx):
    t = pl.program_id(2) if has_ctx else pl.program_id(2) + 1
    pb = post_ref[...]

    def scatter(cap, base, ys):
        pc = pb - float(base)
        pc = jnp.where((pc >= 0.0) & (pc < float(cap)), pc, -1.0).astype(bf16)
        er = lax.broadcasted_iota(jnp.int32, (128, N_EXP * cap), 0)
        ec = lax.broadcasted_iota(jnp.int32, (128, N_EXP * cap), 1) // cap
        rep = jnp.where(er == ec, 1.0, 0.0).astype(bf16)
        slot = (lax.broadcasted_iota(jnp.int32, (TM, N_EXP * cap), 1) % cap).astype(f32)
        onehot = jnp.where(_dot(pc, rep) == slot, 1.0, 0.0).astype(bf16)
        o_ref[...] = x_ref[...] + mod_ref[5:6, :] * _dot(onehot, ys)

    if has_ctx:
        @pl.when(t == 0)
        def _():
            scatter(CAP_CTX, CAP_LAT, ys_ref[:, CAP_LAT:SLOTS, :].reshape(N_EXP * CAP_CTX, TN_C))

    @pl.when(t > 0)
    def _():
        scatter(CAP_LAT, 0, ys_ref[:, 0:CAP_LAT, :].reshape(N_EXP * CAP_LAT, TN_C))


def _combine(post, ys, xa, mods, layer, has_ctx):
    slots = SLOTS if has_ctx else CAP_LAT
    t0 = 0 if has_ctx else 1
    tile = lambda b, t: b * TILES + t + t0
    return pl.pallas_call(
        functools.partial(_combine_kernel, has_ctx=has_ctx),
        grid=(BATCH, D // TN_C, TILES - t0),
        in_specs=[pl.BlockSpec((TM, 128), lambda b, n, t: (tile(b, t), 0)),
                  pl.BlockSpec((N_EXP, None, slots, TN_C), lambda b, n, t: (0, b, 0, n)),
                  pl.BlockSpec((TM, TN_C), lambda b, n, t: (tile(b, t), n)),
                  pl.BlockSpec((None, None, 6, TN_C), lambda b, n, t: (layer, _mod_row(tile(b, t)), 0, n))],
        out_specs=pl.BlockSpec((TM, TN_C), lambda b, n, t: (tile(b, t), n)),
        out_shape=jax.ShapeDtypeStruct((R, D), f32),
        compiler_params=_cp(("arbitrary", "arbitrary", "arbitrary")),
        name="moe_combine",
    )(post, ys.reshape(N_EXP, BATCH, slots, D), xa, mods)


def _final_kernel(x_ref, g_ref, o_ref):
    o_ref[...] = _rms(x_ref[...]) * g_ref[...]


def _final_norm(xa, gain):
    lt = N_LAT // TM
    return pl.pallas_call(
        _final_kernel,
        grid=(BATCH, lt),
        in_specs=[pl.BlockSpec((TM, D), lambda b, t: (b * TILES + 1 + t, 0)),
                  pl.BlockSpec((1, D), lambda b, t: (0, 0))],
        out_specs=pl.BlockSpec((None, TM, D), lambda b, t: (b, t, 0)),
        out_shape=jax.ShapeDtypeStruct((BATCH, N_LAT, D), f32),
        compiler_params=_cp(("arbitrary", "arbitrary")),
        name="final_norm",
    )(xa, gain.reshape(1, D))


def _in_weights(w):
    o = np.cumsum((0, MLA_QR, MLA_KVR, MLA_ROPE, 256, 256, 512, 32, 512, 512, 512, 512, 512))
    piece = lambda i, j: w[:, o[i]:o[j]]
    cq, ckv, kr, gq, gk, gv, gz, gog = (piece(i, i + 1) for i in range(8))
    ret = piece(8, 12)
    pad = jnp.zeros((D, NC - (C_KRZ + MLA_ROPE + 2 * GLA_RANK)), w.dtype)
    return jnp.concatenate([ret, gv, gog, cq, ckv, gq, gk, kr, gz, pad], axis=1).astype(bf16)


def _mla_weights(w_uq, w_ukv):
    half = MLA_ROPE // 2
    wq = w_uq.reshape(MLA_QR, MLA_H, MLA_NOPE + MLA_ROPE)
    nope = wq[:, :, :MLA_NOPE]
    rope = wq[:, :, MLA_NOPE:].reshape(MLA_QR, MLA_H, half, 2)
    ev, od = rope[..., 0], rope[..., 1]
    zpad = jnp.zeros((MLA_QR, MLA_H, HP - MLA_NOPE - MLA_ROPE), w_uq.dtype)
    q_main = jnp.concatenate([nope, ev, od, zpad], axis=-1).reshape(MLA_QR, MLA_H * HP)
    q_part = jnp.concatenate([jnp.zeros_like(nope), od, ev, zpad], axis=-1).reshape(MLA_QR, MLA_H * HP)
    wkv = w_ukv.reshape(MLA_KVR, MLA_H, MLA_NOPE + MLA_V)
    k_main = jnp.concatenate([wkv[:, :, :MLA_NOPE], jnp.zeros((MLA_KVR, MLA_H, HP - MLA_NOPE), w_ukv.dtype)], axis=-1)
    v_main = wkv[:, :, MLA_NOPE:]
    return (jnp.concatenate([q_main, q_part], axis=1).astype(bf16),
            jnp.concatenate([k_main.reshape(MLA_KVR, MLA_H * HP), v_main.reshape(MLA_KVR, MLA_H * MLA_V)], axis=1).astype(bf16))


def _rope_key_placement():
    half = MLA_ROPE // 2
    nq = MLA_H * HP
    e2 = np.zeros((128, 2 * nq), np.float32)
    for h in range(MLA_H):
        for i in range(half):
            ev, od = h * HP + MLA_NOPE + i, h * HP + MLA_NOPE + half + i
            e2[2 * i, ev] = e2[2 * i + 1, od] = 1.0
            e2[2 * i + 1, nq + ev] = e2[2 * i, nq + od] = 1.0
    return jnp.asarray(e2, bf16)


def _tables():
    rows = N_LAT // GRID_W
    row = np.repeat(np.arange(rows, dtype=np.float32), GRID_W)
    colp = np.tile(np.arange(GRID_W, dtype=np.float32), rows)
    n_freq = MLA_ROPE // 4
    inv = jnp.power(ROPE_BASE, -jnp.arange(n_freq, dtype=f32) / n_freq)
    ang = jnp.concatenate([row[:, None] * inv, colp[:, None] * inv], axis=-1)
    cos_a, sin_a = jnp.cos(ang), jnp.sin(ang)
    one = jnp.ones((N_LAT, MLA_NOPE), f32)
    zpad = jnp.zeros((N_LAT, HP - MLA_NOPE - MLA_ROPE), f32)
    tc_lat = jnp.concatenate([one, cos_a, cos_a, zpad], axis=1)
    ts_lat = jnp.concatenate([0 * one, -sin_a, sin_a, zpad], axis=1)
    tc_ctx = jnp.concatenate([jnp.ones((N_CTX, MLA_NOPE + MLA_ROPE), f32), jnp.zeros((N_CTX, HP - MLA_NOPE - MLA_ROPE), f32)], axis=1)
    tc = jnp.concatenate([tc_ctx, tc_lat], axis=0)
    ts = jnp.concatenate([jnp.zeros((N_CTX, HP), f32), ts_lat], axis=0)

    inv_r = 1.0 / jnp.power(ROPE_BASE, jnp.linspace(0.0, 1.0, RET_DK // 2, dtype=f32))
    ang_r = jnp.arange(N_LAT, dtype=f32)[:, None] * inv_r
    cos_r = jnp.concatenate([jnp.ones((N_CTX, RET_DK), f32), jnp.repeat(jnp.cos(ang_r), 2, axis=1)], axis=0)
    sin_r = jnp.concatenate([jnp.zeros((N_CTX, RET_DK), f32),
                             jnp.stack([-jnp.sin(ang_r), jnp.sin(ang_r)], axis=-1).reshape(N_LAT, RET_DK)], axis=0)

    def log_decay(direction):
        e = RET_EXP0 + direction + 2.0 * jnp.arange(RET_H, dtype=f32)
        return jnp.repeat(jnp.log1p(-jnp.exp2(-e)), RET_DK)[None, :]

    steps = jnp.arange(1, CHUNK + 1, dtype=f32)[:, None]
    cumf = steps * log_decay(0.0)
    cumb = steps[::-1] * log_decay(1.0)
    return tc, ts, cos_r, sin_r, cumf, cumb


def kernel(x, c, ctx, c_ctx, ada_w, ada_b, w_in, mla_q_norm, mla_w_uq, mla_kv_norm, mla_w_ukv, gla_gate_w2,
           gla_gate_b, w_out, router_w, exp_w_gate, exp_w_up, exp_w_down, final_norm):
    resid = (x, ctx)
    cc = jnp.concatenate([c, c_ctx[None, :], jnp.zeros((8 - BATCH - 1, D), f32)], axis=0)
    mods = _modulation(cc, ada_w, ada_b).reshape(DEPTH, 8, 6, D)
    tc, ts, cos_r, sin_r, cumf, cumb = _tables()
    e2 = _rope_key_placement()
    kh = GLA_H * GLA_DK

    for l in range(DEPTH):
        w_in_p = _in_weights(w_in[l])
        wq2, wkv = _mla_weights(mla_w_uq[l], mla_w_ukv[l])
        w2p = jnp.zeros((128, 2 * kh), f32)
        w2p = w2p.at[64:64 + GLA_RANK, 0:kh].set(gla_gate_w2[l, 0]).at[64 + GLA_RANK:64 + 2 * GLA_RANK, kh:].set(gla_gate_w2[l, 1])
        b2 = gla_gate_b[l].reshape(1, 2 * kh)

        has_ctx = l < DEPTH - 1
        p = _in_proj(resid, mods, w_in_p, l)
        q, k, vt = _mla_prep(p, tc, ts, mla_q_norm[l].reshape(1, -1), mla_kv_norm[l].reshape(1, -1), wq2, wkv, e2)
        ya = _attention(q, k, vt, has_ctx)
        yb = _gla(p, w2p.astype(bf16), b2)
        yc = _retention(p, cos_r, sin_r, cumf, cumb)
        xa, h2, aff = _out_proj(resid, ya, yb, yc, mods, w_out[l].astype(bf16), router_w[l].T.astype(bf16), l, has_ctx)
        pos, post = _topk(aff, has_ctx)
        xg, gs = _gather(pos, aff, h2, has_ctx)
        ys = _moe_ffn(xg, gs, exp_w_gate, exp_w_up, exp_w_down, l)
        resid = (_combine(post, ys, xa, mods, l, has_ctx),)
    return _final_norm(resid[0], final_norm)
```

```python
import functools

import numpy as np
import jax
import jax.numpy as jnp
from jax import lax
from jax.experimental import pallas as pl
from jax.experimental.pallas import tpu as pltpu

f32 = jnp.float32
bf16 = jnp.bfloat16

D = 2048
BATCH = 4
N_LAT = 2048
N_CTX = 256
S = N_CTX + N_LAT
R = BATCH * S
DEPTH = 2
GRID_W = 64
EPS = 1e-6
LOG2E = 1.4426950408889634
ROPE_BASE = 10000.0
CHUNK = 64

MLA_H, MLA_QR, MLA_KVR, MLA_NOPE, MLA_ROPE, MLA_V = 8, 512, 256, 128, 64, 128
GLA_H, GLA_DK, GLA_DV, GLA_RANK, GLA_TAU = 4, 64, 128, 16, 16.0
RET_H, RET_DK, RET_DV, RET_EXP0 = 4, 128, 128, 5.0
N_EXP, EXP_FF, EC_CAP = 16, 2048, 2
CAP_LAT = EC_CAP * N_LAT // N_EXP
CAP_CTX = EC_CAP * N_CTX // N_EXP
SLOTS = CAP_LAT + CAP_CTX

TM = 256
TILES = S // TM
LAT_TILES = N_LAT // TM
HP = 256

C_RQ, C_RK, C_RV, C_RG = 0, 512, 1024, 1536
C_GV, C_GOG, C_CQ, C_CKV, C_GQ, C_GK, C_KRZ = 2048, 2560, 3072, 3584, 3840, 4096, 4352
NC = 4608
TN_IN = 1536

VMEM_LIMIT = 56 * 1024 * 1024


def _cp(sem):
    return pltpu.CompilerParams(dimension_semantics=sem, vmem_limit_bytes=VMEM_LIMIT)


def _nt(a, b):
    return lax.dot_general(a, b, (((1,), (1,)), ((), ())), preferred_element_type=f32)


def _tn(a, b):
    return lax.dot_general(a, b, (((0,), (0,)), ((), ())), preferred_element_type=f32)


def _dot(a, b):
    return jnp.dot(a, b, preferred_element_type=f32)


def _rms(x):
    return x * lax.rsqrt(jnp.mean(x * x, axis=-1, keepdims=True) + EPS)


def _silu(x):
    return x * (1.0 / (1.0 + jnp.exp(-x)))


def _mod_row(i):
    return jnp.where(i % TILES == 0, BATCH, i // TILES)


def _lat_first(i):
    return (i // TILES) * TILES + (i % TILES + TILES - 1) % TILES


def _mod_kernel(s_ref, w_ref, b_ref, o_ref):
    s = _silu(s_ref[...]).astype(bf16)
    o_ref[...] = _dot(s, w_ref[...].astype(bf16)) + b_ref[...]


def _modulation(cc, ada_w, ada_b):
    tn = 1024
    return pl.pallas_call(
        _mod_kernel,
        grid=(DEPTH, 6 * D // tn),
        in_specs=[pl.BlockSpec((8, D), lambda l, j: (0, 0)),
                  pl.BlockSpec((None, D, tn), lambda l, j: (l, 0, j)),
                  pl.BlockSpec((None, 1, tn), lambda l, j: (l, 0, j))],
        out_specs=pl.BlockSpec((None, 8, tn), lambda l, j: (l, 0, j)),
        out_shape=jax.ShapeDtypeStruct((DEPTH, 8, 6 * D), f32),
        compiler_params=_cp(("arbitrary", "arbitrary")),
        name="modulation",
    )(cc, ada_w, ada_b.reshape(DEPTH, 1, 6 * D))


def _tile_specs(arrs, tile_of):
    w = arrs[0].shape[-1]
    if arrs[0].ndim == 2:
        return [pl.BlockSpec((TM, w), lambda *g: (tile_of(*g), 0))]
    lat = pl.BlockSpec((None, TM, w), lambda *g: (tile_of(*g) // TILES, jnp.maximum(tile_of(*g) % TILES - 1, 0), 0))
    if len(arrs) == 1:
        return [lat]
    return [lat, pl.BlockSpec((None, TM, w), lambda *g: (tile_of(*g) // TILES, 0, 0))]


def _tile_value(refs, is_ctx):
    if len(refs) == 1:
        return refs[0][...]
    return jnp.where(is_ctx, refs[1][...], refs[0][...])


def _in_kernel(*refs):
    *resid, mod_ref, w_ref, o_ref = refs
    x = _tile_value(resid, pl.program_id(0) % TILES == 0)
    h = (_rms(x) * (1.0 + mod_ref[1:2, :]) + mod_ref[0:1, :]).astype(bf16)
    for j in range(NC // TN_IN):
        o_ref[:, j * TN_IN:(j + 1) * TN_IN] = _dot(h, w_ref[:, j * TN_IN:(j + 1) * TN_IN]).astype(bf16)


def _in_proj(resid, mods, w_in_p, layer):
    return pl.pallas_call(
        _in_kernel,
        grid=(R // TM,),
        in_specs=_tile_specs(resid, lambda i: i) + [
            pl.BlockSpec((None, None, 6, D), lambda i: (layer, _mod_row(i), 0, 0)),
            pl.BlockSpec((D, NC), lambda i: (0, 0), pipeline_mode=pl.Buffered(1))],
        out_specs=pl.BlockSpec((TM, NC), lambda i: (i, 0)),
        out_shape=jax.ShapeDtypeStruct((R, NC), bf16),
        compiler_params=_cp(("arbitrary",)),
        name="in_proj",
    )(*resid, mods, w_in_p)


def _mla_prep_kernel(cq_ref, ckv_ref, krz_ref, tc_ref, ts_ref, qn_ref, kvn_ref, wq_ref, wkv_ref, e2_ref,
                     q_ref, k_ref, vt_ref):
    tc = tc_ref[...]
    ts = ts_ref[...]
    scale = (MLA_NOPE + MLA_ROPE) ** -0.5 * LOG2E
    hq = (_rms(cq_ref[...].astype(f32)) * qn_ref[...]).astype(bf16)
    q2 = _dot(hq, wq_ref[...])
    hkv = (_rms(ckv_ref[...].astype(f32)) * kvn_ref[...]).astype(bf16)
    kv = _dot(hkv, wkv_ref[...])
    kr2 = _dot(krz_ref[...], e2_ref[...])
    nq = MLA_H * HP
    for h in range(MLA_H):
        sl = slice(h * HP, (h + 1) * HP)
        sl2 = slice(nq + h * HP, nq + (h + 1) * HP)
        q_ref[:, sl] = ((q2[:, sl] * tc + q2[:, sl2] * ts) * scale).astype(bf16)
        k_ref[:, sl] = (kv[:, sl] + kr2[:, sl] * tc + kr2[:, sl2] * ts).astype(bf16)
    vt_ref[...] = kv[:, nq:].T.astype(bf16)


def _mla_prep(p, tc, ts, qn, kvn, wq2, wkv, e2):
    nq = MLA_H * HP
    nv = MLA_H * MLA_V
    const = lambda shape: pl.BlockSpec(shape, lambda i: (0, 0))
    return pl.pallas_call(
        _mla_prep_kernel,
        grid=(R // TM,),
        in_specs=[pl.BlockSpec((TM, MLA_QR), lambda i: (i, C_CQ // MLA_QR)),
                  pl.BlockSpec((TM, MLA_KVR), lambda i: (i, C_CKV // MLA_KVR)),
                  pl.BlockSpec((TM, 128), lambda i: (i, C_KRZ // 128)),
                  pl.BlockSpec((TM, HP), lambda i: (i % TILES, 0)),
                  pl.BlockSpec((TM, HP), lambda i: (i % TILES, 0)),
                  const((1, MLA_QR)), const((1, MLA_KVR)),
                  const((MLA_QR, 2 * nq)), const((MLA_KVR, nq + nv)), const((128, 2 * nq))],
        out_specs=[pl.BlockSpec((TM, nq), lambda i: (_lat_first(i), 0)),
                   pl.BlockSpec((TM, nq), lambda i: (i, 0)),
                   pl.BlockSpec((None, nv, TM), lambda i: (i // TILES, 0, i % TILES))],
        out_shape=[jax.ShapeDtypeStruct((R, nq), bf16), jax.ShapeDtypeStruct((R, nq), bf16),
                   jax.ShapeDtypeStruct((BATCH, nv, S), bf16)],
        compiler_params=_cp(("arbitrary",)),
        name="mla_prep",
    )(p, p, p, tc, ts, qn, kvn, wq2, wkv, e2)


TQ = 512


def _attn_body(q_ref, k_ref, vt_ref, o_ref, s_sc, n_chunks):
    q = q_ref[...]
    nq = q.shape[0]
    m = None
    for j in range(n_chunks):
        s = _nt(k_ref[j * TM:(j + 1) * TM, :], q)
        s_sc[j] = s
        cm = jnp.max(s, axis=0, keepdims=True)
        m = cm if m is None else jnp.maximum(m, cm)
    l = jnp.zeros((1, nq), f32)
    acc = jnp.zeros((MLA_V, nq), f32)
    for j in range(n_chunks):
        p = jnp.exp2(s_sc[j] - m)
        l = l + jnp.sum(p, axis=0, keepdims=True)
        acc = acc + _dot(vt_ref[:, j * TM:(j + 1) * TM], p.astype(bf16))
    o_ref[...] = (acc * (1.0 / l)).T.astype(bf16)


def _attention(q, k, vt, has_ctx):
    nq = MLA_H * HP
    nv = MLA_H * MLA_V
    q3 = q.reshape(BATCH, S, nq)
    k3 = k.reshape(BATCH, S, nq)
    ya = pl.pallas_call(
        functools.partial(_attn_body, n_chunks=S // TM),
        grid=(BATCH, MLA_H, N_LAT // TQ),
        in_specs=[pl.BlockSpec((None, TQ, HP), lambda b, h, t: (b, t, h)),
                  pl.BlockSpec((None, S, HP), lambda b, h, t: (b, 0, h)),
                  pl.BlockSpec((None, MLA_V, S), lambda b, h, t: (b, h, 0))],
        out_specs=pl.BlockSpec((None, TQ, MLA_V), lambda b, h, t: (b, t, h)),
        out_shape=jax.ShapeDtypeStruct((BATCH, N_LAT, nv), bf16),
        scratch_shapes=[pltpu.VMEM((S // TM, TM, TQ), f32)],
        compiler_params=_cp(("arbitrary", "arbitrary", "arbitrary")),
        name="mla_attention",
    )(q3, k3, vt)
    if not has_ctx:
        return (ya,)
    ya_ctx = pl.pallas_call(
        functools.partial(_attn_body, n_chunks=1),
        grid=(BATCH, MLA_H),
        in_specs=[pl.BlockSpec((None, N_CTX, HP), lambda b, h: (b, N_LAT // N_CTX, h)),
                  pl.BlockSpec((None, N_CTX, HP), lambda b, h: (b, 0, h)),
                  pl.BlockSpec((None, MLA_V, N_CTX), lambda b, h: (b, h, 0))],
        out_specs=pl.BlockSpec((None, N_CTX, MLA_V), lambda b, h: (b, 0, h)),
        out_shape=jax.ShapeDtypeStruct((BATCH, N_CTX, nv), bf16),
        scratch_shapes=[pltpu.VMEM((1, TM, N_CTX), f32)],
        compiler_params=_cp(("arbitrary", "arbitrary")),
        name="mla_attention_ctx",
    )(q3, k3, vt)
    return (ya, ya_ctx)


N_CHUNK = S // CHUNK
N_CCH = N_CTX // CHUNK
SCAN_H = 4
SCAN_V = 128
SCAN_W = SCAN_H * SCAN_V


def _scan_consts(kh):
    dk = kh // SCAN_H
    row = lax.broadcasted_iota(jnp.int32, (CHUNK, SCAN_H * CHUNK), 0)
    col = lax.broadcasted_iota(jnp.int32, (CHUNK, SCAN_H * CHUNK), 1) % CHUNK
    incl = row >= col
    strict = col > row
    krow = lax.broadcasted_iota(jnp.int32, (SCAN_H * CHUNK, kh), 0) // CHUNK
    kcol = lax.broadcasted_iota(jnp.int32, (SCAN_H * CHUNK, kh), 1) // dk
    kmask = krow == kcol
    vrow = lax.broadcasted_iota(jnp.int32, (SCAN_H * CHUNK, SCAN_W), 0) // CHUNK
    vcol = lax.broadcasted_iota(jnp.int32, (SCAN_H * CHUNK, SCAN_W), 1) // SCAN_V
    vmask = vrow == vcol
    srow = lax.broadcasted_iota(jnp.int32, (SCAN_W, kh), 0) // SCAN_V
    scol = lax.broadcasted_iota(jnp.int32, (SCAN_W, kh), 1) // dk
    smask = srow == scol
    return incl, strict, kmask, vmask, smask


def _chunk_step(q, k, v, cum, cend, st_ref, amask, kmask, vmask, smask):
    qd = (q * jnp.exp(cum)).astype(bf16)
    ki = k * jnp.exp(-cum)
    kend = (k * jnp.exp(cend - cum)).astype(bf16)
    dec = jnp.exp(cend)
    kst = jnp.where(kmask, jnp.concatenate([ki] * SCAN_H, axis=0), 0.0).astype(bf16)
    att = jnp.where(amask, _nt(qd, kst), 0.0).astype(bf16)
    vbd = jnp.where(vmask, jnp.concatenate([v] * SCAN_H, axis=0), jnp.zeros((), bf16))
    st = st_ref[...]
    o = _dot(att, vbd) + _nt(qd, st.astype(bf16))
    st_ref[...] = st * dec + jnp.where(smask, _tn(v, kend), 0.0)
    return o


def _bwd_chunk(i):
    return jnp.where(i < N_CCH, N_CCH - 1 - i, N_CHUNK + N_CCH - 1 - i)


def _scan_finish(of_sc, ob_sc, g_ref, y_ref):
    def fin(i, carry):
        r0 = pl.multiple_of(i * TM, TM)
        o = of_sc[pl.ds(r0, TM), :] + ob_sc[pl.ds(r0, TM), :]
        g = g_ref[pl.ds(r0, TM), :].astype(f32)
        for h in range(SCAN_H):
            sl = slice(h * SCAN_V, (h + 1) * SCAN_V)
            y_ref[pl.ds(r0, TM), sl] = (_rms(o[:, sl]) * _silu(g[:, sl])).astype(bf16)
        return carry

    lax.fori_loop(0, TILES, fin, 0)


def _gla_kernel(q_ref, k_ref, v_ref, krz_ref, og_ref, w2_ref, b2_ref, y_ref, cum_sc, of_sc, ob_sc, stf_sc, stb_sc):
    kh = GLA_H * GLA_DK
    ri = lax.broadcasted_iota(jnp.int32, (TM, TM), 0)
    ci = lax.broadcasted_iota(jnp.int32, (TM, TM), 1)
    same = (ri // CHUNK) == (ci // CHUNK)
    pre = jnp.where(same & (ci <= ri), 1.0, 0.0).astype(bf16)
    suf = jnp.where(same & (ci >= ri), 1.0, 0.0).astype(bf16)

    def exact_sum(m, x):
        hi = x.astype(bf16)
        r1 = x - hi.astype(f32)
        mid = r1.astype(bf16)
        lo = (r1 - mid.astype(f32)).astype(bf16)
        return _dot(m, hi) + _dot(m, mid) + _dot(m, lo)

    def gates(i, carry):
        r0 = pl.multiple_of(i * TM, TM)
        lg = _dot(krz_ref[pl.ds(r0, TM), :], w2_ref[...]) + b2_ref[...]
        la = (jnp.minimum(lg, 0.0) - jnp.log1p(jnp.exp(-jnp.abs(lg)))) * (1.0 / GLA_TAU)
        cum_sc[pl.ds(r0, TM), 0:kh] = exact_sum(pre, la[:, 0:kh])
        cum_sc[pl.ds(r0, TM), kh:2 * kh] = exact_sum(suf, la[:, kh:2 * kh])
        return carry

    lax.fori_loop(0, TILES, gates, 0)

    incl, strict, kmask, vmask, smask = _scan_consts(kh)
    stf_sc[...] = jnp.zeros_like(stf_sc)
    stb_sc[...] = jnp.zeros_like(stb_sc)
    qscale = GLA_DK ** -0.5

    def body(i, carry):
        rf = pl.multiple_of(i * CHUNK, CHUNK)
        cum = cum_sc[pl.ds(rf, CHUNK), 0:kh]
        of_sc[pl.ds(rf, CHUNK), :] = _chunk_step(
            q_ref[pl.ds(rf, CHUNK), :].astype(f32) * qscale, k_ref[pl.ds(rf, CHUNK), :].astype(f32),
            v_ref[pl.ds(rf, CHUNK), :], cum, cum[CHUNK - 1:CHUNK, :], stf_sc, incl, kmask, vmask, smask)
        rb = pl.multiple_of(_bwd_chunk(i) * CHUNK, CHUNK)
        rc = cum_sc[pl.ds(rb, CHUNK), kh:2 * kh]
        ob_sc[pl.ds(rb, CHUNK), :] = _chunk_step(
            q_ref[pl.ds(rb, CHUNK), :].astype(f32) * qscale, k_ref[pl.ds(rb, CHUNK), :].astype(f32),
            v_ref[pl.ds(rb, CHUNK), :], rc, rc[0:1, :], stb_sc, strict, kmask, vmask, smask)
        return carry

    lax.fori_loop(0, N_CHUNK, body, 0, unroll=4)
    _scan_finish(of_sc, ob_sc, og_ref, y_ref)


def _gla(p, w2p, b2):
    kh = GLA_H * GLA_DK
    p3 = p.reshape(BATCH, S, NC)
    col = lambda w, c: pl.BlockSpec((None, S, w), lambda b: (b, 0, c // w))
    return pl.pallas_call(
        _gla_kernel,
        grid=(BATCH,),
        in_specs=[col(kh, C_GQ), col(kh, C_GK), col(SCAN_W, C_GV), col(128, C_KRZ), col(SCAN_W, C_GOG),
                  pl.BlockSpec((128, 2 * kh), lambda b: (0, 0)), pl.BlockSpec((1, 2 * kh), lambda b: (0, 0))],
        out_specs=pl.BlockSpec((None, S, SCAN_W), lambda b: (b, 0, 0)),
        out_shape=jax.ShapeDtypeStruct((BATCH, S, SCAN_W), bf16),
        scratch_shapes=[pltpu.VMEM((S, 2 * kh), f32), pltpu.VMEM((S, SCAN_W), f32), pltpu.VMEM((S, SCAN_W), f32),
                        pltpu.VMEM((SCAN_W, kh), f32), pltpu.VMEM((SCAN_W, kh), f32)],
        compiler_params=_cp(("arbitrary",)),
        name="gla",
    )(p3, p3, p3, p3, p3, w2p, b2).reshape(R, SCAN_W)


def _ret_kernel(q_ref, k_ref, v_ref, g_ref, cos_ref, sin_ref, cumf_ref, cumb_ref, y_ref, of_sc, ob_sc, stf_sc, stb_sc):
    kh = RET_H * RET_DK
    incl, strict, kmask, vmask, smask = _scan_consts(kh)
    stf_sc[...] = jnp.zeros_like(stf_sc)
    stb_sc[...] = jnp.zeros_like(stb_sc)
    kscale = RET_DK ** -0.5
    cumf = cumf_ref[...]
    cumb = cumb_ref[...]

    even = lax.broadcasted_iota(jnp.int32, (CHUNK, RET_DK), 1) % 2 == 0

    def rotate(x, r0):
        cos = jnp.concatenate([cos_ref[pl.ds(r0, CHUNK), :]] * RET_H, axis=1)
        sin = jnp.concatenate([sin_ref[pl.ds(r0, CHUNK), :]] * RET_H, axis=1)
        parts = []
        for h in range(RET_H):
            xh = x[:, h * RET_DK:(h + 1) * RET_DK]
            parts.append(jnp.where(even, pltpu.roll(xh, RET_DK - 1, axis=1), pltpu.roll(xh, 1, axis=1)))
        return x * cos + jnp.concatenate(parts, axis=1) * sin

    def body(i, carry):
        rf = pl.multiple_of(i * CHUNK, CHUNK)
        of_sc[pl.ds(rf, CHUNK), :] = _chunk_step(
            rotate(q_ref[pl.ds(rf, CHUNK), :].astype(f32), rf),
            rotate(k_ref[pl.ds(rf, CHUNK), :].astype(f32), rf) * kscale,
            v_ref[pl.ds(rf, CHUNK), :], cumf, cumf[CHUNK - 1:CHUNK, :], stf_sc, incl, kmask, vmask, smask)
        rb = pl.multiple_of(_bwd_chunk(i) * CHUNK, CHUNK)
        ob_sc[pl.ds(rb, CHUNK), :] = _chunk_step(
            rotate(q_ref[pl.ds(rb, CHUNK), :].astype(f32), rb),
            rotate(k_ref[pl.ds(rb, CHUNK), :].astype(f32), rb) * kscale,
            v_ref[pl.ds(rb, CHUNK), :], cumb, cumb[0:1, :], stb_sc, strict, kmask, vmask, smask)
        return carry

    lax.fori_loop(0, N_CHUNK, body, 0, unroll=4)
    _scan_finish(of_sc, ob_sc, g_ref, y_ref)


def _retention(p, cos_r, sin_r, cumf, cumb):
    kh = RET_H * RET_DK
    p3 = p.reshape(BATCH, S, NC)
    col = lambda w, c: pl.BlockSpec((None, S, w), lambda b: (b, 0, c // w))
    const = lambda shape: pl.BlockSpec(shape, lambda b: (0, 0))
    return pl.pallas_call(
        _ret_kernel,
        grid=(BATCH,),
        in_specs=[col(kh, C_RQ), col(kh, C_RK), col(SCAN_W, C_RV), col(SCAN_W, C_RG),
                  const((S, RET_DK)), const((S, RET_DK)), const((CHUNK, kh)), const((CHUNK, kh))],
        out_specs=pl.BlockSpec((None, S, SCAN_W), lambda b: (b, 0, 0)),
        out_shape=jax.ShapeDtypeStruct((BATCH, S, SCAN_W), bf16),
        scratch_shapes=[pltpu.VMEM((S, SCAN_W), f32), pltpu.VMEM((S, SCAN_W), f32),
                        pltpu.VMEM((SCAN_W, kh), f32), pltpu.VMEM((SCAN_W, kh), f32)],
        compiler_params=_cp(("arbitrary",)),
        name="retention",
    )(p3, p3, p3, p3, cos_r, sin_r, cumf, cumb).reshape(R, SCAN_W)


def _lat_tile(g):
    return (g // LAT_TILES) * TILES + 1 + g % LAT_TILES


def _out_kernel(*refs, n_resid, has_ctx):
    resid, refs = refs[:n_resid], refs[n_resid:]
    *ya, yb_ref, yc_ref, mod_ref, w_ref, rwt_ref, xo_ref, h2_ref, aff_ref = refs
    is_ctx = (pl.program_id(0) % TILES == 0) if has_ctx else False
    na = MLA_H * MLA_V
    acc = _dot(_tile_value(ya, is_ctx), w_ref[0:na, :])
    acc += _dot(yb_ref[...], w_ref[na:na + SCAN_W, :])
    acc += _dot(yc_ref[...], w_ref[na + SCAN_W:, :])
    x = _tile_value(resid, is_ctx) + mod_ref[2:3, :] * acc
    xo_ref[...] = x
    hb = (_rms(x) * (1.0 + mod_ref[4:5, :]) + mod_ref[3:4, :]).astype(bf16)
    h2_ref[...] = hb
    lg = _nt(rwt_ref[...], hb)
    e = jnp.exp(lg - jnp.max(lg, axis=0, keepdims=True))
    aff_ref[...] = e / jnp.sum(e, axis=0, keepdims=True)


def _out_proj(resid, ya, yb, yc, mods, w_out, rwt, layer, has_ctx):
    tile = (lambda g: g) if has_ctx else _lat_tile
    n_tiles = R // TM if has_ctx else BATCH * LAT_TILES
    out_row = lambda w: pl.BlockSpec((TM, w), lambda g: (g, 0))
    return pl.pallas_call(
        functools.partial(_out_kernel, n_resid=len(resid), has_ctx=has_ctx),
        grid=(n_tiles,),
        in_specs=_tile_specs(resid, tile) + _tile_specs(ya, tile) + _tile_specs((yb,), tile) + _tile_specs((yc,), tile) + [
            pl.BlockSpec((None, None, 6, D), lambda g: (layer, _mod_row(tile(g)), 0, 0)),
            pl.BlockSpec((D, D), lambda g: (0, 0)),
            pl.BlockSpec((N_EXP, D), lambda g: (0, 0))],
        out_specs=[out_row(D), out_row(D), pl.BlockSpec((N_EXP, TM), lambda g: (0, g))],
        out_shape=[jax.ShapeDtypeStruct((n_tiles * TM, D), f32), jax.ShapeDtypeStruct((n_tiles * TM, D), bf16),
                   jax.ShapeDtypeStruct((N_EXP, n_tiles * TM), f32)],
        compiler_params=_cp(("arbitrary",)),
        name="out_proj",
    )(*resid, *ya, yb, yc, mods, w_out, rwt)


def _topk_kernel(aff_ref, pos_ref, post_ref, *, has_ctx):
    ri = lax.broadcasted_iota(jnp.int32, (TM, TM), 0)
    ci = lax.broadcasted_iota(jnp.int32, (TM, TM), 1)
    before = jnp.where(ri < ci, 1.0, 0.0).astype(bf16)

    def prefix_count(m):
        out = []
        off = jnp.zeros((N_EXP, 1), f32)
        for blk in range(m.shape[1] // TM):
            mb = m[:, blk * TM:(blk + 1) * TM]
            out.append(_dot(mb.astype(bf16), before) + off)
            off = off + jnp.sum(mb, axis=1, keepdims=True)
        return jnp.concatenate(out, axis=1) if len(out) > 1 else out[0]

    def select(a, cap, base):
        bits = pltpu.bitcast(a, jnp.int32)
        capf = float(cap)

        def step(i, thr):
            cand = thr | jnp.left_shift(jnp.int32(1), 30 - i)
            cnt = jnp.sum(jnp.where(bits >= cand, 1.0, 0.0), axis=1, keepdims=True)
            return jnp.where(cnt >= capf, cand, thr)

        thr = lax.fori_loop(0, 31, step, jnp.zeros((N_EXP, 1), jnp.int32))
        gt = jnp.where(bits > thr, 1.0, 0.0)
        eq = jnp.where(bits == thr, 1.0, 0.0)
        need = capf - jnp.sum(gt, axis=1, keepdims=True)
        keep = gt + eq * jnp.where(prefix_count(eq) < need, 1.0, 0.0)
        return jnp.where(keep > 0.5, prefix_count(keep) + float(base), -1.0)

    if has_ctx:
        pos = jnp.concatenate([select(aff_ref[:, 0:N_CTX], CAP_CTX, CAP_LAT), select(aff_ref[:, N_CTX:], CAP_LAT, 0)], axis=1)
    else:
        pos = select(aff_ref[...], CAP_LAT, 0)
    pos_ref[...] = pos
    post_ref[...] = jnp.concatenate([pos, jnp.full((128 - N_EXP, pos.shape[1]), -1.0, f32)], axis=0).T


def _topk(aff, has_ctx):
    n = S if has_ctx else N_LAT
    return pl.pallas_call(
        functools.partial(_topk_kernel, has_ctx=has_ctx),
        grid=(BATCH,),
        in_specs=[pl.BlockSpec((N_EXP, n), lambda b: (0, b))],
        out_specs=[pl.BlockSpec((N_EXP, n), lambda b: (0, b)), pl.BlockSpec((n, 128), lambda b: (b, 0))],
        out_shape=[jax.ShapeDtypeStruct((N_EXP, BATCH * n), f32), jax.ShapeDtypeStruct((BATCH * n, 128), f32)],
        compiler_params=_cp(("arbitrary",)),
        name="route_topk",
    )(aff)


def _gather_kernel(pos_ref, aff_ref, h_ref, x_ref, g_ref):
    e = pl.program_id(1)
    slots, n = x_ref.shape[0], pos_ref.shape[1]
    prow = pos_ref[pl.ds(e, 1), :]
    arow = aff_ref[pl.ds(e, 1), :]
    slot = lax.broadcasted_iota(jnp.int32, (slots, n), 0).astype(f32)
    hit = prow == slot
    onehot = jnp.where(hit, 1.0, 0.0).astype(bf16)
    x_ref[...] = _dot(onehot, h_ref[...]).astype(bf16)
    g = jnp.sum(jnp.where(hit, arow, 0.0), axis=1, keepdims=True)
    g_ref[...] = jnp.broadcast_to(g, (slots, 128))


def _gather(pos, aff, h2, has_ctx):
    slots = SLOTS if has_ctx else CAP_LAT
    n = S if has_ctx else N_LAT
    return pl.pallas_call(
        _gather_kernel,
        grid=(BATCH, N_EXP),
        in_specs=[pl.BlockSpec((N_EXP, n), lambda b, e: (0, b)),
                  pl.BlockSpec((N_EXP, n), lambda b, e: (0, b)),
                  pl.BlockSpec((n, D), lambda b, e: (b, 0))],
        out_specs=[pl.BlockSpec((None, slots, D), lambda b, e: (e, b, 0)),
                   pl.BlockSpec((None, slots, 128), lambda b, e: (e, b, 0))],
        out_shape=[jax.ShapeDtypeStruct((N_EXP, BATCH * slots, D), bf16),
                   jax.ShapeDtypeStruct((N_EXP, BATCH * slots, 128), f32)],
        compiler_params=_cp(("arbitrary", "arbitrary")),
        name="moe_gather",
    )(pos, aff, h2)


TF = 512
N_UP = EXP_FF // TF
N_DOWN = D // TF


def _moe_kernel(x_ref, wg_ref, wu_ref, wd_ref, g_ref, y_ref, hm_ref):
    s = pl.program_id(1)

    @pl.when(s < N_UP)
    def _():
        x = x_ref[...]
        a = _dot(x, wg_ref[...].astype(bf16))
        u = _dot(x, wu_ref[...].astype(bf16))
        hm_ref[s] = (_silu(a) * u).astype(bf16)

    @pl.when(s >= N_UP)
    def _():
        wd = wd_ref[...].astype(bf16)
        acc = _dot(hm_ref[0], wd[0:TF, :])
        for c in range(1, N_UP):
            acc += _dot(hm_ref[c], wd[c * TF:(c + 1) * TF, :])
        y_ref[...] = (acc * g_ref[:, 0:1]).astype(bf16)


def _moe_ffn(xg, gs, w_gate, w_up, w_down, layer):
    rows = xg.shape[1]
    up = lambda s: jnp.minimum(s, N_UP - 1)
    down = lambda s: jnp.maximum(s - N_UP, 0)
    return pl.pallas_call(
        _moe_kernel,
        grid=(N_EXP, N_UP + N_DOWN),
        in_specs=[pl.BlockSpec((None, rows, D), lambda e, s: (e, 0, 0)),
                  pl.BlockSpec((None, None, D, TF), lambda e, s: (layer, e, 0, up(s))),
                  pl.BlockSpec((None, None, D, TF), lambda e, s: (layer, e, 0, up(s))),
                  pl.BlockSpec((None, None, EXP_FF, TF), lambda e, s: (layer, e, 0, down(s))),
                  pl.BlockSpec((None, rows, 128), lambda e, s: (e, 0, 0))],
        out_specs=pl.BlockSpec((None, rows, TF), lambda e, s: (e, 0, down(s))),
        out_shape=jax.ShapeDtypeStruct((N_EXP, rows, D), bf16),
        scratch_shapes=[pltpu.VMEM((N_UP, rows, TF), bf16)],
        compiler_params=_cp(("arbitrary", "arbitrary")),
        name="moe_ffn",
    )(xg, w_gate, w_up, w_down, gs)


TN_C = 1024


def _combine_kernel(post_ref, ys_ref, x_ref, mod_ref, o_ref, *, has_ctx):
    t = pl.program_id(2) if has_ctx else pl.program_id(2) + 1
    pb = post_ref[...]

    def scatter(cap, base, ys):
        pc = pb - float(base)
        pc = jnp.where((pc >= 0.0) & (pc < float(cap)), pc, -1.0).astype(bf16)
        er = lax.broadcasted_iota(jnp.int32, (128, N_EXP * cap), 0)
        ec = lax.broadcasted_iota(jnp.int32, (128, N_EXP * cap), 1) // cap
        rep = jnp.where(er == ec, 1.0, 0.0).astype(bf16)
        slot = (lax.broadcasted_iota(jnp.int32, (TM, N_EXP * cap), 1) % cap).astype(f32)
        onehot = jnp.where(_dot(pc, rep) == slot, 1.0, 0.0).astype(bf16)
        o_ref[...] = x_ref[...] + mod_ref[5:6, :] * _dot(onehot, ys)

    if has_ctx:
        @pl.when(t == 0)
        def _():
            scatter(CAP_CTX, CAP_LAT, ys_ref[:, CAP_LAT:SLOTS, :].reshape(N_EXP * CAP_CTX, TN_C))

    @pl.when(t > 0)
    def _():
        scatter(CAP_LAT, 0, ys_ref[:, 0:CAP_LAT, :].reshape(N_EXP * CAP_LAT, TN_C))


def _combine(post, ys, xa, mods, layer, has_ctx):
    slots = SLOTS if has_ctx else CAP_LAT
    tps = TILES if has_ctx else LAT_TILES
    tile = lambda b, t: b * tps + t
    mod_row = (lambda b, t: jnp.where(t == 0, BATCH, b)) if has_ctx else (lambda b, t: b)
    return pl.pallas_call(
        functools.partial(_combine_kernel, has_ctx=has_ctx),
        grid=(BATCH, D // TN_C, tps),
        in_specs=[pl.BlockSpec((TM, 128), lambda b, n, t: (tile(b, t), 0)),
                  pl.BlockSpec((N_EXP, None, slots, TN_C), lambda b, n, t: (0, b, 0, n)),
                  pl.BlockSpec((TM, TN_C), lambda b, n, t: (tile(b, t), n)),
                  pl.BlockSpec((None, None, 6, TN_C), lambda b, n, t: (layer, mod_row(b, t), 0, n))],
        out_specs=pl.BlockSpec((TM, TN_C), lambda b, n, t: (tile(b, t), n)),
        out_shape=jax.ShapeDtypeStruct((BATCH * tps * TM, D), f32),
        compiler_params=_cp(("arbitrary", "arbitrary", "arbitrary")),
        name="moe_combine",
    )(post, ys.reshape(N_EXP, BATCH, slots, D), xa, mods)


def _final_kernel(x_ref, g_ref, o_ref):
    o_ref[...] = _rms(x_ref[...]) * g_ref[...]


def _final_norm(xa, gain, has_ctx):
    src = (lambda b, t: b * TILES + 1 + t) if has_ctx else (lambda b, t: b * LAT_TILES + t)
    return pl.pallas_call(
        _final_kernel,
        grid=(BATCH, LAT_TILES),
        in_specs=[pl.BlockSpec((TM, D), lambda b, t: (src(b, t), 0)),
                  pl.BlockSpec((1, D), lambda b, t: (0, 0))],
        out_specs=pl.BlockSpec((None, TM, D), lambda b, t: (b, t, 0)),
        out_shape=jax.ShapeDtypeStruct((BATCH, N_LAT, D), f32),
        compiler_params=_cp(("arbitrary", "arbitrary")),
        name="final_norm",
    )(xa, gain.reshape(1, D))


def _in_weights(w):
    o = np.cumsum((0, MLA_QR, MLA_KVR, MLA_ROPE, 256, 256, 512, 32, 512, 512, 512, 512, 512))
    piece = lambda i, j: w[:, o[i]:o[j]]
    cq, ckv, kr, gq, gk, gv, gz, gog = (piece(i, i + 1) for i in range(8))
    ret = piece(8, 12)
    pad = jnp.zeros((D, NC - (C_KRZ + MLA_ROPE + 2 * GLA_RANK)), w.dtype)
    return jnp.concatenate([ret, gv, gog, cq, ckv, gq, gk, kr, gz, pad], axis=1).astype(bf16)


def _mla_weights(w_uq, w_ukv):
    half = MLA_ROPE // 2
    wq = w_uq.reshape(MLA_QR, MLA_H, MLA_NOPE + MLA_ROPE)
    nope = wq[:, :, :MLA_NOPE]
    rope = wq[:, :, MLA_NOPE:].reshape(MLA_QR, MLA_H, half, 2)
    ev, od = rope[..., 0], rope[..., 1]
    zpad = jnp.zeros((MLA_QR, MLA_H, HP - MLA_NOPE - MLA_ROPE), w_uq.dtype)
    q_main = jnp.concatenate([nope, ev, od, zpad], axis=-1).reshape(MLA_QR, MLA_H * HP)
    q_part = jnp.concatenate([jnp.zeros_like(nope), od, ev, zpad], axis=-1).reshape(MLA_QR, MLA_H * HP)
    wkv = w_ukv.reshape(MLA_KVR, MLA_H, MLA_NOPE + MLA_V)
    k_main = jnp.concatenate([wkv[:, :, :MLA_NOPE], jnp.zeros((MLA_KVR, MLA_H, HP - MLA_NOPE), w_ukv.dtype)], axis=-1)
    v_main = wkv[:, :, MLA_NOPE:]
    return (jnp.concatenate([q_main, q_part], axis=1).astype(bf16),
            jnp.concatenate([k_main.reshape(MLA_KVR, MLA_H * HP), v_main.reshape(MLA_KVR, MLA_H * MLA_V)], axis=1).astype(bf16))


def _rope_key_placement():
    half = MLA_ROPE // 2
    nq = MLA_H * HP
    e2 = np.zeros((128, 2 * nq), np.float32)
    for h in range(MLA_H):
        for i in range(half):
            ev, od = h * HP + MLA_NOPE + i, h * HP + MLA_NOPE + half + i
            e2[2 * i, ev] = e2[2 * i + 1, od] = 1.0
            e2[2 * i + 1, nq + ev] = e2[2 * i, nq + od] = 1.0
    return jnp.asarray(e2, bf16)


def _tables():
    rows = N_LAT // GRID_W
    row = np.repeat(np.arange(rows, dtype=np.float32), GRID_W)
    colp = np.tile(np.arange(GRID_W, dtype=np.float32), rows)
    n_freq = MLA_ROPE // 4
    inv = jnp.power(ROPE_BASE, -jnp.arange(n_freq, dtype=f32) / n_freq)
    ang = jnp.concatenate([row[:, None] * inv, colp[:, None] * inv], axis=-1)
    cos_a, sin_a = jnp.cos(ang), jnp.sin(ang)
    one = jnp.ones((N_LAT, MLA_NOPE), f32)
    zpad = jnp.zeros((N_LAT, HP - MLA_NOPE - MLA_ROPE), f32)
    tc_lat = jnp.concatenate([one, cos_a, cos_a, zpad], axis=1)
    ts_lat = jnp.concatenate([0 * one, -sin_a, sin_a, zpad], axis=1)
    tc_ctx = jnp.concatenate([jnp.ones((N_CTX, MLA_NOPE + MLA_ROPE), f32), jnp.zeros((N_CTX, HP - MLA_NOPE - MLA_ROPE), f32)], axis=1)
    tc = jnp.concatenate([tc_ctx, tc_lat], axis=0)
    ts = jnp.concatenate([jnp.zeros((N_CTX, HP), f32), ts_lat], axis=0)

    inv_r = 1.0 / jnp.power(ROPE_BASE, jnp.linspace(0.0, 1.0, RET_DK // 2, dtype=f32))
    ang_r = jnp.arange(N_LAT, dtype=f32)[:, None] * inv_r
    cos_r = jnp.concatenate([jnp.ones((N_CTX, RET_DK), f32), jnp.repeat(jnp.cos(ang_r), 2, axis=1)], axis=0)
    sin_r = jnp.concatenate([jnp.zeros((N_CTX, RET_DK), f32),
                             jnp.stack([-jnp.sin(ang_r), jnp.sin(ang_r)], axis=-1).reshape(N_LAT, RET_DK)], axis=0)

    def log_decay(direction):
        e = RET_EXP0 + direction + 2.0 * jnp.arange(RET_H, dtype=f32)
        return jnp.repeat(jnp.log1p(-jnp.exp2(-e)), RET_DK)[None, :]

    steps = jnp.arange(1, CHUNK + 1, dtype=f32)[:, None]
    cumf = steps * log_decay(0.0)
    cumb = steps[::-1] * log_decay(1.0)
    return tc, ts, cos_r, sin_r, cumf, cumb


def kernel(x, c, ctx, c_ctx, ada_w, ada_b, w_in, mla_q_norm, mla_w_uq, mla_kv_norm, mla_w_ukv, gla_gate_w2,
           gla_gate_b, w_out, router_w, exp_w_gate, exp_w_up, exp_w_down, final_norm):
    resid = (x, ctx)
    cc = jnp.concatenate([c, c_ctx[None, :], jnp.zeros((8 - BATCH - 1, D), f32)], axis=0)
    mods = _modulation(cc, ada_w, ada_b).reshape(DEPTH, 8, 6, D)
    tc, ts, cos_r, sin_r, cumf, cumb = _tables()
    e2 = _rope_key_placement()
    kh = GLA_H * GLA_DK

    for l in range(DEPTH):
        w_in_p = _in_weights(w_in[l])
        wq2, wkv = _mla_weights(mla_w_uq[l], mla_w_ukv[l])
        w2p = jnp.zeros((128, 2 * kh), f32)
        w2p = w2p.at[64:64 + GLA_RANK, 0:kh].set(gla_gate_w2[l, 0]).at[64 + GLA_RANK:64 + 2 * GLA_RANK, kh:].set(gla_gate_w2[l, 1])
        b2 = gla_gate_b[l].reshape(1, 2 * kh)

        has_ctx = l < DEPTH - 1
        p = _in_proj(resid, mods, w_in_p, l)
        q, k, vt = _mla_prep(p, tc, ts, mla_q_norm[l].reshape(1, -1), mla_kv_norm[l].reshape(1, -1), wq2, wkv, e2)
        ya = _attention(q, k, vt, has_ctx)
        yb = _gla(p, w2p.astype(bf16), b2)
        yc = _retention(p, cos_r, sin_r, cumf, cumb)
        xa, h2, aff = _out_proj(resid, ya, yb, yc, mods, w_out[l].astype(bf16), router_w[l].T.astype(bf16), l, has_ctx)
        pos, post = _topk(aff, has_ctx)
        xg, gs = _gather(pos, aff, h2, has_ctx)
        ys = _moe_ffn(xg, gs, exp_w_gate, exp_w_up, exp_w_down, l)
        resid = (_combine(post, ys, xa, mods, l, has_ctx),)
    return _final_norm(resid[0], final_norm, has_ctx)
```

```python
import functools

import numpy as np
import jax
import jax.numpy as jnp
from jax import lax
from jax.experimental import pallas as pl
from jax.experimental.pallas import tpu as pltpu

f32 = jnp.float32
bf16 = jnp.bfloat16

D = 2048
BATCH = 4
N_LAT = 2048
N_CTX = 256
S = N_CTX + N_LAT
R = BATCH * S
DEPTH = 2
GRID_W = 64
EPS = 1e-6
LOG2E = 1.4426950408889634
ROPE_BASE = 10000.0
CHUNK = 64

MLA_H, MLA_QR, MLA_KVR, MLA_NOPE, MLA_ROPE, MLA_V = 8, 512, 256, 128, 64, 128
GLA_H, GLA_DK, GLA_DV, GLA_RANK, GLA_TAU = 4, 64, 128, 16, 16.0
RET_H, RET_DK, RET_DV, RET_EXP0 = 4, 128, 128, 5.0
N_EXP, EXP_FF, EC_CAP = 16, 2048, 2
CAP_LAT = EC_CAP * N_LAT // N_EXP
CAP_CTX = EC_CAP * N_CTX // N_EXP
SLOTS = CAP_LAT + CAP_CTX

TM = 256
TILES = S // TM
LAT_TILES = N_LAT // TM
HP = 256

C_RQ, C_RK, C_RV, C_RG = 0, 512, 1024, 1536
C_GV, C_GOG, C_CQ, C_CKV, C_GQ, C_GK, C_KRZ = 2048, 2560, 3072, 3584, 3840, 4096, 4352
NC = 4608
TN_IN = 1536

VMEM_LIMIT = 56 * 1024 * 1024


def _cp(sem):
    return pltpu.CompilerParams(dimension_semantics=sem, vmem_limit_bytes=VMEM_LIMIT)


def _nt(a, b):
    return lax.dot_general(a, b, (((1,), (1,)), ((), ())), preferred_element_type=f32)


def _tn(a, b):
    return lax.dot_general(a, b, (((0,), (0,)), ((), ())), preferred_element_type=f32)


def _dot(a, b):
    return jnp.dot(a, b, preferred_element_type=f32)


def _rms(x):
    return x * lax.rsqrt(jnp.mean(x * x, axis=-1, keepdims=True) + EPS)


def _silu(x):
    return x * (1.0 / (1.0 + jnp.exp(-x)))


def _mod_row(i):
    return jnp.where(i % TILES == 0, BATCH, i // TILES)


def _lat_first(i):
    return (i // TILES) * TILES + (i % TILES + TILES - 1) % TILES


def _mod_kernel(s_ref, w_ref, b_ref, o_ref):
    s = _silu(s_ref[...]).astype(bf16)
    o_ref[...] = _dot(s, w_ref[...].astype(bf16)) + b_ref[...]


def _modulation(cc, ada_w, ada_b):
    tn = 1024
    return pl.pallas_call(
        _mod_kernel,
        grid=(DEPTH, 6 * D // tn),
        in_specs=[pl.BlockSpec((8, D), lambda l, j: (0, 0)),
                  pl.BlockSpec((None, D, tn), lambda l, j: (l, 0, j)),
                  pl.BlockSpec((None, 1, tn), lambda l, j: (l, 0, j))],
        out_specs=pl.BlockSpec((None, 8, tn), lambda l, j: (l, 0, j)),
        out_shape=jax.ShapeDtypeStruct((DEPTH, 8, 6 * D), f32),
        compiler_params=_cp(("arbitrary", "arbitrary")),
        name="modulation",
    )(cc, ada_w, ada_b.reshape(DEPTH, 1, 6 * D))


def _tile_specs(arrs, tile_of):
    w = arrs[0].shape[-1]
    if arrs[0].ndim == 2:
        return [pl.BlockSpec((TM, w), lambda *g: (tile_of(*g), 0))]
    lat = pl.BlockSpec((None, TM, w), lambda *g: (tile_of(*g) // TILES, jnp.maximum(tile_of(*g) % TILES - 1, 0), 0))
    if len(arrs) == 1:
        return [lat]
    return [lat, pl.BlockSpec((None, TM, w), lambda *g: (tile_of(*g) // TILES, 0, 0))]


def _tile_value(refs, is_ctx, rows=slice(None)):
    if len(refs) == 1:
        return refs[0][rows, :]
    return jnp.where(is_ctx, refs[1][rows, :], refs[0][rows, :])


def _in_kernel(*refs):
    *resid, mod_ref, w_ref, o_ref = refs
    x = _tile_value(resid, pl.program_id(0) % TILES == 0)
    h = (_rms(x) * (1.0 + mod_ref[1:2, :]) + mod_ref[0:1, :]).astype(bf16)
    for j in range(NC // TN_IN):
        o_ref[:, j * TN_IN:(j + 1) * TN_IN] = _dot(h, w_ref[:, j * TN_IN:(j + 1) * TN_IN]).astype(bf16)


def _in_proj(resid, mods, w_in_p, layer):
    return pl.pallas_call(
        _in_kernel,
        grid=(R // TM,),
        in_specs=_tile_specs(resid, lambda i: i) + [
            pl.BlockSpec((None, None, 6, D), lambda i: (layer, _mod_row(i), 0, 0)),
            pl.BlockSpec((D, NC), lambda i: (0, 0), pipeline_mode=pl.Buffered(1))],
        out_specs=pl.BlockSpec((TM, NC), lambda i: (i, 0)),
        out_shape=jax.ShapeDtypeStruct((R, NC), bf16),
        compiler_params=_cp(("arbitrary",)),
        name="in_proj",
    )(*resid, mods, w_in_p)


def _mla_prep_kernel(cq_ref, ckv_ref, krz_ref, tc_ref, ts_ref, qn_ref, kvn_ref, wq_ref, wkv_ref, e2_ref,
                     q_ref, k_ref, vt_ref):
    tc = tc_ref[...]
    ts = ts_ref[...]
    scale = (MLA_NOPE + MLA_ROPE) ** -0.5 * LOG2E
    hq = (_rms(cq_ref[...].astype(f32)) * qn_ref[...]).astype(bf16)
    q2 = _dot(hq, wq_ref[...])
    hkv = (_rms(ckv_ref[...].astype(f32)) * kvn_ref[...]).astype(bf16)
    kv = _dot(hkv, wkv_ref[...])
    kr2 = _dot(krz_ref[...], e2_ref[...])
    nq = MLA_H * HP
    for h in range(MLA_H):
        sl = slice(h * HP, (h + 1) * HP)
        sl2 = slice(nq + h * HP, nq + (h + 1) * HP)
        q_ref[:, sl] = ((q2[:, sl] * tc + q2[:, sl2] * ts) * scale).astype(bf16)
        k_ref[:, sl] = (kv[:, sl] + kr2[:, sl] * tc + kr2[:, sl2] * ts).astype(bf16)
    vt_ref[...] = kv[:, nq:].T.astype(bf16)


def _mla_prep(p, tc, ts, qn, kvn, wq2, wkv, e2):
    nq = MLA_H * HP
    nv = MLA_H * MLA_V
    const = lambda shape: pl.BlockSpec(shape, lambda i: (0, 0))
    return pl.pallas_call(
        _mla_prep_kernel,
        grid=(R // TM,),
        in_specs=[pl.BlockSpec((TM, MLA_QR), lambda i: (i, C_CQ // MLA_QR)),
                  pl.BlockSpec((TM, MLA_KVR), lambda i: (i, C_CKV // MLA_KVR)),
                  pl.BlockSpec((TM, 128), lambda i: (i, C_KRZ // 128)),
                  pl.BlockSpec((TM, HP), lambda i: (i % TILES, 0)),
                  pl.BlockSpec((TM, HP), lambda i: (i % TILES, 0)),
                  const((1, MLA_QR)), const((1, MLA_KVR)),
                  const((MLA_QR, 2 * nq)), const((MLA_KVR, nq + nv)), const((128, 2 * nq))],
        out_specs=[pl.BlockSpec((TM, nq), lambda i: (_lat_first(i), 0)),
                   pl.BlockSpec((TM, nq), lambda i: (i, 0)),
                   pl.BlockSpec((None, nv, TM), lambda i: (i // TILES, 0, i % TILES))],
        out_shape=[jax.ShapeDtypeStruct((R, nq), bf16), jax.ShapeDtypeStruct((R, nq), bf16),
                   jax.ShapeDtypeStruct((BATCH, nv, S), bf16)],
        compiler_params=_cp(("arbitrary",)),
        name="mla_prep",
    )(p, p, p, tc, ts, qn, kvn, wq2, wkv, e2)


TQ = 512
ATT_HEADS_PER_STEP = 2


def _attn_body(q_ref, k_ref, vt_ref, o_ref, s_sc, n_chunks, n_heads):
    nq = q_ref.shape[0]
    qs = [q_ref[:, h * HP:(h + 1) * HP] for h in range(n_heads)]

    def scores(h, j, m):
        s = _nt(k_ref[j * TM:(j + 1) * TM, h * HP:(h + 1) * HP], qs[h])
        s_sc[h, j] = s
        cm = jnp.max(s, axis=0, keepdims=True)
        return cm if m is None else jnp.maximum(m, cm)

    m = [None] * n_heads
    for j in range(n_chunks):
        m[0] = scores(0, j, m[0])
    for h in range(n_heads):
        l = jnp.zeros((1, nq), f32)
        acc = jnp.zeros((MLA_V, nq), f32)
        for j in range(n_chunks):
            p = jnp.exp2(s_sc[h, j] - m[h])
            l = l + jnp.sum(p, axis=0, keepdims=True)
            acc = acc + _dot(vt_ref[h * MLA_V:(h + 1) * MLA_V, j * TM:(j + 1) * TM], p.astype(bf16))
            if h + 1 < n_heads:
                m[h + 1] = scores(h + 1, j, m[h + 1])
        o_ref[:, h * MLA_V:(h + 1) * MLA_V] = (acc * (1.0 / l)).T.astype(bf16)


def _attention(q, k, vt, has_ctx):
    nq = MLA_H * HP
    nv = MLA_H * MLA_V
    q3 = q.reshape(BATCH, S, nq)
    k3 = k.reshape(BATCH, S, nq)
    hs = ATT_HEADS_PER_STEP
    ya = pl.pallas_call(
        functools.partial(_attn_body, n_chunks=S // TM, n_heads=hs),
        grid=(BATCH, MLA_H // hs, N_LAT // TQ),
        in_specs=[pl.BlockSpec((None, TQ, hs * HP), lambda b, h, t: (b, t, h)),
                  pl.BlockSpec((None, S, hs * HP), lambda b, h, t: (b, 0, h)),
                  pl.BlockSpec((None, hs * MLA_V, S), lambda b, h, t: (b, h, 0))],
        out_specs=pl.BlockSpec((None, TQ, hs * MLA_V), lambda b, h, t: (b, t, h)),
        out_shape=jax.ShapeDtypeStruct((BATCH, N_LAT, nv), bf16),
        scratch_shapes=[pltpu.VMEM((hs, S // TM, TM, TQ), f32)],
        compiler_params=_cp(("arbitrary", "arbitrary", "arbitrary")),
        name="mla_attention",
    )(q3, k3, vt)
    if not has_ctx:
        return (ya,)
    ya_ctx = pl.pallas_call(
        functools.partial(_attn_body, n_chunks=1, n_heads=1),
        grid=(BATCH, MLA_H),
        in_specs=[pl.BlockSpec((None, N_CTX, HP), lambda b, h: (b, N_LAT // N_CTX, h)),
                  pl.BlockSpec((None, N_CTX, HP), lambda b, h: (b, 0, h)),
                  pl.BlockSpec((None, MLA_V, N_CTX), lambda b, h: (b, h, 0))],
        out_specs=pl.BlockSpec((None, N_CTX, MLA_V), lambda b, h: (b, 0, h)),
        out_shape=jax.ShapeDtypeStruct((BATCH, N_CTX, nv), bf16),
        scratch_shapes=[pltpu.VMEM((1, 1, TM, N_CTX), f32)],
        compiler_params=_cp(("arbitrary", "arbitrary")),
        name="mla_attention_ctx",
    )(q3, k3, vt)
    return (ya, ya_ctx)


RET_CHUNK = 128
SCAN_H = 4
SCAN_V = 128
SCAN_W = SCAN_H * SCAN_V


def _scan_consts(kh, chunk):
    dk = kh // SCAN_H
    row = lax.broadcasted_iota(jnp.int32, (chunk, SCAN_H * chunk), 0)
    col = lax.broadcasted_iota(jnp.int32, (chunk, SCAN_H * chunk), 1) % chunk
    incl = row >= col
    strict = col > row
    krow = lax.broadcasted_iota(jnp.int32, (SCAN_H * chunk, kh), 0) // chunk
    kcol = lax.broadcasted_iota(jnp.int32, (SCAN_H * chunk, kh), 1) // dk
    kmask = krow == kcol
    vrow = lax.broadcasted_iota(jnp.int32, (SCAN_H * chunk, SCAN_W), 0) // chunk
    vcol = lax.broadcasted_iota(jnp.int32, (SCAN_H * chunk, SCAN_W), 1) // SCAN_V
    vmask = vrow == vcol
    srow = lax.broadcasted_iota(jnp.int32, (SCAN_W, kh), 0) // SCAN_V
    scol = lax.broadcasted_iota(jnp.int32, (SCAN_W, kh), 1) // dk
    smask = srow == scol
    return incl, strict, kmask, vmask, smask


def _chunk_step(q, k, v, cum, cend, st_ref, amask, kmask, vmask, smask):
    qd = (q * jnp.exp(cum)).astype(bf16)
    ki = k * jnp.exp(-cum)
    kend = (k * jnp.exp(cend - cum)).astype(bf16)
    dec = jnp.exp(cend)
    kst = jnp.where(kmask, jnp.concatenate([ki] * SCAN_H, axis=0), 0.0).astype(bf16)
    att = jnp.where(amask, _nt(qd, kst), 0.0).astype(bf16)
    vbd = jnp.where(vmask, jnp.concatenate([v] * SCAN_H, axis=0), jnp.zeros((), bf16))
    st = st_ref[...]
    o = _dot(att, vbd) + _nt(qd, st.astype(bf16))
    st_ref[...] = st * dec + jnp.where(smask, _tn(v, kend), 0.0)
    return o


def _bwd_chunk(i, chunk):
    n_ctx, n_all = N_CTX // chunk, S // chunk
    return jnp.where(i < n_ctx, n_ctx - 1 - i, n_all + n_ctx - 1 - i)


def _scan_finish(of_sc, ob_sc, g_ref, y_ref):
    def fin(i, carry):
        r0 = pl.multiple_of(i * TM, TM)
        o = of_sc[pl.ds(r0, TM), :] + ob_sc[pl.ds(r0, TM), :]
        g = g_ref[pl.ds(r0, TM), :].astype(f32)
        for h in range(SCAN_H):
            sl = slice(h * SCAN_V, (h + 1) * SCAN_V)
            y_ref[pl.ds(r0, TM), sl] = (_rms(o[:, sl]) * _silu(g[:, sl])).astype(bf16)
        return carry

    lax.fori_loop(0, TILES, fin, 0)


def _gla_kernel(q_ref, k_ref, v_ref, krz_ref, og_ref, w2_ref, b2_ref, y_ref, cum_sc, of_sc, ob_sc, stf_sc, stb_sc):
    kh = GLA_H * GLA_DK
    ri = lax.broadcasted_iota(jnp.int32, (TM, TM), 0)
    ci = lax.broadcasted_iota(jnp.int32, (TM, TM), 1)
    same = (ri // CHUNK) == (ci // CHUNK)
    pre = jnp.where(same & (ci <= ri), 1.0, 0.0).astype(bf16)
    suf = jnp.where(same & (ci >= ri), 1.0, 0.0).astype(bf16)

    def exact_sum(m, x):
        hi = x.astype(bf16)
        r1 = x - hi.astype(f32)
        mid = r1.astype(bf16)
        lo = (r1 - mid.astype(f32)).astype(bf16)
        return _dot(m, hi) + _dot(m, mid) + _dot(m, lo)

    def gates(i, carry):
        r0 = pl.multiple_of(i * TM, TM)
        lg = _dot(krz_ref[pl.ds(r0, TM), :], w2_ref[...]) + b2_ref[...]
        la = (jnp.minimum(lg, 0.0) - jnp.log1p(jnp.exp(-jnp.abs(lg)))) * (1.0 / GLA_TAU)
        cum_sc[pl.ds(r0, TM), 0:kh] = exact_sum(pre, la[:, 0:kh])
        cum_sc[pl.ds(r0, TM), kh:2 * kh] = exact_sum(suf, la[:, kh:2 * kh])
        return carry

    lax.fori_loop(0, TILES, gates, 0)

    incl, strict, kmask, vmask, smask = _scan_consts(kh, CHUNK)
    stf_sc[...] = jnp.zeros_like(stf_sc)
    stb_sc[...] = jnp.zeros_like(stb_sc)
    qscale = GLA_DK ** -0.5

    def body(i, carry):
        rf = pl.multiple_of(i * CHUNK, CHUNK)
        cum = cum_sc[pl.ds(rf, CHUNK), 0:kh]
        of_sc[pl.ds(rf, CHUNK), :] = _chunk_step(
            q_ref[pl.ds(rf, CHUNK), :].astype(f32) * qscale, k_ref[pl.ds(rf, CHUNK), :].astype(f32),
            v_ref[pl.ds(rf, CHUNK), :], cum, cum[CHUNK - 1:CHUNK, :], stf_sc, incl, kmask, vmask, smask)
        rb = pl.multiple_of(_bwd_chunk(i, CHUNK) * CHUNK, CHUNK)
        rc = cum_sc[pl.ds(rb, CHUNK), kh:2 * kh]
        ob_sc[pl.ds(rb, CHUNK), :] = _chunk_step(
            q_ref[pl.ds(rb, CHUNK), :].astype(f32) * qscale, k_ref[pl.ds(rb, CHUNK), :].astype(f32),
            v_ref[pl.ds(rb, CHUNK), :], rc, rc[0:1, :], stb_sc, strict, kmask, vmask, smask)
        return carry

    lax.fori_loop(0, S // CHUNK, body, 0, unroll=4)
    _scan_finish(of_sc, ob_sc, og_ref, y_ref)


def _gla(p, w2p, b2):
    kh = GLA_H * GLA_DK
    p3 = p.reshape(BATCH, S, NC)
    col = lambda w, c: pl.BlockSpec((None, S, w), lambda b: (b, 0, c // w))
    return pl.pallas_call(
        _gla_kernel,
        grid=(BATCH,),
        in_specs=[col(kh, C_GQ), col(kh, C_GK), col(SCAN_W, C_GV), col(128, C_KRZ), col(SCAN_W, C_GOG),
                  pl.BlockSpec((128, 2 * kh), lambda b: (0, 0)), pl.BlockSpec((1, 2 * kh), lambda b: (0, 0))],
        out_specs=pl.BlockSpec((None, S, SCAN_W), lambda b: (b, 0, 0)),
        out_shape=jax.ShapeDtypeStruct((BATCH, S, SCAN_W), bf16),
        scratch_shapes=[pltpu.VMEM((S, 2 * kh), f32), pltpu.VMEM((S, SCAN_W), f32), pltpu.VMEM((S, SCAN_W), f32),
                        pltpu.VMEM((SCAN_W, kh), f32), pltpu.VMEM((SCAN_W, kh), f32)],
        compiler_params=_cp(("arbitrary",)),
        name="gla",
    )(p3, p3, p3, p3, p3, w2p, b2).reshape(R, SCAN_W)


def _ret_kernel(q_ref, k_ref, v_ref, g_ref, cos_ref, sin_ref, cumf_ref, cumb_ref, y_ref, of_sc, ob_sc, stf_sc, stb_sc):
    kh = RET_H * RET_DK
    incl, strict, kmask, vmask, smask = _scan_consts(kh, RET_CHUNK)
    stf_sc[...] = jnp.zeros_like(stf_sc)
    stb_sc[...] = jnp.zeros_like(stb_sc)
    kscale = RET_DK ** -0.5
    cumf = cumf_ref[...]
    cumb = cumb_ref[...]

    even = lax.broadcasted_iota(jnp.int32, (RET_CHUNK, RET_DK), 1) % 2 == 0

    def rotate(x, r0):
        cos = jnp.concatenate([cos_ref[pl.ds(r0, RET_CHUNK), :]] * RET_H, axis=1)
        sin = jnp.concatenate([sin_ref[pl.ds(r0, RET_CHUNK), :]] * RET_H, axis=1)
        parts = []
        for h in range(RET_H):
            xh = x[:, h * RET_DK:(h + 1) * RET_DK]
            parts.append(jnp.where(even, pltpu.roll(xh, RET_DK - 1, axis=1), pltpu.roll(xh, 1, axis=1)))
        return x * cos + jnp.concatenate(parts, axis=1) * sin

    def body(i, carry):
        rf = pl.multiple_of(i * RET_CHUNK, RET_CHUNK)
        of_sc[pl.ds(rf, RET_CHUNK), :] = _chunk_step(
            rotate(q_ref[pl.ds(rf, RET_CHUNK), :].astype(f32), rf),
            rotate(k_ref[pl.ds(rf, RET_CHUNK), :].astype(f32), rf) * kscale,
            v_ref[pl.ds(rf, RET_CHUNK), :], cumf, cumf[RET_CHUNK - 1:RET_CHUNK, :], stf_sc, incl, kmask, vmask, smask)
        rb = pl.multiple_of(_bwd_chunk(i, RET_CHUNK) * RET_CHUNK, RET_CHUNK)
        ob_sc[pl.ds(rb, RET_CHUNK), :] = _chunk_step(
            rotate(q_ref[pl.ds(rb, RET_CHUNK), :].astype(f32), rb),
            rotate(k_ref[pl.ds(rb, RET_CHUNK), :].astype(f32), rb) * kscale,
            v_ref[pl.ds(rb, RET_CHUNK), :], cumb, cumb[0:1, :], stb_sc, strict, kmask, vmask, smask)
        return carry

    lax.fori_loop(0, S // RET_CHUNK, body, 0, unroll=2)
    _scan_finish(of_sc, ob_sc, g_ref, y_ref)


def _retention(p, cos_r, sin_r, cumf, cumb):
    kh = RET_H * RET_DK
    p3 = p.reshape(BATCH, S, NC)
    col = lambda w, c: pl.BlockSpec((None, S, w), lambda b: (b, 0, c // w))
    const = lambda shape: pl.BlockSpec(shape, lambda b: (0, 0))
    return pl.pallas_call(
        _ret_kernel,
        grid=(BATCH,),
        in_specs=[col(kh, C_RQ), col(kh, C_RK), col(SCAN_W, C_RV), col(SCAN_W, C_RG),
                  const((S, RET_DK)), const((S, RET_DK)), const((RET_CHUNK, kh)), const((RET_CHUNK, kh))],
        out_specs=pl.BlockSpec((None, S, SCAN_W), lambda b: (b, 0, 0)),
        out_shape=jax.ShapeDtypeStruct((BATCH, S, SCAN_W), bf16),
        scratch_shapes=[pltpu.VMEM((S, SCAN_W), f32), pltpu.VMEM((S, SCAN_W), f32),
                        pltpu.VMEM((SCAN_W, kh), f32), pltpu.VMEM((SCAN_W, kh), f32)],
        compiler_params=_cp(("arbitrary",)),
        name="retention",
    )(p3, p3, p3, p3, cos_r, sin_r, cumf, cumb).reshape(R, SCAN_W)


def _lat_tile(g):
    return (g // LAT_TILES) * TILES + 1 + g % LAT_TILES


def _out_kernel(*refs, n_resid, has_ctx):
    resid, refs = refs[:n_resid], refs[n_resid:]
    *ya, yb_ref, yc_ref, mod_ref, w_ref, rwt_ref, xo_ref, h2_ref, aff_ref = refs
    is_ctx = (pl.program_id(0) % TILES == 0) if has_ctx else False
    na = MLA_H * MLA_V
    half = TM // 2
    for r in range(2):
        rows = slice(r * half, (r + 1) * half)
        acc = _dot(_tile_value(ya, is_ctx, rows), w_ref[0:na, :])
        acc += _dot(yb_ref[rows, :], w_ref[na:na + SCAN_W, :])
        acc += _dot(yc_ref[rows, :], w_ref[na + SCAN_W:, :])
        x = _tile_value(resid, is_ctx, rows) + mod_ref[2:3, :] * acc
        xo_ref[rows, :] = x
        hb = (_rms(x) * (1.0 + mod_ref[4:5, :]) + mod_ref[3:4, :]).astype(bf16)
        h2_ref[rows, :] = hb
        lg = _nt(rwt_ref[...], hb)
        e = jnp.exp(lg - jnp.max(lg, axis=0, keepdims=True))
        aff_ref[:, rows] = e / jnp.sum(e, axis=0, keepdims=True)


def _out_proj(resid, ya, yb, yc, mods, w_out, rwt, layer, has_ctx):
    tile = (lambda g: g) if has_ctx else _lat_tile
    n_tiles = R // TM if has_ctx else BATCH * LAT_TILES
    out_row = lambda w: pl.BlockSpec((TM, w), lambda g: (g, 0))
    return pl.pallas_call(
        functools.partial(_out_kernel, n_resid=len(resid), has_ctx=has_ctx),
        grid=(n_tiles,),
        in_specs=_tile_specs(resid, tile) + _tile_specs(ya, tile) + _tile_specs((yb,), tile) + _tile_specs((yc,), tile) + [
            pl.BlockSpec((None, None, 6, D), lambda g: (layer, _mod_row(tile(g)), 0, 0)),
            pl.BlockSpec((D, D), lambda g: (0, 0)),
            pl.BlockSpec((N_EXP, D), lambda g: (0, 0))],
        out_specs=[out_row(D), out_row(D), pl.BlockSpec((N_EXP, TM), lambda g: (0, g))],
        out_shape=[jax.ShapeDtypeStruct((n_tiles * TM, D), f32), jax.ShapeDtypeStruct((n_tiles * TM, D), bf16),
                   jax.ShapeDtypeStruct((N_EXP, n_tiles * TM), f32)],
        compiler_params=_cp(("arbitrary",)),
        name="out_proj",
    )(*resid, *ya, yb, yc, mods, w_out, rwt)


def _topk_kernel(aff_ref, pos_ref, post_ref, *, has_ctx):
    ri = lax.broadcasted_iota(jnp.int32, (TM, TM), 0)
    ci = lax.broadcasted_iota(jnp.int32, (TM, TM), 1)
    before = jnp.where(ri < ci, 1.0, 0.0).astype(bf16)

    def prefix_count(m):
        out = []
        off = jnp.zeros((N_EXP, 1), f32)
        for blk in range(m.shape[1] // TM):
            mb = m[:, blk * TM:(blk + 1) * TM]
            out.append(_dot(mb.astype(bf16), before) + off)
            off = off + jnp.sum(mb, axis=1, keepdims=True)
        return jnp.concatenate(out, axis=1) if len(out) > 1 else out[0]

    def select(a, cap, base):
        bits = pltpu.bitcast(a, jnp.int32)
        capf = float(cap)

        def step(i, thr):
            cand = thr | jnp.left_shift(jnp.int32(1), 30 - i)
            cnt = jnp.sum(jnp.where(bits >= cand, 1.0, 0.0), axis=1, keepdims=True)
            return jnp.where(cnt >= capf, cand, thr)

        thr = lax.fori_loop(0, 31, step, jnp.zeros((N_EXP, 1), jnp.int32))
        gt = jnp.where(bits > thr, 1.0, 0.0)
        eq = jnp.where(bits == thr, 1.0, 0.0)
        need = capf - jnp.sum(gt, axis=1, keepdims=True)
        keep = gt + eq * jnp.where(prefix_count(eq) < need, 1.0, 0.0)
        return jnp.where(keep > 0.5, prefix_count(keep) + float(base), -1.0)

    if has_ctx:
        pos = jnp.concatenate([select(aff_ref[:, 0:N_CTX], CAP_CTX, CAP_LAT), select(aff_ref[:, N_CTX:], CAP_LAT, 0)], axis=1)
    else:
        pos = select(aff_ref[...], CAP_LAT, 0)
    pos_ref[...] = pos
    post_ref[...] = jnp.concatenate([pos, jnp.full((128 - N_EXP, pos.shape[1]), -1.0, f32)], axis=0).T


def _topk(aff, has_ctx):
    n = S if has_ctx else N_LAT
    return pl.pallas_call(
        functools.partial(_topk_kernel, has_ctx=has_ctx),
        grid=(BATCH,),
        in_specs=[pl.BlockSpec((N_EXP, n), lambda b: (0, b))],
        out_specs=[pl.BlockSpec((N_EXP, n), lambda b: (0, b)), pl.BlockSpec((n, 128), lambda b: (b, 0))],
        out_shape=[jax.ShapeDtypeStruct((N_EXP, BATCH * n), f32), jax.ShapeDtypeStruct((BATCH * n, 128), f32)],
        compiler_params=_cp(("arbitrary",)),
        name="route_topk",
    )(aff)


def _gather_kernel(pos_ref, aff_ref, h_ref, x_ref, g_ref):
    e = pl.program_id(1)
    slots, n = x_ref.shape[0], pos_ref.shape[1]
    prow = pos_ref[pl.ds(e, 1), :]
    arow = aff_ref[pl.ds(e, 1), :]
    slot = lax.broadcasted_iota(jnp.int32, (slots, n), 0).astype(f32)
    hit = prow == slot
    onehot = jnp.where(hit, 1.0, 0.0).astype(bf16)
    x_ref[...] = _dot(onehot, h_ref[...]).astype(bf16)
    g = jnp.sum(jnp.where(hit, arow, 0.0), axis=1, keepdims=True)
    g_ref[...] = jnp.broadcast_to(g, (slots, 128))


def _gather(pos, aff, h2, has_ctx):
    slots = SLOTS if has_ctx else CAP_LAT
    n = S if has_ctx else N_LAT
    return pl.pallas_call(
        _gather_kernel,
        grid=(BATCH, N_EXP),
        in_specs=[pl.BlockSpec((N_EXP, n), lambda b, e: (0, b)),
                  pl.BlockSpec((N_EXP, n), lambda b, e: (0, b)),
                  pl.BlockSpec((n, D), lambda b, e: (b, 0))],
        out_specs=[pl.BlockSpec((None, slots, D), lambda b, e: (e, b, 0)),
                   pl.BlockSpec((None, slots, 128), lambda b, e: (e, b, 0))],
        out_shape=[jax.ShapeDtypeStruct((N_EXP, BATCH * slots, D), bf16),
                   jax.ShapeDtypeStruct((N_EXP, BATCH * slots, 128), f32)],
        compiler_params=_cp(("arbitrary", "arbitrary")),
        name="moe_gather",
    )(pos, aff, h2)


TF = 512
N_UP = EXP_FF // TF
N_DOWN = D // TF


def _moe_kernel(x_ref, wg_ref, wu_ref, wd_ref, g_ref, y_ref, hm_ref):
    s = pl.program_id(1)

    @pl.when(s < N_UP)
    def _():
        x = x_ref[...]
        a = _dot(x, wg_ref[...].astype(bf16))
        u = _dot(x, wu_ref[...].astype(bf16))
        hm_ref[s] = (_silu(a) * u).astype(bf16)

    @pl.when(s >= N_UP)
    def _():
        wd = wd_ref[...].astype(bf16)
        acc = _dot(hm_ref[0], wd[0:TF, :])
        for c in range(1, N_UP):
            acc += _dot(hm_ref[c], wd[c * TF:(c + 1) * TF, :])
        y_ref[...] = (acc * g_ref[:, 0:1]).astype(bf16)


def _moe_ffn(xg, gs, w_gate, w_up, w_down, layer):
    rows = xg.shape[1]
    up = lambda s: jnp.minimum(s, N_UP - 1)
    down = lambda s: jnp.maximum(s - N_UP, 0)
    return pl.pallas_call(
        _moe_kernel,
        grid=(N_EXP, N_UP + N_DOWN),
        in_specs=[pl.BlockSpec((None, rows, D), lambda e, s: (e, 0, 0)),
                  pl.BlockSpec((None, None, D, TF), lambda e, s: (layer, e, 0, up(s))),
                  pl.BlockSpec((None, None, D, TF), lambda e, s: (layer, e, 0, up(s))),
                  pl.BlockSpec((None, None, EXP_FF, TF), lambda e, s: (layer, e, 0, down(s))),
                  pl.BlockSpec((None, rows, 128), lambda e, s: (e, 0, 0))],
        out_specs=pl.BlockSpec((None, rows, TF), lambda e, s: (e, 0, down(s))),
        out_shape=jax.ShapeDtypeStruct((N_EXP, rows, D), bf16),
        scratch_shapes=[pltpu.VMEM((N_UP, rows, TF), bf16)],
        compiler_params=_cp(("arbitrary", "arbitrary")),
        name="moe_ffn",
    )(xg, w_gate, w_up, w_down, gs)


TN_C = 1024


def _combine_kernel(post_ref, ys_ref, x_ref, mod_ref, *rest, has_ctx):
    gain_ref, o_ref = rest if len(rest) == 2 else (None, rest[0])
    tn = o_ref.shape[-1]
    t = pl.program_id(2) if has_ctx else pl.program_id(2) + 1
    pb = post_ref[...]

    def scatter(cap, base, ys):
        if cap % 128 == 0:
            slot = lax.broadcasted_iota(jnp.int32, (TM, cap), 1).astype(f32) + float(base)
            onehot = jnp.concatenate(
                [jnp.where(pb[:, e:e + 1] == slot, 1.0, 0.0).astype(bf16) for e in range(N_EXP)], axis=1)
        else:
            pc = pb - float(base)
            pc = jnp.where((pc >= 0.0) & (pc < float(cap)), pc, -1.0).astype(bf16)
            er = lax.broadcasted_iota(jnp.int32, (128, N_EXP * cap), 0)
            ec = lax.broadcasted_iota(jnp.int32, (128, N_EXP * cap), 1) // cap
            rep = jnp.where(er == ec, 1.0, 0.0).astype(bf16)
            slot = (lax.broadcasted_iota(jnp.int32, (TM, N_EXP * cap), 1) % cap).astype(f32)
            onehot = jnp.where(_dot(pc, rep) == slot, 1.0, 0.0).astype(bf16)
        x = x_ref[...] + mod_ref[5:6, :] * _dot(onehot, ys)
        o_ref[...] = x if gain_ref is None else _rms(x) * gain_ref[...]

    if has_ctx:
        @pl.when(t == 0)
        def _():
            scatter(CAP_CTX, CAP_LAT, ys_ref[:, CAP_LAT:SLOTS, :].reshape(N_EXP * CAP_CTX, tn))

    @pl.when(t > 0)
    def _():
        scatter(CAP_LAT, 0, ys_ref[:, 0:CAP_LAT, :].reshape(N_EXP * CAP_LAT, tn))


def _combine(post, ys, xa, mods, layer, has_ctx, final_gain=None):
    final = final_gain is not None
    assert not (final and has_ctx)
    slots = SLOTS if has_ctx else CAP_LAT
    tps = TILES if has_ctx else LAT_TILES
    tn = D if final else TN_C
    tile = lambda b, t: b * tps + t
    mod_row = (lambda b, t: jnp.where(t == 0, BATCH, b)) if has_ctx else (lambda b, t: b)
    extra_in, extra_specs = ((final_gain.reshape(1, D),), [pl.BlockSpec((1, D), lambda b, n, t: (0, 0))]) if final else ((), [])
    out = pl.pallas_call(
        functools.partial(_combine_kernel, has_ctx=has_ctx),
        grid=(BATCH, D // tn, tps),
        in_specs=[pl.BlockSpec((TM, 128), lambda b, n, t: (tile(b, t), 0)),
                  pl.BlockSpec((N_EXP, None, slots, tn), lambda b, n, t: (0, b, 0, n)),
                  pl.BlockSpec((TM, tn), lambda b, n, t: (tile(b, t), n)),
                  pl.BlockSpec((None, None, 6, tn), lambda b, n, t: (layer, mod_row(b, t), 0, n))] + extra_specs,
        out_specs=pl.BlockSpec((TM, tn), lambda b, n, t: (tile(b, t), n)),
        out_shape=jax.ShapeDtypeStruct((BATCH * tps * TM, D), f32),
        compiler_params=_cp(("arbitrary", "arbitrary", "arbitrary")),
        name="moe_combine",
    )(post, ys.reshape(N_EXP, BATCH, slots, D), xa, mods, *extra_in)
    return out.reshape(BATCH, N_LAT, D) if final else out


def _in_weights(w):
    o = np.cumsum((0, MLA_QR, MLA_KVR, MLA_ROPE, 256, 256, 512, 32, 512, 512, 512, 512, 512))
    piece = lambda i, j: w[:, o[i]:o[j]]
    cq, ckv, kr, gq, gk, gv, gz, gog = (piece(i, i + 1) for i in range(8))
    ret = piece(8, 12)
    pad = jnp.zeros((D, NC - (C_KRZ + MLA_ROPE + 2 * GLA_RANK)), w.dtype)
    return jnp.concatenate([ret, gv, gog, cq, ckv, gq, gk, kr, gz, pad], axis=1).astype(bf16)


def _mla_weights(w_uq, w_ukv):
    half = MLA_ROPE // 2
    wq = w_uq.reshape(MLA_QR, MLA_H, MLA_NOPE + MLA_ROPE)
    nope = wq[:, :, :MLA_NOPE]
    rope = wq[:, :, MLA_NOPE:].reshape(MLA_QR, MLA_H, half, 2)
    ev, od = rope[..., 0], rope[..., 1]
    zpad = jnp.zeros((MLA_QR, MLA_H, HP - MLA_NOPE - MLA_ROPE), w_uq.dtype)
    q_main = jnp.concatenate([nope, ev, od, zpad], axis=-1).reshape(MLA_QR, MLA_H * HP)
    q_part = jnp.concatenate([jnp.zeros_like(nope), od, ev, zpad], axis=-1).reshape(MLA_QR, MLA_H * HP)
    wkv = w_ukv.reshape(MLA_KVR, MLA_H, MLA_NOPE + MLA_V)
    k_main = jnp.concatenate([wkv[:, :, :MLA_NOPE], jnp.zeros((MLA_KVR, MLA_H, HP - MLA_NOPE), w_ukv.dtype)], axis=-1)
    v_main = wkv[:, :, MLA_NOPE:]
    return (jnp.concatenate([q_main, q_part], axis=1).astype(bf16),
            jnp.concatenate([k_main.reshape(MLA_KVR, MLA_H * HP), v_main.reshape(MLA_KVR, MLA_H * MLA_V)], axis=1).astype(bf16))


def _rope_key_placement():
    half = MLA_ROPE // 2
    nq = MLA_H * HP
    e2 = np.zeros((128, 2 * nq), np.float32)
    for h in range(MLA_H):
        for i in range(half):
            ev, od = h * HP + MLA_NOPE + i, h * HP + MLA_NOPE + half + i
            e2[2 * i, ev] = e2[2 * i + 1, od] = 1.0
            e2[2 * i + 1, nq + ev] = e2[2 * i, nq + od] = 1.0
    return jnp.asarray(e2, bf16)


def _tables():
    rows = N_LAT // GRID_W
    row = np.repeat(np.arange(rows, dtype=np.float32), GRID_W)
    colp = np.tile(np.arange(GRID_W, dtype=np.float32), rows)
    n_freq = MLA_ROPE // 4
    inv = jnp.power(ROPE_BASE, -jnp.arange(n_freq, dtype=f32) / n_freq)
    ang = jnp.concatenate([row[:, None] * inv, colp[:, None] * inv], axis=-1)
    cos_a, sin_a = jnp.cos(ang), jnp.sin(ang)
    one = jnp.ones((N_LAT, MLA_NOPE), f32)
    zpad = jnp.zeros((N_LAT, HP - MLA_NOPE - MLA_ROPE), f32)
    tc_lat = jnp.concatenate([one, cos_a, cos_a, zpad], axis=1)
    ts_lat = jnp.concatenate([0 * one, -sin_a, sin_a, zpad], axis=1)
    tc_ctx = jnp.concatenate([jnp.ones((N_CTX, MLA_NOPE + MLA_ROPE), f32), jnp.zeros((N_CTX, HP - MLA_NOPE - MLA_ROPE), f32)], axis=1)
    tc = jnp.concatenate([tc_ctx, tc_lat], axis=0)
    ts = jnp.concatenate([jnp.zeros((N_CTX, HP), f32), ts_lat], axis=0)

    inv_r = 1.0 / jnp.power(ROPE_BASE, jnp.linspace(0.0, 1.0, RET_DK // 2, dtype=f32))
    ang_r = jnp.arange(N_LAT, dtype=f32)[:, None] * inv_r
    cos_r = jnp.concatenate([jnp.ones((N_CTX, RET_DK), f32), jnp.repeat(jnp.cos(ang_r), 2, axis=1)], axis=0)
    sin_r = jnp.concatenate([jnp.zeros((N_CTX, RET_DK), f32),
                             jnp.stack([-jnp.sin(ang_r), jnp.sin(ang_r)], axis=-1).reshape(N_LAT, RET_DK)], axis=0)

    def log_decay(direction):
        e = RET_EXP0 + direction + 2.0 * jnp.arange(RET_H, dtype=f32)
        return jnp.repeat(jnp.log1p(-jnp.exp2(-e)), RET_DK)[None, :]

    steps = jnp.arange(1, RET_CHUNK + 1, dtype=f32)[:, None]
    cumf = steps * log_decay(0.0)
    cumb = steps[::-1] * log_decay(1.0)
    return tc, ts, cos_r, sin_r, cumf, cumb


def kernel(x, c, ctx, c_ctx, ada_w, ada_b, w_in, mla_q_norm, mla_w_uq, mla_kv_norm, mla_w_ukv, gla_gate_w2,
           gla_gate_b, w_out, router_w, exp_w_gate, exp_w_up, exp_w_down, final_norm):
    resid = (x, ctx)
    cc = jnp.concatenate([c, c_ctx[None, :], jnp.zeros((8 - BATCH - 1, D), f32)], axis=0)
    mods = _modulation(cc, ada_w, ada_b).reshape(DEPTH, 8, 6, D)
    tc, ts, cos_r, sin_r, cumf, cumb = _tables()
    e2 = _rope_key_placement()
    kh = GLA_H * GLA_DK

    for l in range(DEPTH):
        w_in_p = _in_weights(w_in[l])
        wq2, wkv = _mla_weights(mla_w_uq[l], mla_w_ukv[l])
        w2p = jnp.zeros((128, 2 * kh), f32)
        w2p = w2p.at[64:64 + GLA_RANK, 0:kh].set(gla_gate_w2[l, 0]).at[64 + GLA_RANK:64 + 2 * GLA_RANK, kh:].set(gla_gate_w2[l, 1])
        b2 = gla_gate_b[l].reshape(1, 2 * kh)

        has_ctx = l < DEPTH - 1
        p = _in_proj(resid, mods, w_in_p, l)
        q, k, vt = _mla_prep(p, tc, ts, mla_q_norm[l].reshape(1, -1), mla_kv_norm[l].reshape(1, -1), wq2, wkv, e2)
        ya = _attention(q, k, vt, has_ctx)
        yb = _gla(p, w2p.astype(bf16), b2)
        yc = _retention(p, cos_r, sin_r, cumf, cumb)
        xa, h2, aff = _out_proj(resid, ya, yb, yc, mods, w_out[l].astype(bf16), router_w[l].T.astype(bf16), l, has_ctx)
        pos, post = _topk(aff, has_ctx)
        xg, gs = _gather(pos, aff, h2, has_ctx)
        ys = _moe_ffn(xg, gs, exp_w_gate, exp_w_up, exp_w_down, l)
        resid = (_combine(post, ys, xa, mods, l, has_ctx, final_gain=None if has_ctx else final_norm),)
    return resid[0]
```

```python
import functools

import numpy as np
import jax
import jax.numpy as jnp
from jax import lax
from jax.experimental import pallas as pl
from jax.experimental.pallas import tpu as pltpu

f32 = jnp.float32
bf16 = jnp.bfloat16

D = 2048
BATCH = 4
N_LAT = 2048
N_CTX = 256
S = N_CTX + N_LAT
R = BATCH * S
DEPTH = 2
GRID_W = 64
EPS = 1e-6
LOG2E = 1.4426950408889634
ROPE_BASE = 10000.0
CHUNK = 64

MLA_H, MLA_QR, MLA_KVR, MLA_NOPE, MLA_ROPE, MLA_V = 8, 512, 256, 128, 64, 128
GLA_H, GLA_DK, GLA_DV, GLA_RANK, GLA_TAU = 4, 64, 128, 16, 16.0
RET_H, RET_DK, RET_DV, RET_EXP0 = 4, 128, 128, 5.0
N_EXP, EXP_FF, EC_CAP = 16, 2048, 2
CAP_LAT = EC_CAP * N_LAT // N_EXP
CAP_CTX = EC_CAP * N_CTX // N_EXP
SLOTS = CAP_LAT + CAP_CTX

TM = 256
TILES = S // TM
LAT_TILES = N_LAT // TM
HP = 256

C_RQ, C_RK, C_RV, C_RG = 0, 512, 1024, 1536
C_GV, C_GOG, C_CQ, C_CKV, C_GQ, C_GK, C_KRZ = 2048, 2560, 3072, 3584, 3840, 4096, 4352
NC = 4608
TN_IN = 1536

VMEM_LIMIT = 56 * 1024 * 1024


def _cp(sem):
    return pltpu.CompilerParams(dimension_semantics=sem, vmem_limit_bytes=VMEM_LIMIT)


def _nt(a, b):
    return lax.dot_general(a, b, (((1,), (1,)), ((), ())), preferred_element_type=f32)


def _tn(a, b):
    return lax.dot_general(a, b, (((0,), (0,)), ((), ())), preferred_element_type=f32)


def _dot(a, b):
    return jnp.dot(a, b, preferred_element_type=f32)


def _rms(x):
    return x * lax.rsqrt(jnp.mean(x * x, axis=-1, keepdims=True) + EPS)


def _silu(x):
    return x * (1.0 / (1.0 + jnp.exp(-x)))


def _mod_row(i):
    return jnp.where(i % TILES == 0, BATCH, i // TILES)


def _lat_first(i):
    return (i // TILES) * TILES + (i % TILES + TILES - 1) % TILES


def _mod_kernel(s_ref, w_ref, b_ref, o_ref):
    s = _silu(s_ref[...]).astype(bf16)
    o_ref[...] = _dot(s, w_ref[...].astype(bf16)) + b_ref[...]


def _modulation(cc, ada_w, ada_b):
    tn = 1024
    return pl.pallas_call(
        _mod_kernel,
        grid=(DEPTH, 6 * D // tn),
        in_specs=[pl.BlockSpec((8, D), lambda l, j: (0, 0)),
                  pl.BlockSpec((None, D, tn), lambda l, j: (l, 0, j)),
                  pl.BlockSpec((None, 1, tn), lambda l, j: (l, 0, j))],
        out_specs=pl.BlockSpec((None, 8, tn), lambda l, j: (l, 0, j)),
        out_shape=jax.ShapeDtypeStruct((DEPTH, 8, 6 * D), f32),
        compiler_params=_cp(("arbitrary", "arbitrary")),
        name="modulation",
    )(cc, ada_w, ada_b.reshape(DEPTH, 1, 6 * D))


def _tile_specs(arrs, tile_of):
    w = arrs[0].shape[-1]
    if arrs[0].ndim == 2:
        return [pl.BlockSpec((TM, w), lambda *g: (tile_of(*g), 0))]
    lat = pl.BlockSpec((None, TM, w), lambda *g: (tile_of(*g) // TILES, jnp.maximum(tile_of(*g) % TILES - 1, 0), 0))
    if len(arrs) == 1:
        return [lat]
    return [lat, pl.BlockSpec((None, TM, w), lambda *g: (tile_of(*g) // TILES, 0, 0))]


def _tile_value(refs, is_ctx, rows=slice(None)):
    if len(refs) == 1:
        return refs[0][rows, :]
    return jnp.where(is_ctx, refs[1][rows, :], refs[0][rows, :])


def _in_kernel(*refs):
    *resid, mod_ref, w_ref, o_ref = refs
    x = _tile_value(resid, pl.program_id(0) % TILES == 0)
    h = (_rms(x) * (1.0 + mod_ref[1:2, :]) + mod_ref[0:1, :]).astype(bf16)
    for j in range(NC // TN_IN):
        o_ref[:, j * TN_IN:(j + 1) * TN_IN] = _dot(h, w_ref[:, j * TN_IN:(j + 1) * TN_IN]).astype(bf16)


def _in_proj(resid, mods, w_in_p, layer):
    return pl.pallas_call(
        _in_kernel,
        grid=(R // TM,),
        in_specs=_tile_specs(resid, lambda i: i) + [
            pl.BlockSpec((None, None, 6, D), lambda i: (layer, _mod_row(i), 0, 0)),
            pl.BlockSpec((D, NC), lambda i: (0, 0), pipeline_mode=pl.Buffered(1))],
        out_specs=pl.BlockSpec((TM, NC), lambda i: (i, 0)),
        out_shape=jax.ShapeDtypeStruct((R, NC), bf16),
        compiler_params=_cp(("arbitrary",)),
        name="in_proj",
    )(*resid, mods, w_in_p)


def _mla_prep_kernel(cq_ref, ckv_ref, krz_ref, tc_ref, ts_ref, qn_ref, kvn_ref, wq_ref, wkv_ref, e2_ref,
                     q_ref, k_ref, vt_ref):
    tc = tc_ref[...]
    ts = ts_ref[...]
    scale = (MLA_NOPE + MLA_ROPE) ** -0.5 * LOG2E
    hq = (_rms(cq_ref[...].astype(f32)) * qn_ref[...]).astype(bf16)
    q2 = _dot(hq, wq_ref[...])
    hkv = (_rms(ckv_ref[...].astype(f32)) * kvn_ref[...]).astype(bf16)
    kv = _dot(hkv, wkv_ref[...])
    kr2 = _dot(krz_ref[...], e2_ref[...])
    nq = MLA_H * HP
    for h in range(MLA_H):
        sl = slice(h * HP, (h + 1) * HP)
        sl2 = slice(nq + h * HP, nq + (h + 1) * HP)
        q_ref[:, sl] = ((q2[:, sl] * tc + q2[:, sl2] * ts) * scale).astype(bf16)
        k_ref[:, sl] = (kv[:, sl] + kr2[:, sl] * tc + kr2[:, sl2] * ts).astype(bf16)
    vt_ref[...] = kv[:, nq:].T.astype(bf16)


def _mla_prep(p, tc, ts, qn, kvn, wq2, wkv, e2):
    nq = MLA_H * HP
    nv = MLA_H * MLA_V
    const = lambda shape: pl.BlockSpec(shape, lambda i: (0, 0))
    return pl.pallas_call(
        _mla_prep_kernel,
        grid=(R // TM,),
        in_specs=[pl.BlockSpec((TM, MLA_QR), lambda i: (i, C_CQ // MLA_QR)),
                  pl.BlockSpec((TM, MLA_KVR), lambda i: (i, C_CKV // MLA_KVR)),
                  pl.BlockSpec((TM, 128), lambda i: (i, C_KRZ // 128)),
                  pl.BlockSpec((TM, HP), lambda i: (i % TILES, 0)),
                  pl.BlockSpec((TM, HP), lambda i: (i % TILES, 0)),
                  const((1, MLA_QR)), const((1, MLA_KVR)),
                  const((MLA_QR, 2 * nq)), const((MLA_KVR, nq + nv)), const((128, 2 * nq))],
        out_specs=[pl.BlockSpec((TM, nq), lambda i: (_lat_first(i), 0)),
                   pl.BlockSpec((TM, nq), lambda i: (i, 0)),
                   pl.BlockSpec((None, nv, TM), lambda i: (i // TILES, 0, i % TILES))],
        out_shape=[jax.ShapeDtypeStruct((R, nq), bf16), jax.ShapeDtypeStruct((R, nq), bf16),
                   jax.ShapeDtypeStruct((BATCH, nv, S), bf16)],
        compiler_params=_cp(("arbitrary",)),
        name="mla_prep",
    )(p, p, p, tc, ts, qn, kvn, wq2, wkv, e2)


TQ = 512
ATT_HEADS_PER_STEP = 1


def _attn_body(q_ref, k_ref, vt_ref, o_ref, s_sc, n_chunks, n_heads):
    nq = q_ref.shape[0]
    qs = [q_ref[:, h * HP:(h + 1) * HP] for h in range(n_heads)]

    def scores(h, j, m):
        s = _nt(k_ref[j * TM:(j + 1) * TM, h * HP:(h + 1) * HP], qs[h])
        s_sc[h, j] = s
        cm = jnp.max(s, axis=0, keepdims=True)
        return cm if m is None else jnp.maximum(m, cm)

    m = [None] * n_heads
    for j in range(n_chunks):
        m[0] = scores(0, j, m[0])
    for h in range(n_heads):
        l = jnp.zeros((1, nq), f32)
        acc = jnp.zeros((MLA_V, nq), f32)
        for j in range(n_chunks):
            p = jnp.exp2(s_sc[h, j] - m[h])
            l = l + jnp.sum(p, axis=0, keepdims=True)
            acc = acc + _dot(vt_ref[h * MLA_V:(h + 1) * MLA_V, j * TM:(j + 1) * TM], p.astype(bf16))
            if h + 1 < n_heads:
                m[h + 1] = scores(h + 1, j, m[h + 1])
        o_ref[:, h * MLA_V:(h + 1) * MLA_V] = (acc * (1.0 / l)).T.astype(bf16)


def _attention(q, k, vt, has_ctx):
    nq = MLA_H * HP
    nv = MLA_H * MLA_V
    q3 = q.reshape(BATCH, S, nq)
    k3 = k.reshape(BATCH, S, nq)
    hs = ATT_HEADS_PER_STEP
    ya = pl.pallas_call(
        functools.partial(_attn_body, n_chunks=S // TM, n_heads=hs),
        grid=(BATCH, MLA_H // hs, N_LAT // TQ),
        in_specs=[pl.BlockSpec((None, TQ, hs * HP), lambda b, h, t: (b, t, h)),
                  pl.BlockSpec((None, S, hs * HP), lambda b, h, t: (b, 0, h)),
                  pl.BlockSpec((None, hs * MLA_V, S), lambda b, h, t: (b, h, 0))],
        out_specs=pl.BlockSpec((None, TQ, hs * MLA_V), lambda b, h, t: (b, t, h)),
        out_shape=jax.ShapeDtypeStruct((BATCH, N_LAT, nv), bf16),
        scratch_shapes=[pltpu.VMEM((hs, S // TM, TM, TQ), f32)],
        compiler_params=_cp(("arbitrary", "arbitrary", "arbitrary")),
        name="mla_attention",
    )(q3, k3, vt)
    if not has_ctx:
        return (ya,)
    ya_ctx = pl.pallas_call(
        functools.partial(_attn_body, n_chunks=1, n_heads=1),
        grid=(BATCH, MLA_H),
        in_specs=[pl.BlockSpec((None, N_CTX, HP), lambda b, h: (b, N_LAT // N_CTX, h)),
                  pl.BlockSpec((None, N_CTX, HP), lambda b, h: (b, 0, h)),
                  pl.BlockSpec((None, MLA_V, N_CTX), lambda b, h: (b, h, 0))],
        out_specs=pl.BlockSpec((None, N_CTX, MLA_V), lambda b, h: (b, 0, h)),
        out_shape=jax.ShapeDtypeStruct((BATCH, N_CTX, nv), bf16),
        scratch_shapes=[pltpu.VMEM((1, 1, TM, N_CTX), f32)],
        compiler_params=_cp(("arbitrary", "arbitrary")),
        name="mla_attention_ctx",
    )(q3, k3, vt)
    return (ya, ya_ctx)


RET_CHUNK = 128
SCAN_H = 4
SCAN_V = 128
SCAN_W = SCAN_H * SCAN_V


def _scan_consts(kh, chunk):
    dk = kh // SCAN_H
    row = lax.broadcasted_iota(jnp.int32, (chunk, SCAN_H * chunk), 0)
    col = lax.broadcasted_iota(jnp.int32, (chunk, SCAN_H * chunk), 1) % chunk
    incl = row >= col
    strict = col > row
    krow = lax.broadcasted_iota(jnp.int32, (SCAN_H * chunk, kh), 0) // chunk
    kcol = lax.broadcasted_iota(jnp.int32, (SCAN_H * chunk, kh), 1) // dk
    kmask = krow == kcol
    vrow = lax.broadcasted_iota(jnp.int32, (SCAN_H * chunk, SCAN_W), 0) // chunk
    vcol = lax.broadcasted_iota(jnp.int32, (SCAN_H * chunk, SCAN_W), 1) // SCAN_V
    vmask = vrow == vcol
    srow = lax.broadcasted_iota(jnp.int32, (SCAN_W, kh), 0) // SCAN_V
    scol = lax.broadcasted_iota(jnp.int32, (SCAN_W, kh), 1) // dk
    smask = srow == scol
    return incl, strict, kmask, vmask, smask


def _chunk_step(q, k, v, cum, cend, st_ref, amask, kmask, vmask, smask):
    qd = (q * jnp.exp(cum)).astype(bf16)
    ki = k * jnp.exp(-cum)
    kend = (k * jnp.exp(cend - cum)).astype(bf16)
    dec = jnp.exp(cend)
    kst = jnp.where(kmask, jnp.concatenate([ki] * SCAN_H, axis=0), 0.0).astype(bf16)
    att = jnp.where(amask, _nt(qd, kst), 0.0).astype(bf16)
    vbd = jnp.where(vmask, jnp.concatenate([v] * SCAN_H, axis=0), jnp.zeros((), bf16))
    st = st_ref[...]
    o = _dot(att, vbd) + _nt(qd, st.astype(bf16))
    st_ref[...] = st * dec + jnp.where(smask, _tn(v, kend), 0.0)
    return o


def _bwd_chunk(i, chunk):
    n_ctx, n_all = N_CTX // chunk, S // chunk
    return jnp.where(i < n_ctx, n_ctx - 1 - i, n_all + n_ctx - 1 - i)


def _scan_finish(of_sc, ob_sc, g_ref, y_ref):
    def fin(i, carry):
        r0 = pl.multiple_of(i * TM, TM)
        o = of_sc[pl.ds(r0, TM), :] + ob_sc[pl.ds(r0, TM), :]
        g = g_ref[pl.ds(r0, TM), :].astype(f32)
        for h in range(SCAN_H):
            sl = slice(h * SCAN_V, (h + 1) * SCAN_V)
            y_ref[pl.ds(r0, TM), sl] = (_rms(o[:, sl]) * _silu(g[:, sl])).astype(bf16)
        return carry

    lax.fori_loop(0, TILES, fin, 0)


def _gla_kernel(q_ref, k_ref, v_ref, krz_ref, og_ref, w2_ref, b2_ref, y_ref, cum_sc, of_sc, ob_sc, stf_sc, stb_sc):
    kh = GLA_H * GLA_DK
    ri = lax.broadcasted_iota(jnp.int32, (TM, TM), 0)
    ci = lax.broadcasted_iota(jnp.int32, (TM, TM), 1)
    same = (ri // CHUNK) == (ci // CHUNK)
    pre = jnp.where(same & (ci <= ri), 1.0, 0.0).astype(bf16)
    suf = jnp.where(same & (ci >= ri), 1.0, 0.0).astype(bf16)

    def exact_sum(m, x):
        hi = x.astype(bf16)
        r1 = x - hi.astype(f32)
        mid = r1.astype(bf16)
        lo = (r1 - mid.astype(f32)).astype(bf16)
        return _dot(m, hi) + _dot(m, mid) + _dot(m, lo)

    def gates(i, carry):
        r0 = pl.multiple_of(i * TM, TM)
        lg = _dot(krz_ref[pl.ds(r0, TM), :], w2_ref[...]) + b2_ref[...]
        la = (jnp.minimum(lg, 0.0) - jnp.log1p(jnp.exp(-jnp.abs(lg)))) * (1.0 / GLA_TAU)
        cum_sc[pl.ds(r0, TM), 0:kh] = exact_sum(pre, la[:, 0:kh])
        cum_sc[pl.ds(r0, TM), kh:2 * kh] = exact_sum(suf, la[:, kh:2 * kh])
        return carry

    lax.fori_loop(0, TILES, gates, 0)

    incl, strict, kmask, vmask, smask = _scan_consts(kh, CHUNK)
    stf_sc[...] = jnp.zeros_like(stf_sc)
    stb_sc[...] = jnp.zeros_like(stb_sc)
    qscale = GLA_DK ** -0.5

    def body(i, carry):
        rf = pl.multiple_of(i * CHUNK, CHUNK)
        cum = cum_sc[pl.ds(rf, CHUNK), 0:kh]
        of_sc[pl.ds(rf, CHUNK), :] = _chunk_step(
            q_ref[pl.ds(rf, CHUNK), :].astype(f32) * qscale, k_ref[pl.ds(rf, CHUNK), :].astype(f32),
            v_ref[pl.ds(rf, CHUNK), :], cum, cum[CHUNK - 1:CHUNK, :], stf_sc, incl, kmask, vmask, smask)
        rb = pl.multiple_of(_bwd_chunk(i, CHUNK) * CHUNK, CHUNK)
        rc = cum_sc[pl.ds(rb, CHUNK), kh:2 * kh]
        ob_sc[pl.ds(rb, CHUNK), :] = _chunk_step(
            q_ref[pl.ds(rb, CHUNK), :].astype(f32) * qscale, k_ref[pl.ds(rb, CHUNK), :].astype(f32),
            v_ref[pl.ds(rb, CHUNK), :], rc, rc[0:1, :], stb_sc, strict, kmask, vmask, smask)
        return carry

    lax.fori_loop(0, S // CHUNK, body, 0, unroll=4)
    _scan_finish(of_sc, ob_sc, og_ref, y_ref)


def _gla(p, w2p, b2):
    kh = GLA_H * GLA_DK
    p3 = p.reshape(BATCH, S, NC)
    col = lambda w, c: pl.BlockSpec((None, S, w), lambda b: (b, 0, c // w))
    return pl.pallas_call(
        _gla_kernel,
        grid=(BATCH,),
        in_specs=[col(kh, C_GQ), col(kh, C_GK), col(SCAN_W, C_GV), col(128, C_KRZ), col(SCAN_W, C_GOG),
                  pl.BlockSpec((128, 2 * kh), lambda b: (0, 0)), pl.BlockSpec((1, 2 * kh), lambda b: (0, 0))],
        out_specs=pl.BlockSpec((None, S, SCAN_W), lambda b: (b, 0, 0)),
        out_shape=jax.ShapeDtypeStruct((BATCH, S, SCAN_W), bf16),
        scratch_shapes=[pltpu.VMEM((S, 2 * kh), f32), pltpu.VMEM((S, SCAN_W), f32), pltpu.VMEM((S, SCAN_W), f32),
                        pltpu.VMEM((SCAN_W, kh), f32), pltpu.VMEM((SCAN_W, kh), f32)],
        compiler_params=_cp(("arbitrary",)),
        name="gla",
    )(p3, p3, p3, p3, p3, w2p, b2).reshape(R, SCAN_W)


def _ret_kernel(q_ref, k_ref, v_ref, g_ref, cos_ref, sin_ref, cumf_ref, cumb_ref, y_ref, of_sc, ob_sc, stf_sc, stb_sc):
    kh = RET_H * RET_DK
    incl, strict, kmask, vmask, smask = _scan_consts(kh, RET_CHUNK)
    stf_sc[...] = jnp.zeros_like(stf_sc)
    stb_sc[...] = jnp.zeros_like(stb_sc)
    kscale = RET_DK ** -0.5
    cumf = cumf_ref[...]
    cumb = cumb_ref[...]

    even = lax.broadcasted_iota(jnp.int32, (RET_CHUNK, RET_DK), 1) % 2 == 0

    def rotate(x, r0):
        cos = jnp.concatenate([cos_ref[pl.ds(r0, RET_CHUNK), :]] * RET_H, axis=1)
        sin = jnp.concatenate([sin_ref[pl.ds(r0, RET_CHUNK), :]] * RET_H, axis=1)
        parts = []
        for h in range(RET_H):
            xh = x[:, h * RET_DK:(h + 1) * RET_DK]
            parts.append(jnp.where(even, pltpu.roll(xh, RET_DK - 1, axis=1), pltpu.roll(xh, 1, axis=1)))
        return x * cos + jnp.concatenate(parts, axis=1) * sin

    def body(i, carry):
        rf = pl.multiple_of(i * RET_CHUNK, RET_CHUNK)
        of_sc[pl.ds(rf, RET_CHUNK), :] = _chunk_step(
            rotate(q_ref[pl.ds(rf, RET_CHUNK), :].astype(f32), rf),
            rotate(k_ref[pl.ds(rf, RET_CHUNK), :].astype(f32), rf) * kscale,
            v_ref[pl.ds(rf, RET_CHUNK), :], cumf, cumf[RET_CHUNK - 1:RET_CHUNK, :], stf_sc, incl, kmask, vmask, smask)
        rb = pl.multiple_of(_bwd_chunk(i, RET_CHUNK) * RET_CHUNK, RET_CHUNK)
        ob_sc[pl.ds(rb, RET_CHUNK), :] = _chunk_step(
            rotate(q_ref[pl.ds(rb, RET_CHUNK), :].astype(f32), rb),
            rotate(k_ref[pl.ds(rb, RET_CHUNK), :].astype(f32), rb) * kscale,
            v_ref[pl.ds(rb, RET_CHUNK), :], cumb, cumb[0:1, :], stb_sc, strict, kmask, vmask, smask)
        return carry

    lax.fori_loop(0, S // RET_CHUNK, body, 0, unroll=2)
    _scan_finish(of_sc, ob_sc, g_ref, y_ref)


def _retention(p, cos_r, sin_r, cumf, cumb):
    kh = RET_H * RET_DK
    p3 = p.reshape(BATCH, S, NC)
    col = lambda w, c: pl.BlockSpec((None, S, w), lambda b: (b, 0, c // w))
    const = lambda shape: pl.BlockSpec(shape, lambda b: (0, 0))
    return pl.pallas_call(
        _ret_kernel,
        grid=(BATCH,),
        in_specs=[col(kh, C_RQ), col(kh, C_RK), col(SCAN_W, C_RV), col(SCAN_W, C_RG),
                  const((S, RET_DK)), const((S, RET_DK)), const((RET_CHUNK, kh)), const((RET_CHUNK, kh))],
        out_specs=pl.BlockSpec((None, S, SCAN_W), lambda b: (b, 0, 0)),
        out_shape=jax.ShapeDtypeStruct((BATCH, S, SCAN_W), bf16),
        scratch_shapes=[pltpu.VMEM((S, SCAN_W), f32), pltpu.VMEM((S, SCAN_W), f32),
                        pltpu.VMEM((SCAN_W, kh), f32), pltpu.VMEM((SCAN_W, kh), f32)],
        compiler_params=_cp(("arbitrary",)),
        name="retention",
    )(p3, p3, p3, p3, cos_r, sin_r, cumf, cumb).reshape(R, SCAN_W)


def _lat_tile(g):
    return (g // LAT_TILES) * TILES + 1 + g % LAT_TILES


def _out_kernel(*refs, n_resid, has_ctx):
    resid, refs = refs[:n_resid], refs[n_resid:]
    *ya, yb_ref, yc_ref, mod_ref, wf_ref, rwt_ref, xo_ref, h2_ref, aff_ref, w_ref = refs
    is_ctx = (pl.program_id(0) % TILES == 0) if has_ctx else False

    @pl.when(pl.program_id(0) == 0)
    def _():
        w_ref[...] = wf_ref[...].astype(bf16)

    na = MLA_H * MLA_V
    half = TM // 2
    for r in range(2):
        rows = slice(r * half, (r + 1) * half)
        acc = _dot(_tile_value(ya, is_ctx, rows), w_ref[0:na, :])
        acc += _dot(yb_ref[rows, :], w_ref[na:na + SCAN_W, :])
        acc += _dot(yc_ref[rows, :], w_ref[na + SCAN_W:, :])
        x = _tile_value(resid, is_ctx, rows) + mod_ref[2:3, :] * acc
        xo_ref[rows, :] = x
        hb = (_rms(x) * (1.0 + mod_ref[4:5, :]) + mod_ref[3:4, :]).astype(bf16)
        bits = pltpu.bitcast(hb.astype(f32), jnp.uint32)
        h2_ref[rows, :] = (bits[:, :D // 2] >> 16) | (bits[:, D // 2:] & jnp.uint32(0xFFFF0000))
        lg = _nt(rwt_ref[...], hb)
        e = jnp.exp(lg - jnp.max(lg, axis=0, keepdims=True))
        aff_ref[:, rows] = e / jnp.sum(e, axis=0, keepdims=True)


def _out_proj(resid, ya, yb, yc, mods, w_out, rwt, layer, has_ctx):
    tile = (lambda g: g) if has_ctx else _lat_tile
    n_tiles = R // TM if has_ctx else BATCH * LAT_TILES
    out_row = lambda w: pl.BlockSpec((TM, w), lambda g: (g, 0))
    return pl.pallas_call(
        functools.partial(_out_kernel, n_resid=len(resid), has_ctx=has_ctx),
        grid=(n_tiles,),
        in_specs=_tile_specs(resid, tile) + _tile_specs(ya, tile) + _tile_specs((yb,), tile) + _tile_specs((yc,), tile) + [
            pl.BlockSpec((None, None, 6, D), lambda g: (layer, _mod_row(tile(g)), 0, 0)),
            pl.BlockSpec((None, D, D), lambda g: (layer, 0, 0), pipeline_mode=pl.Buffered(1)),
            pl.BlockSpec((N_EXP, D), lambda g: (0, 0))],
        out_specs=[out_row(D), out_row(D // 2), pl.BlockSpec((N_EXP, TM), lambda g: (0, g))],
        out_shape=[jax.ShapeDtypeStruct((n_tiles * TM, D), f32), jax.ShapeDtypeStruct((n_tiles * TM, D // 2), jnp.uint32),
                   jax.ShapeDtypeStruct((N_EXP, n_tiles * TM), f32)],
        scratch_shapes=[pltpu.VMEM((D, D), bf16)],
        compiler_params=_cp(("arbitrary",)),
        name="out_proj",
    )(*resid, *ya, yb, yc, mods, w_out, rwt)


def _topk_kernel(aff_ref, post_ref, tok_ref, gate_ref, pos_sc, *, has_ctx):
    ri = lax.broadcasted_iota(jnp.int32, (TM, TM), 0)
    ci = lax.broadcasted_iota(jnp.int32, (TM, TM), 1)
    before = jnp.where(ri < ci, 1.0, 0.0).astype(bf16)

    def prefix_count(m):
        out = []
        off = jnp.zeros((N_EXP, 1), f32)
        for blk in range(m.shape[1] // TM):
            mb = m[:, blk * TM:(blk + 1) * TM]
            out.append(_dot(mb.astype(bf16), before) + off)
            off = off + jnp.sum(mb, axis=1, keepdims=True)
        return jnp.concatenate(out, axis=1) if len(out) > 1 else out[0]

    def select(a, cap, base):
        bits = pltpu.bitcast(a, jnp.int32)
        capf = float(cap)

        def step(i, thr):
            cand = thr | jnp.left_shift(jnp.int32(1), 30 - i)
            cnt = jnp.sum(jnp.where(bits >= cand, 1.0, 0.0), axis=1, keepdims=True)
            return jnp.where(cnt >= capf, cand, thr)

        thr = lax.fori_loop(0, 31, step, jnp.zeros((N_EXP, 1), jnp.int32))
        gt = jnp.where(bits > thr, 1.0, 0.0)
        eq = jnp.where(bits == thr, 1.0, 0.0)
        need = capf - jnp.sum(gt, axis=1, keepdims=True)
        keep = gt + eq * jnp.where(prefix_count(eq) < need, 1.0, 0.0)
        return jnp.where(keep > 0.5, prefix_count(keep) + float(base), -1.0)

    if has_ctx:
        pos = jnp.concatenate([select(aff_ref[:, 0:N_CTX], CAP_CTX, CAP_LAT), select(aff_ref[:, N_CTX:], CAP_LAT, 0)], axis=1)
    else:
        pos = select(aff_ref[...], CAP_LAT, 0)
    n = pos.shape[1]
    slots = tok_ref.shape[0]
    pos_sc[...] = pos
    post_ref[...] = jnp.concatenate([pos, jnp.full((128 - N_EXP, n), -1.0, f32)], axis=0).T

    slot_id = lax.broadcasted_iota(jnp.int32, (slots, n), 0).astype(f32)
    tok_id = lax.broadcasted_iota(jnp.int32, (slots, n), 1).astype(f32) + (pl.program_id(0) * n).astype(f32)
    lane = lax.broadcasted_iota(jnp.int32, (slots, 128), 1)

    def invert(e, carry):
        tok, gate = carry
        hit = pos_sc[pl.ds(e, 1), :] == slot_id
        t_e = jnp.sum(jnp.where(hit, tok_id, 0.0), axis=1, keepdims=True)
        g_e = jnp.sum(jnp.where(hit, aff_ref[pl.ds(e, 1), :], 0.0), axis=1, keepdims=True)
        return jnp.where(lane == e, t_e, tok), jnp.where(lane == e, g_e, gate)

    tok, gate = lax.fori_loop(0, N_EXP, invert, (jnp.zeros((slots, 128), f32), jnp.zeros((slots, 128), f32)))
    tok_ref[...] = tok
    gate_ref[...] = gate


def _topk(aff, has_ctx):
    n = S if has_ctx else N_LAT
    slots = SLOTS if has_ctx else CAP_LAT
    return pl.pallas_call(
        functools.partial(_topk_kernel, has_ctx=has_ctx),
        grid=(BATCH,),
        in_specs=[pl.BlockSpec((N_EXP, n), lambda b: (0, b))],
        out_specs=[pl.BlockSpec((n, 128), lambda b: (b, 0)), pl.BlockSpec((slots, 128), lambda b: (b, 0)),
                   pl.BlockSpec((slots, 128), lambda b: (b, 0))],
        out_shape=[jax.ShapeDtypeStruct((BATCH * n, 128), f32), jax.ShapeDtypeStruct((BATCH * slots, 128), f32),
                   jax.ShapeDtypeStruct((BATCH * slots, 128), f32)],
        scratch_shapes=[pltpu.VMEM((N_EXP, n), f32)],
        compiler_params=_cp(("arbitrary",)),
        name="route_topk",
    )(aff)


TF = 512
N_UP = EXP_FF // TF
N_DOWN = D // TF
N_STEP = N_UP + N_DOWN
HW = D // 2
MOE_VMEM_LIMIT = 60 * 1024 * 1024


def _moe_kernel(idx_ref, h_hbm, wg_ref, wu_ref, wd_ref, gate_ref, y_ref, xraw, xlo, xhi, hm_ref, sem):
    e = pl.program_id(0)
    s = pl.program_id(1)
    rows = xlo.shape[0]
    per = rows // N_STEP
    cur = e % 2
    nxt = 1 - cur

    def row_copy(expert, r, slot):
        src = idx_ref[expert * rows + r]
        return pltpu.make_async_copy(h_hbm.at[pl.ds(src, 1), :], xraw.at[slot, pl.ds(r, 1), :], sem.at[slot])

    def all_rows(slot):
        return pltpu.make_async_copy(h_hbm.at[pl.ds(0, rows), :], xraw.at[slot], sem.at[slot])

    def prefetch_share():
        for r in range(per):
            row_copy(e + 1, s * per + r, nxt).start()

    @pl.when(s == 0)
    def _():
        @pl.when(e == 0)
        def _():
            def first(r, carry):
                row_copy(0, r, 0).start()
                return carry

            lax.fori_loop(0, rows, first, 0)

        all_rows(cur).wait()
        w = xraw[cur]
        xlo[...] = pltpu.bitcast(w << 16, f32).astype(bf16)
        xhi[...] = pltpu.bitcast(w & jnp.uint32(0xFFFF0000), f32).astype(bf16)

    @pl.when(s < N_UP)
    def _():
        prefetch_share()
        wg = wg_ref[...].astype(bf16)
        wu = wu_ref[...].astype(bf16)
        a = _dot(xlo[...], wg[:HW, :]) + _dot(xhi[...], wg[HW:, :])
        u = _dot(xlo[...], wu[:HW, :]) + _dot(xhi[...], wu[HW:, :])
        hm_ref[s] = (_silu(a) * u).astype(bf16)

    @pl.when(s >= N_UP)
    def _():
        prefetch_share()
        wd = wd_ref[...].astype(bf16)
        acc = _dot(hm_ref[0], wd[0:TF, :])
        for c in range(1, N_UP):
            acc += _dot(hm_ref[c], wd[c * TF:(c + 1) * TF, :])
        lane = lax.broadcasted_iota(jnp.int32, (rows, 128), 1)
        g = jnp.sum(jnp.where(lane == e, gate_ref[...], 0.0), axis=1, keepdims=True)
        y_ref[...] = (acc * g).astype(bf16)

    @pl.when((e == N_EXP - 1) & (s == N_STEP - 1))
    def _():
        all_rows(nxt).wait()


def _moe_ffn(tok, gate, h2p, w_gate, w_up, w_down, layer):
    rows = tok.shape[0]
    idx = tok[:, :N_EXP].T.astype(jnp.int32).reshape(-1)
    idx = jnp.concatenate([idx, jnp.zeros((rows,), jnp.int32)])
    up = lambda s: jnp.minimum(s, N_UP - 1)
    down = lambda s: jnp.maximum(s - N_UP, 0)
    return pl.pallas_call(
        _moe_kernel,
        grid_spec=pltpu.PrefetchScalarGridSpec(
            num_scalar_prefetch=1,
            grid=(N_EXP, N_STEP),
            in_specs=[pl.BlockSpec(memory_space=pl.ANY),
                      pl.BlockSpec((None, None, D, TF), lambda e, s, idx: (layer, e, 0, up(s))),
                      pl.BlockSpec((None, None, D, TF), lambda e, s, idx: (layer, e, 0, up(s))),
                      pl.BlockSpec((None, None, EXP_FF, TF), lambda e, s, idx: (layer, e, 0, down(s))),
                      pl.BlockSpec((rows, 128), lambda e, s, idx: (0, 0))],
            out_specs=pl.BlockSpec((None, rows, TF), lambda e, s, idx: (e, 0, down(s))),
            scratch_shapes=[pltpu.VMEM((2, rows, HW), jnp.uint32), pltpu.VMEM((rows, HW), bf16),
                            pltpu.VMEM((rows, HW), bf16), pltpu.VMEM((N_UP, rows, TF), bf16),
                            pltpu.SemaphoreType.DMA((2,))]),
        out_shape=jax.ShapeDtypeStruct((N_EXP, rows, D), bf16),
        compiler_params=pltpu.CompilerParams(dimension_semantics=("arbitrary", "arbitrary"),
                                             vmem_limit_bytes=MOE_VMEM_LIMIT, disable_bounds_checks=True),
        name="moe_ffn",
    )(idx, h2p, w_gate, w_up, w_down, gate)


TN_C = 1024


def _combine_kernel(post_ref, ys_ref, x_ref, mod_ref, *rest, has_ctx):
    gain_ref, o_ref = rest if len(rest) == 2 else (None, rest[0])
    tn = o_ref.shape[-1]
    t = pl.program_id(2) if has_ctx else pl.program_id(2) + 1
    pb = post_ref[...]

    def scatter(cap, base, ys):
        if cap % 128 == 0:
            slot = lax.broadcasted_iota(jnp.int32, (TM, cap), 1).astype(f32) + float(base)
            onehot = jnp.concatenate(
                [jnp.where(pb[:, e:e + 1] == slot, 1.0, 0.0).astype(bf16) for e in range(N_EXP)], axis=1)
        else:
            pc = pb - float(base)
            pc = jnp.where((pc >= 0.0) & (pc < float(cap)), pc, -1.0).astype(bf16)
            er = lax.broadcasted_iota(jnp.int32, (128, N_EXP * cap), 0)
            ec = lax.broadcasted_iota(jnp.int32, (128, N_EXP * cap), 1) // cap
            rep = jnp.where(er == ec, 1.0, 0.0).astype(bf16)
            slot = (lax.broadcasted_iota(jnp.int32, (TM, N_EXP * cap), 1) % cap).astype(f32)
            onehot = jnp.where(_dot(pc, rep) == slot, 1.0, 0.0).astype(bf16)
        x = x_ref[...] + mod_ref[5:6, :] * _dot(onehot, ys)
        o_ref[...] = x if gain_ref is None else _rms(x) * gain_ref[...]

    if has_ctx:
        @pl.when(t == 0)
        def _():
            scatter(CAP_CTX, CAP_LAT, ys_ref[:, CAP_LAT:SLOTS, :].reshape(N_EXP * CAP_CTX, tn))

    @pl.when(t > 0)
    def _():
        scatter(CAP_LAT, 0, ys_ref[:, 0:CAP_LAT, :].reshape(N_EXP * CAP_LAT, tn))


def _combine(post, ys, xa, mods, layer, has_ctx, final_gain=None):
    final = final_gain is not None
    assert not (final and has_ctx)
    slots = SLOTS if has_ctx else CAP_LAT
    tps = TILES if has_ctx else LAT_TILES
    tn = D if final else TN_C
    tile = lambda b, t: b * tps + t
    mod_row = (lambda b, t: jnp.where(t == 0, BATCH, b)) if has_ctx else (lambda b, t: b)
    extra_in, extra_specs = ((final_gain.reshape(1, D),), [pl.BlockSpec((1, D), lambda b, n, t: (0, 0))]) if final else ((), [])
    out = pl.pallas_call(
        functools.partial(_combine_kernel, has_ctx=has_ctx),
        grid=(BATCH, D // tn, tps),
        in_specs=[pl.BlockSpec((TM, 128), lambda b, n, t: (tile(b, t), 0)),
                  pl.BlockSpec((N_EXP, None, slots, tn), lambda b, n, t: (0, b, 0, n)),
                  pl.BlockSpec((TM, tn), lambda b, n, t: (tile(b, t), n)),
                  pl.BlockSpec((None, None, 6, tn), lambda b, n, t: (layer, mod_row(b, t), 0, n))] + extra_specs,
        out_specs=pl.BlockSpec((TM, tn), lambda b, n, t: (tile(b, t), n)),
        out_shape=jax.ShapeDtypeStruct((BATCH * tps * TM, D), f32),
        compiler_params=_cp(("arbitrary", "arbitrary", "arbitrary")),
        name="moe_combine",
    )(post, ys.reshape(N_EXP, BATCH, slots, D), xa, mods, *extra_in)
    return out.reshape(BATCH, N_LAT, D) if final else out


def _in_weights(w):
    o = np.cumsum((0, MLA_QR, MLA_KVR, MLA_ROPE, 256, 256, 512, 32, 512, 512, 512, 512, 512))
    piece = lambda i, j: w[:, o[i]:o[j]]
    cq, ckv, kr, gq, gk, gv, gz, gog = (piece(i, i + 1) for i in range(8))
    ret = piece(8, 12)
    pad = jnp.zeros((D, NC - (C_KRZ + MLA_ROPE + 2 * GLA_RANK)), w.dtype)
    return jnp.concatenate([ret, gv, gog, cq, ckv, gq, gk, kr, gz, pad], axis=1).astype(bf16)


def _mla_weights(w_uq, w_ukv):
    half = MLA_ROPE // 2
    wq = w_uq.reshape(MLA_QR, MLA_H, MLA_NOPE + MLA_ROPE)
    nope = wq[:, :, :MLA_NOPE]
    rope = wq[:, :, MLA_NOPE:].reshape(MLA_QR, MLA_H, half, 2)
    ev, od = rope[..., 0], rope[..., 1]
    zpad = jnp.zeros((MLA_QR, MLA_H, HP - MLA_NOPE - MLA_ROPE), w_uq.dtype)
    q_main = jnp.concatenate([nope, ev, od, zpad], axis=-1).reshape(MLA_QR, MLA_H * HP)
    q_part = jnp.concatenate([jnp.zeros_like(nope), od, ev, zpad], axis=-1).reshape(MLA_QR, MLA_H * HP)
    wkv = w_ukv.reshape(MLA_KVR, MLA_H, MLA_NOPE + MLA_V)
    k_main = jnp.concatenate([wkv[:, :, :MLA_NOPE], jnp.zeros((MLA_KVR, MLA_H, HP - MLA_NOPE), w_ukv.dtype)], axis=-1)
    v_main = wkv[:, :, MLA_NOPE:]
    return (jnp.concatenate([q_main, q_part], axis=1).astype(bf16),
            jnp.concatenate([k_main.reshape(MLA_KVR, MLA_H * HP), v_main.reshape(MLA_KVR, MLA_H * MLA_V)], axis=1).astype(bf16))


def _rope_key_placement():
    half = MLA_ROPE // 2
    nq = MLA_H * HP
    e2 = np.zeros((128, 2 * nq), np.float32)
    for h in range(MLA_H):
        for i in range(half):
            ev, od = h * HP + MLA_NOPE + i, h * HP + MLA_NOPE + half + i
            e2[2 * i, ev] = e2[2 * i + 1, od] = 1.0
            e2[2 * i + 1, nq + ev] = e2[2 * i, nq + od] = 1.0
    return jnp.asarray(e2, bf16)


def _tables():
    rows = N_LAT // GRID_W
    row = np.repeat(np.arange(rows, dtype=np.float32), GRID_W)
    colp = np.tile(np.arange(GRID_W, dtype=np.float32), rows)
    n_freq = MLA_ROPE // 4
    nf32 = np.float32
    inv = np.power(nf32(ROPE_BASE), -np.arange(n_freq, dtype=nf32) / nf32(n_freq))
    ang = np.concatenate([row[:, None] * inv, colp[:, None] * inv], axis=-1)
    cos_a, sin_a = np.cos(ang), np.sin(ang)
    one = np.ones((N_LAT, MLA_NOPE), nf32)
    zpad = np.zeros((N_LAT, HP - MLA_NOPE - MLA_ROPE), nf32)
    tc_lat = np.concatenate([one, cos_a, cos_a, zpad], axis=1)
    ts_lat = np.concatenate([0 * one, -sin_a, sin_a, zpad], axis=1)
    tc_ctx = np.concatenate([np.ones((N_CTX, MLA_NOPE + MLA_ROPE), nf32), np.zeros((N_CTX, HP - MLA_NOPE - MLA_ROPE), nf32)], axis=1)
    tc = np.concatenate([tc_ctx, tc_lat], axis=0)
    ts = np.concatenate([np.zeros((N_CTX, HP), nf32), ts_lat], axis=0)

    inv_r = nf32(1.0) / np.power(nf32(ROPE_BASE), np.linspace(0.0, 1.0, RET_DK // 2, dtype=nf32))
    ang_r = np.arange(N_LAT, dtype=nf32)[:, None] * inv_r
    cos_r = np.concatenate([np.ones((N_CTX, RET_DK), nf32), np.repeat(np.cos(ang_r), 2, axis=1)], axis=0)
    sin_r = np.concatenate([np.zeros((N_CTX, RET_DK), nf32),
                            np.stack([-np.sin(ang_r), np.sin(ang_r)], axis=-1).reshape(N_LAT, RET_DK)], axis=0)

    def log_decay(direction):
        e = nf32(RET_EXP0 + direction) + nf32(2.0) * np.arange(RET_H, dtype=nf32)
        return np.repeat(np.log1p(-np.exp2(-e)), RET_DK)[None, :]

    steps = np.arange(1, RET_CHUNK + 1, dtype=nf32)[:, None]
    cumf = steps * log_decay(0.0)
    cumb = steps[::-1] * log_decay(1.0)
    return tc, ts, cos_r, sin_r, cumf, cumb


def kernel(x, c, ctx, c_ctx, ada_w, ada_b, w_in, mla_q_norm, mla_w_uq, mla_kv_norm, mla_w_ukv, gla_gate_w2,
           gla_gate_b, w_out, router_w, exp_w_gate, exp_w_up, exp_w_down, final_norm):
    resid = (x, ctx)
    cc = jnp.concatenate([c, c_ctx[None, :], jnp.zeros((8 - BATCH - 1, D), f32)], axis=0)
    mods = _modulation(cc, ada_w, ada_b).reshape(DEPTH, 8, 6, D)
    tc, ts, cos_r, sin_r, cumf, cumb = _tables()
    e2 = _rope_key_placement()
    kh = GLA_H * GLA_DK

    for l in range(DEPTH):
        w_in_p = _in_weights(w_in[l])
        wq2, wkv = _mla_weights(mla_w_uq[l], mla_w_ukv[l])
        w2p = jnp.zeros((128, 2 * kh), f32)
        w2p = w2p.at[64:64 + GLA_RANK, 0:kh].set(gla_gate_w2[l, 0]).at[64 + GLA_RANK:64 + 2 * GLA_RANK, kh:].set(gla_gate_w2[l, 1])
        b2 = gla_gate_b[l].reshape(1, 2 * kh)

        has_ctx = l < DEPTH - 1
        p = _in_proj(resid, mods, w_in_p, l)
        q, k, vt = _mla_prep(p, tc, ts, mla_q_norm[l].reshape(1, -1), mla_kv_norm[l].reshape(1, -1), wq2, wkv, e2)
        ya = _attention(q, k, vt, has_ctx)
        yb = _gla(p, w2p.astype(bf16), b2)
        yc = _retention(p, cos_r, sin_r, cumf, cumb)
        xa, h2p, aff = _out_proj(resid, ya, yb, yc, mods, w_out, router_w[l].T.astype(bf16), l, has_ctx)
        post, tok, gate = _topk(aff, has_ctx)
        ys = _moe_ffn(tok, gate, h2p, exp_w_gate, exp_w_up, exp_w_down, l)
        resid = (_combine(post, ys, xa, mods, l, has_ctx, final_gain=None if has_ctx else final_norm),)
    return resid[0]
```

```python
import functools

import numpy as np
import jax
import jax.numpy as jnp
from jax import lax
from jax.experimental import pallas as pl
from jax.experimental.pallas import tpu as pltpu

f32 = jnp.float32
bf16 = jnp.bfloat16

D = 2048
BATCH = 4
N_LAT = 2048
N_CTX = 256
S = N_CTX + N_LAT
R = BATCH * S
DEPTH = 2
GRID_W = 64
EPS = 1e-6
LOG2E = 1.4426950408889634
ROPE_BASE = 10000.0
CHUNK = 64

MLA_H, MLA_QR, MLA_KVR, MLA_NOPE, MLA_ROPE, MLA_V = 8, 512, 256, 128, 64, 128
GLA_H, GLA_DK, GLA_DV, GLA_RANK, GLA_TAU = 4, 64, 128, 16, 16.0
RET_H, RET_DK, RET_DV, RET_EXP0 = 4, 128, 128, 5.0
N_EXP, EXP_FF, EC_CAP = 16, 2048, 2
CAP_LAT = EC_CAP * N_LAT // N_EXP
CAP_CTX = EC_CAP * N_CTX // N_EXP
SLOTS = CAP_LAT + CAP_CTX

TM = 256
TILES = S // TM
LAT_TILES = N_LAT // TM
HP = 256

C_RQ, C_RK, C_RV, C_RG = 0, 512, 1024, 1536
C_GV, C_GOG, C_CQ, C_CKV, C_GQ, C_GK, C_KRZ = 2048, 2560, 3072, 3584, 3840, 4096, 4352
NC = 4608
TN_IN = 1536

VMEM_LIMIT = 56 * 1024 * 1024


def _cp(sem):
    return pltpu.CompilerParams(dimension_semantics=sem, vmem_limit_bytes=VMEM_LIMIT)


def _nt(a, b):
    return lax.dot_general(a, b, (((1,), (1,)), ((), ())), preferred_element_type=f32)


def _tn(a, b):
    return lax.dot_general(a, b, (((0,), (0,)), ((), ())), preferred_element_type=f32)


def _dot(a, b):
    return jnp.dot(a, b, preferred_element_type=f32)


def _rms(x):
    return x * lax.rsqrt(jnp.mean(x * x, axis=-1, keepdims=True) + EPS)


def _silu(x):
    return x * (1.0 / (1.0 + jnp.exp(-x)))


def _mod_row(i):
    return jnp.where(i % TILES == 0, BATCH, i // TILES)


def _lat_first(i):
    return (i // TILES) * TILES + (i % TILES + TILES - 1) % TILES


def _mod_kernel(s_ref, w_ref, b_ref, o_ref):
    s = _silu(s_ref[...]).astype(bf16)
    o_ref[...] = _dot(s, w_ref[...].astype(bf16)) + b_ref[...]


def _modulation(cc, ada_w, ada_b):
    tn = 1024
    return pl.pallas_call(
        _mod_kernel,
        grid=(DEPTH, 6 * D // tn),
        in_specs=[pl.BlockSpec((8, D), lambda l, j: (0, 0)),
                  pl.BlockSpec((None, D, tn), lambda l, j: (l, 0, j)),
                  pl.BlockSpec((None, 1, tn), lambda l, j: (l, 0, j))],
        out_specs=pl.BlockSpec((None, 8, tn), lambda l, j: (l, 0, j)),
        out_shape=jax.ShapeDtypeStruct((DEPTH, 8, 6 * D), f32),
        compiler_params=_cp(("arbitrary", "arbitrary")),
        name="modulation",
    )(cc, ada_w, ada_b.reshape(DEPTH, 1, 6 * D))


def _tile_specs(arrs, tile_of):
    w = arrs[0].shape[-1]
    if arrs[0].ndim == 2:
        return [pl.BlockSpec((TM, w), lambda *g: (tile_of(*g), 0))]
    lat = pl.BlockSpec((None, TM, w), lambda *g: (tile_of(*g) // TILES, jnp.maximum(tile_of(*g) % TILES - 1, 0), 0))
    if len(arrs) == 1:
        return [lat]
    return [lat, pl.BlockSpec((None, TM, w), lambda *g: (tile_of(*g) // TILES, 0, 0))]


def _tile_value(refs, is_ctx, rows=slice(None)):
    if len(refs) == 1:
        return refs[0][rows, :]
    return jnp.where(is_ctx, refs[1][rows, :], refs[0][rows, :])


def _in_pieces():
    o = np.cumsum((0, MLA_QR, MLA_KVR, MLA_ROPE, 256, 256, 512, 32, 512, 512, 512, 512, 512))
    cq, ckv, kr, gq, gk, gv, gz, gog, ret = (int(v) for v in o[:9])
    return ((ret, 2048, C_RQ), (gv, 512, C_GV), (gog, 512, C_GOG), (cq, 512, C_CQ), (ckv, 256, C_CKV),
            (gq, 256, C_GQ), (gk, 256, C_GK), (kr, MLA_ROPE, C_KRZ), (gz, 2 * GLA_RANK, C_KRZ + MLA_ROPE))


IN_COLS = 4448
RB = 256


def _in_kernel(*refs):
    *resid, mod_ref, ws_ref, o_ref, w_ref = refs

    @pl.when(pl.program_id(0) == 0)
    def _():
        def move(i, carry):
            r0 = pl.multiple_of(i * RB, RB)
            for src, width, dst in _in_pieces():
                w_ref[pl.ds(r0, RB), dst:dst + width] = ws_ref[pl.ds(r0, RB), src:src + width]
            end = C_KRZ + MLA_ROPE + 2 * GLA_RANK
            w_ref[pl.ds(r0, RB), end:NC] = jnp.zeros((RB, NC - end), bf16)
            return carry

        lax.fori_loop(0, D // RB, move, 0)

    x = _tile_value(resid, pl.program_id(0) % TILES == 0)
    h = (_rms(x) * (1.0 + mod_ref[1:2, :]) + mod_ref[0:1, :]).astype(bf16)
    for j in range(NC // TN_IN):
        o_ref[:, j * TN_IN:(j + 1) * TN_IN] = _dot(h, w_ref[:, j * TN_IN:(j + 1) * TN_IN]).astype(bf16)


def _in_proj(resid, mods, w_in_bf, layer):
    return pl.pallas_call(
        _in_kernel,
        grid=(R // TM,),
        in_specs=_tile_specs(resid, lambda i: i) + [
            pl.BlockSpec((None, None, 6, D), lambda i: (layer, _mod_row(i), 0, 0)),
            pl.BlockSpec((None, D, IN_COLS), lambda i: (layer, 0, 0), pipeline_mode=pl.Buffered(1))],
        out_specs=pl.BlockSpec((TM, NC), lambda i: (i, 0)),
        out_shape=jax.ShapeDtypeStruct((R, NC), bf16),
        scratch_shapes=[pltpu.VMEM((D, NC), bf16)],
        compiler_params=pltpu.CompilerParams(dimension_semantics=("arbitrary",), vmem_limit_bytes=60 * 1024 * 1024),
        name="in_proj",
    )(*resid, mods, w_in_bf)


def _mla_prep_kernel(cq_ref, ckv_ref, krz_ref, tc_ref, ts_ref, qn_ref, kvn_ref, wq_ref, wkv_ref, e2_ref,
                     q_ref, k_ref, vt_ref):
    tc = tc_ref[...]
    ts = ts_ref[...]
    scale = (MLA_NOPE + MLA_ROPE) ** -0.5 * LOG2E
    hq = (_rms(cq_ref[...].astype(f32)) * qn_ref[...]).astype(bf16)
    q2 = _dot(hq, wq_ref[...])
    hkv = (_rms(ckv_ref[...].astype(f32)) * kvn_ref[...]).astype(bf16)
    kv = _dot(hkv, wkv_ref[...])
    kr2 = _dot(krz_ref[...], e2_ref[...])
    nq = MLA_H * HP
    for h in range(MLA_H):
        sl = slice(h * HP, (h + 1) * HP)
        sl2 = slice(nq + h * HP, nq + (h + 1) * HP)
        q_ref[:, sl] = ((q2[:, sl] * tc + q2[:, sl2] * ts) * scale).astype(bf16)
        k_ref[:, sl] = (kv[:, sl] + kr2[:, sl] * tc + kr2[:, sl2] * ts).astype(bf16)
    vt_ref[...] = kv[:, nq:].T.astype(bf16)


def _mla_prep(p, tc, ts, qn, kvn, wq2, wkv, e2):
    nq = MLA_H * HP
    nv = MLA_H * MLA_V
    const = lambda shape: pl.BlockSpec(shape, lambda i: (0, 0))
    return pl.pallas_call(
        _mla_prep_kernel,
        grid=(R // TM,),
        in_specs=[pl.BlockSpec((TM, MLA_QR), lambda i: (i, C_CQ // MLA_QR)),
                  pl.BlockSpec((TM, MLA_KVR), lambda i: (i, C_CKV // MLA_KVR)),
                  pl.BlockSpec((TM, 128), lambda i: (i, C_KRZ // 128)),
                  pl.BlockSpec((TM, HP), lambda i: (i % TILES, 0)),
                  pl.BlockSpec((TM, HP), lambda i: (i % TILES, 0)),
                  const((1, MLA_QR)), const((1, MLA_KVR)),
                  const((MLA_QR, 2 * nq)), const((MLA_KVR, nq + nv)), const((128, 2 * nq))],
        out_specs=[pl.BlockSpec((TM, nq), lambda i: (_lat_first(i), 0)),
                   pl.BlockSpec((TM, nq), lambda i: (i, 0)),
                   pl.BlockSpec((None, nv, TM), lambda i: (i // TILES, 0, i % TILES))],
        out_shape=[jax.ShapeDtypeStruct((R, nq), bf16), jax.ShapeDtypeStruct((R, nq), bf16),
                   jax.ShapeDtypeStruct((BATCH, nv, S), bf16)],
        compiler_params=_cp(("arbitrary",)),
        name="mla_prep",
    )(p, p, p, tc, ts, qn, kvn, wq2, wkv, e2)


TQ = 512
ATT_HEADS_PER_STEP = 1


def _attn_body(q_ref, k_ref, vt_ref, o_ref, s_sc, n_chunks, n_heads):
    nq = q_ref.shape[0]
    qs = [q_ref[:, h * HP:(h + 1) * HP] for h in range(n_heads)]

    def scores(h, j, m):
        s = _nt(k_ref[j * TM:(j + 1) * TM, h * HP:(h + 1) * HP], qs[h])
        s_sc[h, j] = s
        cm = jnp.max(s, axis=0, keepdims=True)
        return cm if m is None else jnp.maximum(m, cm)

    m = [None] * n_heads
    for j in range(n_chunks):
        m[0] = scores(0, j, m[0])
    for h in range(n_heads):
        l = jnp.zeros((1, nq), f32)
        acc = jnp.zeros((MLA_V, nq), f32)
        for j in range(n_chunks):
            p = jnp.exp2(s_sc[h, j] - m[h])
            l = l + jnp.sum(p, axis=0, keepdims=True)
            acc = acc + _dot(vt_ref[h * MLA_V:(h + 1) * MLA_V, j * TM:(j + 1) * TM], p.astype(bf16))
            if h + 1 < n_heads:
                m[h + 1] = scores(h + 1, j, m[h + 1])
        o_ref[:, h * MLA_V:(h + 1) * MLA_V] = (acc * (1.0 / l)).T.astype(bf16)


def _attention(q, k, vt, has_ctx):
    nq = MLA_H * HP
    nv = MLA_H * MLA_V
    q3 = q.reshape(BATCH, S, nq)
    k3 = k.reshape(BATCH, S, nq)
    hs = ATT_HEADS_PER_STEP
    ya = pl.pallas_call(
        functools.partial(_attn_body, n_chunks=S // TM, n_heads=hs),
        grid=(BATCH, MLA_H // hs, N_LAT // TQ),
        in_specs=[pl.BlockSpec((None, TQ, hs * HP), lambda b, h, t: (b, t, h)),
                  pl.BlockSpec((None, S, hs * HP), lambda b, h, t: (b, 0, h)),
                  pl.BlockSpec((None, hs * MLA_V, S), lambda b, h, t: (b, h, 0))],
        out_specs=pl.BlockSpec((None, TQ, hs * MLA_V), lambda b, h, t: (b, t, h)),
        out_shape=jax.ShapeDtypeStruct((BATCH, N_LAT, nv), bf16),
        scratch_shapes=[pltpu.VMEM((hs, S // TM, TM, TQ), f32)],
        compiler_params=_cp(("arbitrary", "arbitrary", "arbitrary")),
        name="mla_attention",
    )(q3, k3, vt)
    if not has_ctx:
        return (ya,)
    ya_ctx = pl.pallas_call(
        functools.partial(_attn_body, n_chunks=1, n_heads=1),
        grid=(BATCH, MLA_H),
        in_specs=[pl.BlockSpec((None, N_CTX, HP), lambda b, h: (b, N_LAT // N_CTX, h)),
                  pl.BlockSpec((None, N_CTX, HP), lambda b, h: (b, 0, h)),
                  pl.BlockSpec((None, MLA_V, N_CTX), lambda b, h: (b, h, 0))],
        out_specs=pl.BlockSpec((None, N_CTX, MLA_V), lambda b, h: (b, 0, h)),
        out_shape=jax.ShapeDtypeStruct((BATCH, N_CTX, nv), bf16),
        scratch_shapes=[pltpu.VMEM((1, 1, TM, N_CTX), f32)],
        compiler_params=_cp(("arbitrary", "arbitrary")),
        name="mla_attention_ctx",
    )(q3, k3, vt)
    return (ya, ya_ctx)


RET_CHUNK = 128
SCAN_H = 4
SCAN_V = 128
SCAN_W = SCAN_H * SCAN_V


def _scan_consts(kh, chunk):
    dk = kh // SCAN_H
    row = lax.broadcasted_iota(jnp.int32, (chunk, SCAN_H * chunk), 0)
    col = lax.broadcasted_iota(jnp.int32, (chunk, SCAN_H * chunk), 1) % chunk
    incl = row >= col
    strict = col > row
    krow = lax.broadcasted_iota(jnp.int32, (SCAN_H * chunk, kh), 0) // chunk
    kcol = lax.broadcasted_iota(jnp.int32, (SCAN_H * chunk, kh), 1) // dk
    kmask = krow == kcol
    vrow = lax.broadcasted_iota(jnp.int32, (SCAN_H * chunk, SCAN_W), 0) // chunk
    vcol = lax.broadcasted_iota(jnp.int32, (SCAN_H * chunk, SCAN_W), 1) // SCAN_V
    vmask = vrow == vcol
    srow = lax.broadcasted_iota(jnp.int32, (SCAN_W, kh), 0) // SCAN_V
    scol = lax.broadcasted_iota(jnp.int32, (SCAN_W, kh), 1) // dk
    smask = srow == scol
    return incl, strict, kmask, vmask, smask


def _chunk_step(q, k, v, cum, cend, st_ref, amask, kmask, vmask, smask):
    qd = (q * jnp.exp(cum)).astype(bf16)
    ki = k * jnp.exp(-cum)
    kend = (k * jnp.exp(cend - cum)).astype(bf16)
    dec = jnp.exp(cend)
    kst = jnp.where(kmask, jnp.concatenate([ki] * SCAN_H, axis=0), 0.0).astype(bf16)
    att = jnp.where(amask, _nt(qd, kst), 0.0).astype(bf16)
    vbd = jnp.where(vmask, jnp.concatenate([v] * SCAN_H, axis=0), jnp.zeros((), bf16))
    st = st_ref[...]
    o = _dot(att, vbd) + _nt(qd, st.astype(bf16))
    st_ref[...] = st * dec + jnp.where(smask, _tn(v, kend), 0.0)
    return o


def _bwd_chunk(i, chunk):
    n_ctx, n_all = N_CTX // chunk, S // chunk
    return jnp.where(i < n_ctx, n_ctx - 1 - i, n_all + n_ctx - 1 - i)


def _scan_finish(of_sc, ob_sc, g_ref, y_ref):
    def fin(i, carry):
        r0 = pl.multiple_of(i * TM, TM)
        o = of_sc[pl.ds(r0, TM), :] + ob_sc[pl.ds(r0, TM), :]
        g = g_ref[pl.ds(r0, TM), :].astype(f32)
        for h in range(SCAN_H):
            sl = slice(h * SCAN_V, (h + 1) * SCAN_V)
            y_ref[pl.ds(r0, TM), sl] = (_rms(o[:, sl]) * _silu(g[:, sl])).astype(bf16)
        return carry

    lax.fori_loop(0, TILES, fin, 0)


def _gla_kernel(q_ref, k_ref, v_ref, krz_ref, og_ref, w2_ref, b2_ref, y_ref, cum_sc, of_sc, ob_sc, stf_sc, stb_sc):
    kh = GLA_H * GLA_DK
    ri = lax.broadcasted_iota(jnp.int32, (TM, TM), 0)
    ci = lax.broadcasted_iota(jnp.int32, (TM, TM), 1)
    same = (ri // CHUNK) == (ci // CHUNK)
    pre = jnp.where(same & (ci <= ri), 1.0, 0.0).astype(bf16)
    suf = jnp.where(same & (ci >= ri), 1.0, 0.0).astype(bf16)

    def exact_sum(m, x):
        hi = x.astype(bf16)
        r1 = x - hi.astype(f32)
        mid = r1.astype(bf16)
        lo = (r1 - mid.astype(f32)).astype(bf16)
        return _dot(m, hi) + _dot(m, mid) + _dot(m, lo)

    def gates(i, carry):
        r0 = pl.multiple_of(i * TM, TM)
        lg = _dot(krz_ref[pl.ds(r0, TM), :], w2_ref[...]) + b2_ref[...]
        la = (jnp.minimum(lg, 0.0) - jnp.log1p(jnp.exp(-jnp.abs(lg)))) * (1.0 / GLA_TAU)
        cum_sc[pl.ds(r0, TM), 0:kh] = exact_sum(pre, la[:, 0:kh])
        cum_sc[pl.ds(r0, TM), kh:2 * kh] = exact_sum(suf, la[:, kh:2 * kh])
        return carry

    lax.fori_loop(0, TILES, gates, 0, unroll=3)

    incl, strict, kmask, vmask, smask = _scan_consts(kh, CHUNK)
    stf_sc[...] = jnp.zeros_like(stf_sc)
    stb_sc[...] = jnp.zeros_like(stb_sc)
    qscale = GLA_DK ** -0.5

    def body(i, carry):
        rf = pl.multiple_of(i * CHUNK, CHUNK)
        cum = cum_sc[pl.ds(rf, CHUNK), 0:kh]
        of_sc[pl.ds(rf, CHUNK), :] = _chunk_step(
            q_ref[pl.ds(rf, CHUNK), :].astype(f32) * qscale, k_ref[pl.ds(rf, CHUNK), :].astype(f32),
            v_ref[pl.ds(rf, CHUNK), :], cum, cum[CHUNK - 1:CHUNK, :], stf_sc, incl, kmask, vmask, smask)
        rb = pl.multiple_of(_bwd_chunk(i, CHUNK) * CHUNK, CHUNK)
        rc = cum_sc[pl.ds(rb, CHUNK), kh:2 * kh]
        ob_sc[pl.ds(rb, CHUNK), :] = _chunk_step(
            q_ref[pl.ds(rb, CHUNK), :].astype(f32) * qscale, k_ref[pl.ds(rb, CHUNK), :].astype(f32),
            v_ref[pl.ds(rb, CHUNK), :], rc, rc[0:1, :], stb_sc, strict, kmask, vmask, smask)
        return carry

    lax.fori_loop(0, S // CHUNK, body, 0, unroll=6)
    _scan_finish(of_sc, ob_sc, og_ref, y_ref)


def _gla(p, w2p, b2):
    kh = GLA_H * GLA_DK
    p3 = p.reshape(BATCH, S, NC)
    col = lambda w, c: pl.BlockSpec((None, S, w), lambda b: (b, 0, c // w))
    return pl.pallas_call(
        _gla_kernel,
        grid=(BATCH,),
        in_specs=[col(kh, C_GQ), col(kh, C_GK), col(SCAN_W, C_GV), col(128, C_KRZ), col(SCAN_W, C_GOG),
                  pl.BlockSpec((128, 2 * kh), lambda b: (0, 0)), pl.BlockSpec((1, 2 * kh), lambda b: (0, 0))],
        out_specs=pl.BlockSpec((None, S, SCAN_W), lambda b: (b, 0, 0)),
        out_shape=jax.ShapeDtypeStruct((BATCH, S, SCAN_W), bf16),
        scratch_shapes=[pltpu.VMEM((S, 2 * kh), f32), pltpu.VMEM((S, SCAN_W), f32), pltpu.VMEM((S, SCAN_W), f32),
                        pltpu.VMEM((SCAN_W, kh), f32), pltpu.VMEM((SCAN_W, kh), f32)],
        compiler_params=_cp(("arbitrary",)),
        name="gla",
    )(p3, p3, p3, p3, p3, w2p, b2).reshape(R, SCAN_W)


def _ret_kernel(q_ref, k_ref, v_ref, g_ref, cos_ref, sin_ref, cumf_ref, cumb_ref, y_ref, of_sc, ob_sc, stf_sc, stb_sc):
    kh = RET_H * RET_DK
    incl, strict, kmask, vmask, smask = _scan_consts(kh, RET_CHUNK)
    stf_sc[...] = jnp.zeros_like(stf_sc)
    stb_sc[...] = jnp.zeros_like(stb_sc)
    kscale = RET_DK ** -0.5
    cumf = cumf_ref[...]
    cumb = cumb_ref[...]

    even = lax.broadcasted_iota(jnp.int32, (RET_CHUNK, RET_DK), 1) % 2 == 0

    def rotate(x, r0):
        cos = jnp.concatenate([cos_ref[pl.ds(r0, RET_CHUNK), :]] * RET_H, axis=1)
        sin = jnp.concatenate([sin_ref[pl.ds(r0, RET_CHUNK), :]] * RET_H, axis=1)
        parts = []
        for h in range(RET_H):
            xh = x[:, h * RET_DK:(h + 1) * RET_DK]
            parts.append(jnp.where(even, pltpu.roll(xh, RET_DK - 1, axis=1), pltpu.roll(xh, 1, axis=1)))
        return x * cos + jnp.concatenate(parts, axis=1) * sin

    def body(i, carry):
        rf = pl.multiple_of(i * RET_CHUNK, RET_CHUNK)
        of_sc[pl.ds(rf, RET_CHUNK), :] = _chunk_step(
            rotate(q_ref[pl.ds(rf, RET_CHUNK), :].astype(f32), rf),
            rotate(k_ref[pl.ds(rf, RET_CHUNK), :].astype(f32), rf) * kscale,
            v_ref[pl.ds(rf, RET_CHUNK), :], cumf, cumf[RET_CHUNK - 1:RET_CHUNK, :], stf_sc, incl, kmask, vmask, smask)
        rb = pl.multiple_of(_bwd_chunk(i, RET_CHUNK) * RET_CHUNK, RET_CHUNK)
        ob_sc[pl.ds(rb, RET_CHUNK), :] = _chunk_step(
            rotate(q_ref[pl.ds(rb, RET_CHUNK), :].astype(f32), rb),
            rotate(k_ref[pl.ds(rb, RET_CHUNK), :].astype(f32), rb) * kscale,
            v_ref[pl.ds(rb, RET_CHUNK), :], cumb, cumb[0:1, :], stb_sc, strict, kmask, vmask, smask)
        return carry

    lax.fori_loop(0, S // RET_CHUNK, body, 0, unroll=2)
    _scan_finish(of_sc, ob_sc, g_ref, y_ref)


def _retention(p, cos_r, sin_r, cumf, cumb):
    kh = RET_H * RET_DK
    p3 = p.reshape(BATCH, S, NC)
    col = lambda w, c: pl.BlockSpec((None, S, w), lambda b: (b, 0, c // w))
    const = lambda shape: pl.BlockSpec(shape, lambda b: (0, 0))
    return pl.pallas_call(
        _ret_kernel,
        grid=(BATCH,),
        in_specs=[col(kh, C_RQ), col(kh, C_RK), col(SCAN_W, C_RV), col(SCAN_W, C_RG),
                  const((S, RET_DK)), const((S, RET_DK)), const((RET_CHUNK, kh)), const((RET_CHUNK, kh))],
        out_specs=pl.BlockSpec((None, S, SCAN_W), lambda b: (b, 0, 0)),
        out_shape=jax.ShapeDtypeStruct((BATCH, S, SCAN_W), bf16),
        scratch_shapes=[pltpu.VMEM((S, SCAN_W), f32), pltpu.VMEM((S, SCAN_W), f32),
                        pltpu.VMEM((SCAN_W, kh), f32), pltpu.VMEM((SCAN_W, kh), f32)],
        compiler_params=_cp(("arbitrary",)),
        name="retention",
    )(p3, p3, p3, p3, cos_r, sin_r, cumf, cumb).reshape(R, SCAN_W)


def _lat_tile(g):
    return (g // LAT_TILES) * TILES + 1 + g % LAT_TILES


def _out_kernel(*refs, n_resid, has_ctx):
    resid, refs = refs[:n_resid], refs[n_resid:]
    *ya, yb_ref, yc_ref, mod_ref, wf_ref, rwt_ref, xo_ref, h2_ref, aff_ref, w_ref = refs
    is_ctx = (pl.program_id(0) % TILES == 0) if has_ctx else False

    @pl.when(pl.program_id(0) == 0)
    def _():
        w_ref[...] = wf_ref[...].astype(bf16)

    na = MLA_H * MLA_V
    half = TM // 2
    for r in range(2):
        rows = slice(r * half, (r + 1) * half)
        acc = _dot(_tile_value(ya, is_ctx, rows), w_ref[0:na, :])
        acc += _dot(yb_ref[rows, :], w_ref[na:na + SCAN_W, :])
        acc += _dot(yc_ref[rows, :], w_ref[na + SCAN_W:, :])
        x = _tile_value(resid, is_ctx, rows) + mod_ref[2:3, :] * acc
        xo_ref[rows, :] = x
        h = _rms(x) * (1.0 + mod_ref[4:5, :]) + mod_ref[3:4, :]
        h2_ref[rows, :] = h
        hb = h.astype(bf16)
        lg = _nt(rwt_ref[...], hb)
        e = jnp.exp(lg - jnp.max(lg, axis=0, keepdims=True))
        aff_ref[:, rows] = e / jnp.sum(e, axis=0, keepdims=True)


def _out_proj(resid, ya, yb, yc, mods, w_out, rwt, layer, has_ctx):
    tile = (lambda g: g) if has_ctx else _lat_tile
    n_tiles = R // TM if has_ctx else BATCH * LAT_TILES
    out_row = lambda w: pl.BlockSpec((TM, w), lambda g: (g, 0))
    return pl.pallas_call(
        functools.partial(_out_kernel, n_resid=len(resid), has_ctx=has_ctx),
        grid=(n_tiles,),
        in_specs=_tile_specs(resid, tile) + _tile_specs(ya, tile) + _tile_specs((yb,), tile) + _tile_specs((yc,), tile) + [
            pl.BlockSpec((None, None, 6, D), lambda g: (layer, _mod_row(tile(g)), 0, 0)),
            pl.BlockSpec((None, D, D), lambda g: (layer, 0, 0), pipeline_mode=pl.Buffered(1)),
            pl.BlockSpec((N_EXP, D), lambda g: (0, 0))],
        out_specs=[out_row(D), out_row(D), pl.BlockSpec((N_EXP, TM), lambda g: (0, g))],
        out_shape=[jax.ShapeDtypeStruct((n_tiles * TM, D), f32), jax.ShapeDtypeStruct((n_tiles * TM, D), f32),
                   jax.ShapeDtypeStruct((N_EXP, n_tiles * TM), f32)],
        scratch_shapes=[pltpu.VMEM((D, D), bf16)],
        compiler_params=_cp(("arbitrary",)),
        name="out_proj",
    )(*resid, *ya, yb, yc, mods, w_out, rwt)


def _topk_kernel(aff_ref, post_ref, tok_ref, gate_ref, pos_sc, *, has_ctx):
    ri = lax.broadcasted_iota(jnp.int32, (TM, TM), 0)
    ci = lax.broadcasted_iota(jnp.int32, (TM, TM), 1)
    before = jnp.where(ri < ci, 1.0, 0.0).astype(bf16)

    def prefix_count(m):
        out = []
        off = jnp.zeros((N_EXP, 1), f32)
        for blk in range(m.shape[1] // TM):
            mb = m[:, blk * TM:(blk + 1) * TM]
            out.append(_dot(mb.astype(bf16), before) + off)
            off = off + jnp.sum(mb, axis=1, keepdims=True)
        return jnp.concatenate(out, axis=1) if len(out) > 1 else out[0]

    def select(a, cap, base):
        capf = float(cap)

        def step(i, thr_bits):
            cand = thr_bits | jnp.left_shift(jnp.int32(1), 30 - i)
            cnt = jnp.sum(jnp.where(a >= pltpu.bitcast(cand, f32), 1.0, 0.0), axis=1, keepdims=True)
            return jnp.where(cnt >= capf, cand, thr_bits)

        thr = pltpu.bitcast(lax.fori_loop(0, 31, step, jnp.zeros((N_EXP, 1), jnp.int32)), f32)
        gt = jnp.where(a > thr, 1.0, 0.0)
        eq = jnp.where(a == thr, 1.0, 0.0)
        need = capf - jnp.sum(gt, axis=1, keepdims=True)
        keep = gt + eq * jnp.where(prefix_count(eq) < need, 1.0, 0.0)
        return jnp.where(keep > 0.5, prefix_count(keep) + float(base), -1.0)

    if has_ctx:
        pos = jnp.concatenate([select(aff_ref[:, 0:N_CTX], CAP_CTX, CAP_LAT), select(aff_ref[:, N_CTX:], CAP_LAT, 0)], axis=1)
    else:
        pos = select(aff_ref[...], CAP_LAT, 0)
    n = pos.shape[1]
    slots = tok_ref.shape[0]
    pos_sc[...] = pos
    post_ref[...] = jnp.concatenate([pos, jnp.full((128 - N_EXP, n), -1.0, f32)], axis=0).T

    slot_id = lax.broadcasted_iota(jnp.int32, (slots, n), 0).astype(f32)
    tok_id = lax.broadcasted_iota(jnp.int32, (slots, n), 1).astype(f32) + (pl.program_id(0) * n).astype(f32)
    lane = lax.broadcasted_iota(jnp.int32, (slots, 128), 1)

    def invert(e, carry):
        tok, gate = carry
        hit = pos_sc[pl.ds(e, 1), :] == slot_id
        t_e = jnp.sum(jnp.where(hit, tok_id, 0.0), axis=1, keepdims=True)
        g_e = jnp.sum(jnp.where(hit, aff_ref[pl.ds(e, 1), :], 0.0), axis=1, keepdims=True)
        return jnp.where(lane == e, t_e, tok), jnp.where(lane == e, g_e, gate)

    tok, gate = lax.fori_loop(0, N_EXP, invert, (jnp.zeros((slots, 128), f32), jnp.zeros((slots, 128), f32)))
    tok_ref[...] = tok
    gate_ref[...] = gate


def _topk(aff, has_ctx):
    n = S if has_ctx else N_LAT
    slots = SLOTS if has_ctx else CAP_LAT
    return pl.pallas_call(
        functools.partial(_topk_kernel, has_ctx=has_ctx),
        grid=(BATCH,),
        in_specs=[pl.BlockSpec((N_EXP, n), lambda b: (0, b))],
        out_specs=[pl.BlockSpec((n, 128), lambda b: (b, 0)), pl.BlockSpec((slots, 128), lambda b: (b, 0)),
                   pl.BlockSpec((slots, 128), lambda b: (b, 0))],
        out_shape=[jax.ShapeDtypeStruct((BATCH * n, 128), f32), jax.ShapeDtypeStruct((BATCH * slots, 128), f32),
                   jax.ShapeDtypeStruct((BATCH * slots, 128), f32)],
        scratch_shapes=[pltpu.VMEM((N_EXP, n), f32)],
        compiler_params=_cp(("arbitrary",)),
        name="route_topk",
    )(aff)


TF = 512
N_UP = EXP_FF // TF
N_DOWN = D // TF
N_STEP = N_UP + N_DOWN
MOE_VMEM_LIMIT = 60 * 1024 * 1024


def _moe_kernel(idx_ref, h_hbm, wg_ref, wu_ref, wd_ref, gate_ref, y_ref, xraw, xb, hm_ref, sem):
    e = pl.program_id(0)
    s = pl.program_id(1)
    rows = xb.shape[0]
    per = rows // N_STEP

    def row_copy(expert, r):
        src = idx_ref[expert * rows + r]
        return pltpu.make_async_copy(h_hbm.at[pl.ds(src, 1), :], xraw.at[pl.ds(r, 1), :], sem.at[0])

    def all_rows():
        return pltpu.make_async_copy(h_hbm.at[pl.ds(0, rows), :], xraw, sem.at[0])

    def prefetch_share():
        for r in range(per):
            row_copy(e + 1, s * per + r).start()

    @pl.when(s == 0)
    def _():
        @pl.when(e == 0)
        def _():
            def first(r, carry):
                row_copy(0, r).start()
                return carry

            lax.fori_loop(0, rows, first, 0)

        all_rows().wait()
        xb[...] = xraw[...].astype(bf16)

    @pl.when(s < N_UP)
    def _():
        prefetch_share()
        x = xb[...]
        a = _dot(x, wg_ref[...].astype(bf16))
        u = _dot(x, wu_ref[...].astype(bf16))
        hm_ref[s] = (_silu(a) * u).astype(bf16)

    @pl.when(s >= N_UP)
    def _():
        prefetch_share()
        wd = wd_ref[...].astype(bf16)
        acc = _dot(hm_ref[0], wd[0:TF, :])
        for c in range(1, N_UP):
            acc += _dot(hm_ref[c], wd[c * TF:(c + 1) * TF, :])
        lane = lax.broadcasted_iota(jnp.int32, (rows, 128), 1)
        g = jnp.sum(jnp.where(lane == e, gate_ref[...], 0.0), axis=1, keepdims=True)
        y_ref[...] = (acc * g).astype(bf16)

    @pl.when((e == N_EXP - 1) & (s == N_STEP - 1))
    def _():
        all_rows().wait()


def _moe_ffn(tok, gate, h2, w_gate, w_up, w_down, layer):
    rows = tok.shape[0]
    idx = tok[:, :N_EXP].T.astype(jnp.int32).reshape(-1)
    idx = jnp.concatenate([idx, jnp.zeros((rows,), jnp.int32)])
    up = lambda s: jnp.minimum(s, N_UP - 1)
    down = lambda s: jnp.maximum(s - N_UP, 0)
    return pl.pallas_call(
        _moe_kernel,
        grid_spec=pltpu.PrefetchScalarGridSpec(
            num_scalar_prefetch=1,
            grid=(N_EXP, N_STEP),
            in_specs=[pl.BlockSpec(memory_space=pl.ANY),
                      pl.BlockSpec((None, None, D, TF), lambda e, s, idx: (layer, e, 0, up(s))),
                      pl.BlockSpec((None, None, D, TF), lambda e, s, idx: (layer, e, 0, up(s))),
                      pl.BlockSpec((None, None, EXP_FF, TF), lambda e, s, idx: (layer, e, 0, down(s))),
                      pl.BlockSpec((rows, 128), lambda e, s, idx: (0, 0))],
            out_specs=pl.BlockSpec((None, rows, TF), lambda e, s, idx: (e, 0, down(s))),
            scratch_shapes=[pltpu.VMEM((rows, D), f32), pltpu.VMEM((rows, D), bf16),
                            pltpu.VMEM((N_UP, rows, TF), bf16), pltpu.SemaphoreType.DMA((1,))]),
        out_shape=jax.ShapeDtypeStruct((N_EXP, rows, D), bf16),
        compiler_params=pltpu.CompilerParams(dimension_semantics=("arbitrary", "arbitrary"),
                                             vmem_limit_bytes=MOE_VMEM_LIMIT, disable_bounds_checks=True),
        name="moe_ffn",
    )(idx, h2, w_gate, w_up, w_down, gate)


TN_C = 1024


def _combine_kernel(post_ref, ys_ref, x_ref, mod_ref, *rest, has_ctx):
    gain_ref, o_ref = rest if len(rest) == 2 else (None, rest[0])
    tn = o_ref.shape[-1]
    t = pl.program_id(2) if has_ctx else pl.program_id(2) + 1
    pb = post_ref[...]

    def scatter(cap, base, ys):
        if cap % 128 == 0:
            slot = lax.broadcasted_iota(jnp.int32, (TM, cap), 1).astype(f32) + float(base)
            onehot = jnp.concatenate(
                [jnp.where(pb[:, e:e + 1] == slot, 1.0, 0.0).astype(bf16) for e in range(N_EXP)], axis=1)
        else:
            pc = pb - float(base)
            pc = jnp.where((pc >= 0.0) & (pc < float(cap)), pc, -1.0).astype(bf16)
            er = lax.broadcasted_iota(jnp.int32, (128, N_EXP * cap), 0)
            ec = lax.broadcasted_iota(jnp.int32, (128, N_EXP * cap), 1) // cap
            rep = jnp.where(er == ec, 1.0, 0.0).astype(bf16)
            slot = (lax.broadcasted_iota(jnp.int32, (TM, N_EXP * cap), 1) % cap).astype(f32)
            onehot = jnp.where(_dot(pc, rep) == slot, 1.0, 0.0).astype(bf16)
        x = x_ref[...] + mod_ref[5:6, :] * _dot(onehot, ys)
        o_ref[...] = x if gain_ref is None else _rms(x) * gain_ref[...]

    if has_ctx:
        @pl.when(t == 0)
        def _():
            scatter(CAP_CTX, CAP_LAT, ys_ref[:, CAP_LAT:SLOTS, :].reshape(N_EXP * CAP_CTX, tn))

    @pl.when(t > 0)
    def _():
        scatter(CAP_LAT, 0, ys_ref[:, 0:CAP_LAT, :].reshape(N_EXP * CAP_LAT, tn))


def _combine(post, ys, xa, mods, layer, has_ctx, final_gain=None):
    final = final_gain is not None
    assert not (final and has_ctx)
    slots = SLOTS if has_ctx else CAP_LAT
    tps = TILES if has_ctx else LAT_TILES
    tn = D if final else TN_C
    tile = lambda b, t: b * tps + t
    mod_row = (lambda b, t: jnp.where(t == 0, BATCH, b)) if has_ctx else (lambda b, t: b)
    extra_in, extra_specs = ((final_gain.reshape(1, D),), [pl.BlockSpec((1, D), lambda b, n, t: (0, 0))]) if final else ((), [])
    out = pl.pallas_call(
        functools.partial(_combine_kernel, has_ctx=has_ctx),
        grid=(BATCH, D // tn, tps),
        in_specs=[pl.BlockSpec((TM, 128), lambda b, n, t: (tile(b, t), 0)),
                  pl.BlockSpec((N_EXP, None, slots, tn), lambda b, n, t: (0, b, 0, n)),
                  pl.BlockSpec((TM, tn), lambda b, n, t: (tile(b, t), n)),
                  pl.BlockSpec((None, None, 6, tn), lambda b, n, t: (layer, mod_row(b, t), 0, n))] + extra_specs,
        out_specs=pl.BlockSpec((TM, tn), lambda b, n, t: (tile(b, t), n)),
        out_shape=jax.ShapeDtypeStruct((BATCH * tps * TM, D), f32),
        compiler_params=_cp(("arbitrary", "arbitrary", "arbitrary")),
        name="moe_combine",
    )(post, ys.reshape(N_EXP, BATCH, slots, D), xa, mods, *extra_in)
    return out.reshape(BATCH, N_LAT, D) if final else out


def _mla_weights(w_uq, w_ukv):
    half = MLA_ROPE // 2
    wq = w_uq.reshape(MLA_QR, MLA_H, MLA_NOPE + MLA_ROPE)
    nope = wq[:, :, :MLA_NOPE]
    rope = wq[:, :, MLA_NOPE:].reshape(MLA_QR, MLA_H, half, 2)
    ev, od = rope[..., 0], rope[..., 1]
    zpad = jnp.zeros((MLA_QR, MLA_H, HP - MLA_NOPE - MLA_ROPE), w_uq.dtype)
    q_main = jnp.concatenate([nope, ev, od, zpad], axis=-1).reshape(MLA_QR, MLA_H * HP)
    q_part = jnp.concatenate([jnp.zeros_like(nope), od, ev, zpad], axis=-1).reshape(MLA_QR, MLA_H * HP)
    wkv = w_ukv.reshape(MLA_KVR, MLA_H, MLA_NOPE + MLA_V)
    k_main = jnp.concatenate([wkv[:, :, :MLA_NOPE], jnp.zeros((MLA_KVR, MLA_H, HP - MLA_NOPE), w_ukv.dtype)], axis=-1)
    v_main = wkv[:, :, MLA_NOPE:]
    return (jnp.concatenate([q_main, q_part], axis=1).astype(bf16),
            jnp.concatenate([k_main.reshape(MLA_KVR, MLA_H * HP), v_main.reshape(MLA_KVR, MLA_H * MLA_V)], axis=1).astype(bf16))


def _rope_key_placement():
    half = MLA_ROPE // 2
    nq = MLA_H * HP
    e2 = np.zeros((128, 2 * nq), np.float32)
    for h in range(MLA_H):
        for i in range(half):
            ev, od = h * HP + MLA_NOPE + i, h * HP + MLA_NOPE + half + i
            e2[2 * i, ev] = e2[2 * i + 1, od] = 1.0
            e2[2 * i + 1, nq + ev] = e2[2 * i, nq + od] = 1.0
    return jnp.asarray(e2, bf16)


def _tables():
    rows = N_LAT // GRID_W
    row = np.repeat(np.arange(rows, dtype=np.float32), GRID_W)
    colp = np.tile(np.arange(GRID_W, dtype=np.float32), rows)
    n_freq = MLA_ROPE // 4
    nf32 = np.float32
    inv = np.power(nf32(ROPE_BASE), -np.arange(n_freq, dtype=nf32) / nf32(n_freq))
    ang = np.concatenate([row[:, None] * inv, colp[:, None] * inv], axis=-1)
    cos_a, sin_a = np.cos(ang), np.sin(ang)
    one = np.ones((N_LAT, MLA_NOPE), nf32)
    zpad = np.zeros((N_LAT, HP - MLA_NOPE - MLA_ROPE), nf32)
    tc_lat = np.concatenate([one, cos_a, cos_a, zpad], axis=1)
    ts_lat = np.concatenate([0 * one, -sin_a, sin_a, zpad], axis=1)
    tc_ctx = np.concatenate([np.ones((N_CTX, MLA_NOPE + MLA_ROPE), nf32), np.zeros((N_CTX, HP - MLA_NOPE - MLA_ROPE), nf32)], axis=1)
    tc = np.concatenate([tc_ctx, tc_lat], axis=0)
    ts = np.concatenate([np.zeros((N_CTX, HP), nf32), ts_lat], axis=0)

    inv_r = nf32(1.0) / np.power(nf32(ROPE_BASE), np.linspace(0.0, 1.0, RET_DK // 2, dtype=nf32))
    ang_r = np.arange(N_LAT, dtype=nf32)[:, None] * inv_r
    cos_r = np.concatenate([np.ones((N_CTX, RET_DK), nf32), np.repeat(np.cos(ang_r), 2, axis=1)], axis=0)
    sin_r = np.concatenate([np.zeros((N_CTX, RET_DK), nf32),
                            np.stack([-np.sin(ang_r), np.sin(ang_r)], axis=-1).reshape(N_LAT, RET_DK)], axis=0)

    def log_decay(direction):
        e = nf32(RET_EXP0 + direction) + nf32(2.0) * np.arange(RET_H, dtype=nf32)
        return np.repeat(np.log1p(-np.exp2(-e)), RET_DK)[None, :]

    steps = np.arange(1, RET_CHUNK + 1, dtype=nf32)[:, None]
    cumf = steps * log_decay(0.0)
    cumb = steps[::-1] * log_decay(1.0)
    return tc, ts, cos_r, sin_r, cumf, cumb


def kernel(x, c, ctx, c_ctx, ada_w, ada_b, w_in, mla_q_norm, mla_w_uq, mla_kv_norm, mla_w_ukv, gla_gate_w2,
           gla_gate_b, w_out, router_w, exp_w_gate, exp_w_up, exp_w_down, final_norm):
    resid = (x, ctx)
    cc = jnp.concatenate([c, c_ctx[None, :], jnp.zeros((8 - BATCH - 1, D), f32)], axis=0)
    mods = _modulation(cc, ada_w, ada_b).reshape(DEPTH, 8, 6, D)
    tc, ts, cos_r, sin_r, cumf, cumb = _tables()
    e2 = _rope_key_placement()
    w_in_bf = w_in.astype(bf16)
    kh = GLA_H * GLA_DK

    for l in range(DEPTH):
        wq2, wkv = _mla_weights(mla_w_uq[l], mla_w_ukv[l])
        w2p = jnp.zeros((128, 2 * kh), f32)
        w2p = w2p.at[64:64 + GLA_RANK, 0:kh].set(gla_gate_w2[l, 0]).at[64 + GLA_RANK:64 + 2 * GLA_RANK, kh:].set(gla_gate_w2[l, 1])
        b2 = gla_gate_b[l].reshape(1, 2 * kh)

        has_ctx = l < DEPTH - 1
        p = _in_proj(resid, mods, w_in_bf, l)
        q, k, vt = _mla_prep(p, tc, ts, mla_q_norm[l].reshape(1, -1), mla_kv_norm[l].reshape(1, -1), wq2, wkv, e2)
        ya = _attention(q, k, vt, has_ctx)
        yb = _gla(p, w2p.astype(bf16), b2)
        yc = _retention(p, cos_r, sin_r, cumf, cumb)
        xa, h2, aff = _out_proj(resid, ya, yb, yc, mods, w_out, router_w[l].T.astype(bf16), l, has_ctx)
        post, tok, gate = _topk(aff, has_ctx)
        ys = _moe_ffn(tok, gate, h2, exp_w_gate, exp_w_up, exp_w_down, l)
        resid = (_combine(post, ys, xa, mods, l, has_ctx, final_gain=None if has_ctx else final_norm),)
    return resid[0]
```

```python
import functools

import numpy as np
import jax
import jax.numpy as jnp
from jax import lax
from jax.experimental import pallas as pl
from jax.experimental.pallas import tpu as pltpu

f32 = jnp.float32
bf16 = jnp.bfloat16

D = 2048
BATCH = 4
N_LAT = 2048
N_CTX = 256
S = N_CTX + N_LAT
R = BATCH * S
DEPTH = 2
GRID_W = 64
EPS = 1e-6
LOG2E = 1.4426950408889634
ROPE_BASE = 10000.0
CHUNK = 64

MLA_H, MLA_QR, MLA_KVR, MLA_NOPE, MLA_ROPE, MLA_V = 8, 512, 256, 128, 64, 128
GLA_H, GLA_DK, GLA_DV, GLA_RANK, GLA_TAU = 4, 64, 128, 16, 16.0
RET_H, RET_DK, RET_DV, RET_EXP0 = 4, 128, 128, 5.0
N_EXP, EXP_FF, EC_CAP = 16, 2048, 2
CAP_LAT = EC_CAP * N_LAT // N_EXP
CAP_CTX = EC_CAP * N_CTX // N_EXP
SLOTS = CAP_LAT + CAP_CTX

TM = 256
TILES = S // TM
LAT_TILES = N_LAT // TM
HP = 256

C_RQ, C_RK, C_RV, C_RG = 0, 512, 1024, 1536
C_GV, C_GOG, C_CQ, C_CKV, C_GQ, C_GK, C_KRZ = 2048, 2560, 3072, 3584, 3840, 4096, 4352
NC = 4608
TN_IN = 1536

VMEM_LIMIT = 56 * 1024 * 1024


def _cp(sem):
    return pltpu.CompilerParams(dimension_semantics=sem, vmem_limit_bytes=VMEM_LIMIT)


def _nt(a, b):
    return lax.dot_general(a, b, (((1,), (1,)), ((), ())), preferred_element_type=f32)


def _tn(a, b):
    return lax.dot_general(a, b, (((0,), (0,)), ((), ())), preferred_element_type=f32)


def _dot(a, b):
    return jnp.dot(a, b, preferred_element_type=f32)


def _rms(x):
    return x * lax.rsqrt(jnp.mean(x * x, axis=-1, keepdims=True) + EPS)


def _silu(x):
    return x * (1.0 / (1.0 + jnp.exp(-x)))


def _mod_row(i):
    return jnp.where(i % TILES == 0, BATCH, i // TILES)


def _lat_first(i):
    return (i // TILES) * TILES + (i % TILES + TILES - 1) % TILES


def _mod_kernel(s_ref, w_ref, b_ref, o_ref):
    s = _silu(s_ref[...]).astype(bf16)
    o_ref[...] = _dot(s, w_ref[...].astype(bf16)) + b_ref[...]


def _modulation(cc, ada_w, ada_b):
    tn = 1024
    return pl.pallas_call(
        _mod_kernel,
        grid=(DEPTH, 6 * D // tn),
        in_specs=[pl.BlockSpec((8, D), lambda l, j: (0, 0)),
                  pl.BlockSpec((None, D, tn), lambda l, j: (l, 0, j)),
                  pl.BlockSpec((None, 1, tn), lambda l, j: (l, 0, j))],
        out_specs=pl.BlockSpec((None, 8, tn), lambda l, j: (l, 0, j)),
        out_shape=jax.ShapeDtypeStruct((DEPTH, 8, 6 * D), f32),
        compiler_params=_cp(("arbitrary", "arbitrary")),
        name="modulation",
    )(cc, ada_w, ada_b.reshape(DEPTH, 1, 6 * D))


def _tile_specs(arrs, tile_of):
    w = arrs[0].shape[-1]
    if arrs[0].ndim == 2:
        return [pl.BlockSpec((TM, w), lambda *g: (tile_of(*g), 0))]
    lat = pl.BlockSpec((None, TM, w), lambda *g: (tile_of(*g) // TILES, jnp.maximum(tile_of(*g) % TILES - 1, 0), 0))
    if len(arrs) == 1:
        return [lat]
    return [lat, pl.BlockSpec((None, TM, w), lambda *g: (tile_of(*g) // TILES, 0, 0))]


def _tile_value(refs, is_ctx, rows=slice(None)):
    if len(refs) == 1:
        return refs[0][rows, :]
    return jnp.where(is_ctx, refs[1][rows, :], refs[0][rows, :])


def _in_pieces():
    o = np.cumsum((0, MLA_QR, MLA_KVR, MLA_ROPE, 256, 256, 512, 32, 512, 512, 512, 512, 512))
    cq, ckv, kr, gq, gk, gv, gz, gog, ret = (int(v) for v in o[:9])
    return ((ret, 2048, C_RQ), (gv, 512, C_GV), (gog, 512, C_GOG), (cq, 512, C_CQ), (ckv, 256, C_CKV),
            (gq, 256, C_GQ), (gk, 256, C_GK), (kr, MLA_ROPE, C_KRZ), (gz, 2 * GLA_RANK, C_KRZ + MLA_ROPE))


IN_COLS = 4448
RB = 256


def _in_kernel(*refs):
    *resid, mod_ref, ws_ref, o_ref, w_ref = refs

    @pl.when(pl.program_id(0) == 0)
    def _():
        def move(i, carry):
            r0 = pl.multiple_of(i * RB, RB)
            for src, width, dst in _in_pieces():
                w_ref[pl.ds(r0, RB), dst:dst + width] = ws_ref[pl.ds(r0, RB), src:src + width]
            end = C_KRZ + MLA_ROPE + 2 * GLA_RANK
            w_ref[pl.ds(r0, RB), end:NC] = jnp.zeros((RB, NC - end), bf16)
            return carry

        lax.fori_loop(0, D // RB, move, 0)

    x = _tile_value(resid, pl.program_id(0) % TILES == 0)
    h = (_rms(x) * (1.0 + mod_ref[1:2, :]) + mod_ref[0:1, :]).astype(bf16)
    for j in range(NC // TN_IN):
        o_ref[:, j * TN_IN:(j + 1) * TN_IN] = _dot(h, w_ref[:, j * TN_IN:(j + 1) * TN_IN]).astype(bf16)


def _in_proj(resid, mods, w_in_bf, layer):
    return pl.pallas_call(
        _in_kernel,
        grid=(R // TM,),
        in_specs=_tile_specs(resid, lambda i: i) + [
            pl.BlockSpec((None, None, 6, D), lambda i: (layer, _mod_row(i), 0, 0)),
            pl.BlockSpec((None, D, IN_COLS), lambda i: (layer, 0, 0), pipeline_mode=pl.Buffered(1))],
        out_specs=pl.BlockSpec((TM, NC), lambda i: (i, 0)),
        out_shape=jax.ShapeDtypeStruct((R, NC), bf16),
        scratch_shapes=[pltpu.VMEM((D, NC), bf16)],
        compiler_params=pltpu.CompilerParams(dimension_semantics=("arbitrary",), vmem_limit_bytes=60 * 1024 * 1024),
        name="in_proj",
    )(*resid, mods, w_in_bf)


def _mla_prep_kernel(cq_ref, ckv_ref, krz_ref, tc_ref, ts_ref, qn_ref, kvn_ref, wq_ref, wkv_ref, e2_ref,
                     q_ref, k_ref, vt_ref):
    tc = tc_ref[...]
    ts = ts_ref[...]
    scale = (MLA_NOPE + MLA_ROPE) ** -0.5 * LOG2E
    hq = (_rms(cq_ref[...].astype(f32)) * qn_ref[...]).astype(bf16)
    q2 = _dot(hq, wq_ref[...])
    hkv = (_rms(ckv_ref[...].astype(f32)) * kvn_ref[...]).astype(bf16)
    kv = _dot(hkv, wkv_ref[...])
    kr2 = _dot(krz_ref[...], e2_ref[...])
    nq = MLA_H * HP
    for h in range(MLA_H):
        sl = slice(h * HP, (h + 1) * HP)
        sl2 = slice(nq + h * HP, nq + (h + 1) * HP)
        q_ref[:, sl] = ((q2[:, sl] * tc + q2[:, sl2] * ts) * scale).astype(bf16)
        k_ref[:, sl] = (kv[:, sl] + kr2[:, sl] * tc + kr2[:, sl2] * ts).astype(bf16)
    vt_ref[...] = kv[:, nq:].T.astype(bf16)


def _mla_prep(p, tc, ts, qn, kvn, wq2, wkv, e2):
    nq = MLA_H * HP
    nv = MLA_H * MLA_V
    const = lambda shape: pl.BlockSpec(shape, lambda i: (0, 0))
    return pl.pallas_call(
        _mla_prep_kernel,
        grid=(R // TM,),
        in_specs=[pl.BlockSpec((TM, MLA_QR), lambda i: (i, C_CQ // MLA_QR)),
                  pl.BlockSpec((TM, MLA_KVR), lambda i: (i, C_CKV // MLA_KVR)),
                  pl.BlockSpec((TM, 128), lambda i: (i, C_KRZ // 128)),
                  pl.BlockSpec((TM, HP), lambda i: (i % TILES, 0)),
                  pl.BlockSpec((TM, HP), lambda i: (i % TILES, 0)),
                  const((1, MLA_QR)), const((1, MLA_KVR)),
                  const((MLA_QR, 2 * nq)), const((MLA_KVR, nq + nv)), const((128, 2 * nq))],
        out_specs=[pl.BlockSpec((TM, nq), lambda i: (_lat_first(i), 0)),
                   pl.BlockSpec((TM, nq), lambda i: (i, 0)),
                   pl.BlockSpec((None, nv, TM), lambda i: (i // TILES, 0, i % TILES))],
        out_shape=[jax.ShapeDtypeStruct((R, nq), bf16), jax.ShapeDtypeStruct((R, nq), bf16),
                   jax.ShapeDtypeStruct((BATCH, nv, S), bf16)],
        compiler_params=_cp(("arbitrary",)),
        name="mla_prep",
    )(p, p, p, tc, ts, qn, kvn, wq2, wkv, e2)


TQ = 1024
ATT_HEADS_PER_STEP = 1


def _attn_body(q_ref, k_ref, vt_ref, o_ref, s_sc, n_chunks, n_heads):
    nq = q_ref.shape[0]
    qs = [q_ref[:, h * HP:(h + 1) * HP] for h in range(n_heads)]

    def scores(h, j, m):
        s = _nt(k_ref[j * TM:(j + 1) * TM, h * HP:(h + 1) * HP], qs[h])
        s_sc[h, j] = s
        cm = jnp.max(s, axis=0, keepdims=True)
        return cm if m is None else jnp.maximum(m, cm)

    m = [None] * n_heads
    for j in range(n_chunks):
        m[0] = scores(0, j, m[0])
    for h in range(n_heads):
        l = jnp.zeros((1, nq), f32)
        acc = jnp.zeros((MLA_V, nq), f32)
        for j in range(n_chunks):
            p = jnp.exp2(s_sc[h, j] - m[h])
            l = l + jnp.sum(p, axis=0, keepdims=True)
            acc = acc + _dot(vt_ref[h * MLA_V:(h + 1) * MLA_V, j * TM:(j + 1) * TM], p.astype(bf16))
            if h + 1 < n_heads:
                m[h + 1] = scores(h + 1, j, m[h + 1])
        o_ref[:, h * MLA_V:(h + 1) * MLA_V] = (acc * (1.0 / l)).T.astype(bf16)


def _attention(q, k, vt, has_ctx):
    nq = MLA_H * HP
    nv = MLA_H * MLA_V
    q3 = q.reshape(BATCH, S, nq)
    k3 = k.reshape(BATCH, S, nq)
    hs = ATT_HEADS_PER_STEP
    ya = pl.pallas_call(
        functools.partial(_attn_body, n_chunks=S // TM, n_heads=hs),
        grid=(BATCH, MLA_H // hs, N_LAT // TQ),
        in_specs=[pl.BlockSpec((None, TQ, hs * HP), lambda b, h, t: (b, t, h)),
                  pl.BlockSpec((None, S, hs * HP), lambda b, h, t: (b, 0, h)),
                  pl.BlockSpec((None, hs * MLA_V, S), lambda b, h, t: (b, h, 0))],
        out_specs=pl.BlockSpec((None, TQ, hs * MLA_V), lambda b, h, t: (b, t, h)),
        out_shape=jax.ShapeDtypeStruct((BATCH, N_LAT, nv), bf16),
        scratch_shapes=[pltpu.VMEM((hs, S // TM, TM, TQ), f32)],
        compiler_params=_cp(("arbitrary", "arbitrary", "arbitrary")),
        name="mla_attention",
    )(q3, k3, vt)
    if not has_ctx:
        return (ya,)
    ya_ctx = pl.pallas_call(
        functools.partial(_attn_body, n_chunks=1, n_heads=1),
        grid=(BATCH, MLA_H),
        in_specs=[pl.BlockSpec((None, N_CTX, HP), lambda b, h: (b, N_LAT // N_CTX, h)),
                  pl.BlockSpec((None, N_CTX, HP), lambda b, h: (b, 0, h)),
                  pl.BlockSpec((None, MLA_V, N_CTX), lambda b, h: (b, h, 0))],
        out_specs=pl.BlockSpec((None, N_CTX, MLA_V), lambda b, h: (b, 0, h)),
        out_shape=jax.ShapeDtypeStruct((BATCH, N_CTX, nv), bf16),
        scratch_shapes=[pltpu.VMEM((1, 1, TM, N_CTX), f32)],
        compiler_params=_cp(("arbitrary", "arbitrary")),
        name="mla_attention_ctx",
    )(q3, k3, vt)
    return (ya, ya_ctx)


RET_CHUNK = 128
SCAN_H = 4
SCAN_V = 128
SCAN_W = SCAN_H * SCAN_V


def _scan_consts(kh, chunk):
    dk = kh // SCAN_H
    row = lax.broadcasted_iota(jnp.int32, (chunk, SCAN_H * chunk), 0)
    col = lax.broadcasted_iota(jnp.int32, (chunk, SCAN_H * chunk), 1) % chunk
    incl = row >= col
    strict = col > row
    krow = lax.broadcasted_iota(jnp.int32, (SCAN_H * chunk, kh), 0) // chunk
    kcol = lax.broadcasted_iota(jnp.int32, (SCAN_H * chunk, kh), 1) // dk
    kmask = krow == kcol
    vrow = lax.broadcasted_iota(jnp.int32, (SCAN_H * chunk, SCAN_W), 0) // chunk
    vcol = lax.broadcasted_iota(jnp.int32, (SCAN_H * chunk, SCAN_W), 1) // SCAN_V
    vmask = vrow == vcol
    srow = lax.broadcasted_iota(jnp.int32, (SCAN_W, kh), 0) // SCAN_V
    scol = lax.broadcasted_iota(jnp.int32, (SCAN_W, kh), 1) // dk
    smask = srow == scol
    return incl, strict, kmask, vmask, smask


def _chunk_step(q, k, v, cum, cend, st_ref, amask, kmask, vmask, smask):
    qd = (q * jnp.exp(cum)).astype(bf16)
    ki = k * jnp.exp(-cum)
    kend = (k * jnp.exp(cend - cum)).astype(bf16)
    dec = jnp.exp(cend)
    kst = jnp.where(kmask, jnp.concatenate([ki] * SCAN_H, axis=0), 0.0).astype(bf16)
    att = jnp.where(amask, _nt(qd, kst), 0.0).astype(bf16)
    vbd = jnp.where(vmask, jnp.concatenate([v] * SCAN_H, axis=0), jnp.zeros((), bf16))
    st = st_ref[...]
    o = _dot(att, vbd) + _nt(qd, st.astype(bf16))
    st_ref[...] = st * dec + jnp.where(smask, _tn(v, kend), 0.0)
    return o


def _bwd_chunk(i, chunk):
    n_ctx, n_all = N_CTX // chunk, S // chunk
    return jnp.where(i < n_ctx, n_ctx - 1 - i, n_all + n_ctx - 1 - i)


def _scan_finish(of_sc, ob_sc, g_ref, y_ref):
    def fin(i, carry):
        r0 = pl.multiple_of(i * TM, TM)
        o = of_sc[pl.ds(r0, TM), :] + ob_sc[pl.ds(r0, TM), :]
        g = g_ref[pl.ds(r0, TM), :].astype(f32)
        for h in range(SCAN_H):
            sl = slice(h * SCAN_V, (h + 1) * SCAN_V)
            y_ref[pl.ds(r0, TM), sl] = (_rms(o[:, sl]) * _silu(g[:, sl])).astype(bf16)
        return carry

    lax.fori_loop(0, TILES, fin, 0)


def _gla_kernel(q_ref, k_ref, v_ref, krz_ref, og_ref, w2_ref, b2_ref, y_ref, cum_sc, of_sc, ob_sc, stf_sc, stb_sc):
    kh = GLA_H * GLA_DK
    ri = lax.broadcasted_iota(jnp.int32, (TM, TM), 0)
    ci = lax.broadcasted_iota(jnp.int32, (TM, TM), 1)
    same = (ri // CHUNK) == (ci // CHUNK)
    pre = jnp.where(same & (ci <= ri), 1.0, 0.0).astype(bf16)
    suf = jnp.where(same & (ci >= ri), 1.0, 0.0).astype(bf16)

    def exact_sum(m, x):
        hi = x.astype(bf16)
        r1 = x - hi.astype(f32)
        mid = r1.astype(bf16)
        lo = (r1 - mid.astype(f32)).astype(bf16)
        return _dot(m, hi) + _dot(m, mid) + _dot(m, lo)

    def gates(i, carry):
        r0 = pl.multiple_of(i * TM, TM)
        lg = _dot(krz_ref[pl.ds(r0, TM), :], w2_ref[...]) + b2_ref[...]
        la = (jnp.minimum(lg, 0.0) - jnp.log1p(jnp.exp(-jnp.abs(lg)))) * (1.0 / GLA_TAU)
        cum_sc[pl.ds(r0, TM), 0:kh] = exact_sum(pre, la[:, 0:kh])
        cum_sc[pl.ds(r0, TM), kh:2 * kh] = exact_sum(suf, la[:, kh:2 * kh])
        return carry

    lax.fori_loop(0, TILES, gates, 0, unroll=3)

    incl, strict, kmask, vmask, smask = _scan_consts(kh, CHUNK)
    stf_sc[...] = jnp.zeros_like(stf_sc)
    stb_sc[...] = jnp.zeros_like(stb_sc)
    qscale = GLA_DK ** -0.5

    def body(i, carry):
        rf = pl.multiple_of(i * CHUNK, CHUNK)
        cum = cum_sc[pl.ds(rf, CHUNK), 0:kh]
        of_sc[pl.ds(rf, CHUNK), :] = _chunk_step(
            q_ref[pl.ds(rf, CHUNK), :].astype(f32) * qscale, k_ref[pl.ds(rf, CHUNK), :].astype(f32),
            v_ref[pl.ds(rf, CHUNK), :], cum, cum[CHUNK - 1:CHUNK, :], stf_sc, incl, kmask, vmask, smask)
        rb = pl.multiple_of(_bwd_chunk(i, CHUNK) * CHUNK, CHUNK)
        rc = cum_sc[pl.ds(rb, CHUNK), kh:2 * kh]
        ob_sc[pl.ds(rb, CHUNK), :] = _chunk_step(
            q_ref[pl.ds(rb, CHUNK), :].astype(f32) * qscale, k_ref[pl.ds(rb, CHUNK), :].astype(f32),
            v_ref[pl.ds(rb, CHUNK), :], rc, rc[0:1, :], stb_sc, strict, kmask, vmask, smask)
        return carry

    lax.fori_loop(0, S // CHUNK, body, 0, unroll=6)
    _scan_finish(of_sc, ob_sc, og_ref, y_ref)


def _gla(p, w2p, b2):
    kh = GLA_H * GLA_DK
    p3 = p.reshape(BATCH, S, NC)
    col = lambda w, c: pl.BlockSpec((None, S, w), lambda b: (b, 0, c // w))
    return pl.pallas_call(
        _gla_kernel,
        grid=(BATCH,),
        in_specs=[col(kh, C_GQ), col(kh, C_GK), col(SCAN_W, C_GV), col(128, C_KRZ), col(SCAN_W, C_GOG),
                  pl.BlockSpec((128, 2 * kh), lambda b: (0, 0)), pl.BlockSpec((1, 2 * kh), lambda b: (0, 0))],
        out_specs=pl.BlockSpec((None, S, SCAN_W), lambda b: (b, 0, 0)),
        out_shape=jax.ShapeDtypeStruct((BATCH, S, SCAN_W), bf16),
        scratch_shapes=[pltpu.VMEM((S, 2 * kh), f32), pltpu.VMEM((S, SCAN_W), f32), pltpu.VMEM((S, SCAN_W), f32),
                        pltpu.VMEM((SCAN_W, kh), f32), pltpu.VMEM((SCAN_W, kh), f32)],
        compiler_params=_cp(("arbitrary",)),
        name="gla",
    )(p3, p3, p3, p3, p3, w2p, b2).reshape(R, SCAN_W)


def _ret_kernel(q_ref, k_ref, v_ref, g_ref, cos_ref, sin_ref, cumf_ref, cumb_ref, y_ref, of_sc, ob_sc, stf_sc, stb_sc):
    kh = RET_H * RET_DK
    incl, strict, kmask, vmask, smask = _scan_consts(kh, RET_CHUNK)
    stf_sc[...] = jnp.zeros_like(stf_sc)
    stb_sc[...] = jnp.zeros_like(stb_sc)
    kscale = RET_DK ** -0.5
    cumf = cumf_ref[...]
    cumb = cumb_ref[...]

    even = lax.broadcasted_iota(jnp.int32, (RET_CHUNK, RET_DK), 1) % 2 == 0

    def rotate(x, r0):
        cos = jnp.concatenate([cos_ref[pl.ds(r0, RET_CHUNK), :]] * RET_H, axis=1)
        sin = jnp.concatenate([sin_ref[pl.ds(r0, RET_CHUNK), :]] * RET_H, axis=1)
        parts = []
        for h in range(RET_H):
            xh = x[:, h * RET_DK:(h + 1) * RET_DK]
            parts.append(jnp.where(even, pltpu.roll(xh, RET_DK - 1, axis=1), pltpu.roll(xh, 1, axis=1)))
        return x * cos + jnp.concatenate(parts, axis=1) * sin

    def body(i, carry):
        rf = pl.multiple_of(i * RET_CHUNK, RET_CHUNK)
        of_sc[pl.ds(rf, RET_CHUNK), :] = _chunk_step(
            rotate(q_ref[pl.ds(rf, RET_CHUNK), :].astype(f32), rf),
            rotate(k_ref[pl.ds(rf, RET_CHUNK), :].astype(f32), rf) * kscale,
            v_ref[pl.ds(rf, RET_CHUNK), :], cumf, cumf[RET_CHUNK - 1:RET_CHUNK, :], stf_sc, incl, kmask, vmask, smask)
        rb = pl.multiple_of(_bwd_chunk(i, RET_CHUNK) * RET_CHUNK, RET_CHUNK)
        ob_sc[pl.ds(rb, RET_CHUNK), :] = _chunk_step(
            rotate(q_ref[pl.ds(rb, RET_CHUNK), :].astype(f32), rb),
            rotate(k_ref[pl.ds(rb, RET_CHUNK), :].astype(f32), rb) * kscale,
            v_ref[pl.ds(rb, RET_CHUNK), :], cumb, cumb[0:1, :], stb_sc, strict, kmask, vmask, smask)
        return carry

    lax.fori_loop(0, S // RET_CHUNK, body, 0, unroll=3)
    _scan_finish(of_sc, ob_sc, g_ref, y_ref)


def _retention(p, cos_r, sin_r, cumf, cumb):
    kh = RET_H * RET_DK
    p3 = p.reshape(BATCH, S, NC)
    col = lambda w, c: pl.BlockSpec((None, S, w), lambda b: (b, 0, c // w))
    const = lambda shape: pl.BlockSpec(shape, lambda b: (0, 0))
    return pl.pallas_call(
        _ret_kernel,
        grid=(BATCH,),
        in_specs=[col(kh, C_RQ), col(kh, C_RK), col(SCAN_W, C_RV), col(SCAN_W, C_RG),
                  const((S, RET_DK)), const((S, RET_DK)), const((RET_CHUNK, kh)), const((RET_CHUNK, kh))],
        out_specs=pl.BlockSpec((None, S, SCAN_W), lambda b: (b, 0, 0)),
        out_shape=jax.ShapeDtypeStruct((BATCH, S, SCAN_W), bf16),
        scratch_shapes=[pltpu.VMEM((S, SCAN_W), f32), pltpu.VMEM((S, SCAN_W), f32),
                        pltpu.VMEM((SCAN_W, kh), f32), pltpu.VMEM((SCAN_W, kh), f32)],
        compiler_params=_cp(("arbitrary",)),
        name="retention",
    )(p3, p3, p3, p3, cos_r, sin_r, cumf, cumb).reshape(R, SCAN_W)


def _lat_tile(g):
    return (g // LAT_TILES) * TILES + 1 + g % LAT_TILES


def _out_kernel(*refs, n_resid, has_ctx):
    resid, refs = refs[:n_resid], refs[n_resid:]
    *ya, yb_ref, yc_ref, mod_ref, wf_ref, rwt_ref, xo_ref, h2_ref, aff_ref, w_ref = refs
    is_ctx = (pl.program_id(0) % TILES == 0) if has_ctx else False

    @pl.when(pl.program_id(0) == 0)
    def _():
        w_ref[...] = wf_ref[...].astype(bf16)

    na = MLA_H * MLA_V
    half = TM // 2
    for r in range(2):
        rows = slice(r * half, (r + 1) * half)
        acc = _dot(_tile_value(ya, is_ctx, rows), w_ref[0:na, :])
        acc += _dot(yb_ref[rows, :], w_ref[na:na + SCAN_W, :])
        acc += _dot(yc_ref[rows, :], w_ref[na + SCAN_W:, :])
        x = _tile_value(resid, is_ctx, rows) + mod_ref[2:3, :] * acc
        xo_ref[rows, :] = x
        h = _rms(x) * (1.0 + mod_ref[4:5, :]) + mod_ref[3:4, :]
        h2_ref[rows, :] = h
        hb = h.astype(bf16)
        lg = _nt(rwt_ref[...], hb)
        e = jnp.exp(lg - jnp.max(lg, axis=0, keepdims=True))
        aff_ref[:, rows] = e / jnp.sum(e, axis=0, keepdims=True)


def _out_proj(resid, ya, yb, yc, mods, w_out, rwt, layer, has_ctx):
    tile = (lambda g: g) if has_ctx else _lat_tile
    n_tiles = R // TM if has_ctx else BATCH * LAT_TILES
    out_row = lambda w: pl.BlockSpec((TM, w), lambda g: (g, 0))
    return pl.pallas_call(
        functools.partial(_out_kernel, n_resid=len(resid), has_ctx=has_ctx),
        grid=(n_tiles,),
        in_specs=_tile_specs(resid, tile) + _tile_specs(ya, tile) + _tile_specs((yb,), tile) + _tile_specs((yc,), tile) + [
            pl.BlockSpec((None, None, 6, D), lambda g: (layer, _mod_row(tile(g)), 0, 0)),
            pl.BlockSpec((None, D, D), lambda g: (layer, 0, 0), pipeline_mode=pl.Buffered(1)),
            pl.BlockSpec((N_EXP, D), lambda g: (0, 0))],
        out_specs=[out_row(D), out_row(D), pl.BlockSpec((N_EXP, TM), lambda g: (0, g))],
        out_shape=[jax.ShapeDtypeStruct((n_tiles * TM, D), f32), jax.ShapeDtypeStruct((n_tiles * TM, D), f32),
                   jax.ShapeDtypeStruct((N_EXP, n_tiles * TM), f32)],
        scratch_shapes=[pltpu.VMEM((D, D), bf16)],
        compiler_params=_cp(("arbitrary",)),
        name="out_proj",
    )(*resid, *ya, yb, yc, mods, w_out, rwt)


def _topk_kernel(aff_ref, post_ref, tok_ref, gate_ref, pos_sc, *, has_ctx):
    ri = lax.broadcasted_iota(jnp.int32, (TM, TM), 0)
    ci = lax.broadcasted_iota(jnp.int32, (TM, TM), 1)
    before = jnp.where(ri < ci, 1.0, 0.0).astype(bf16)

    def prefix_count(m):
        out = []
        off = jnp.zeros((N_EXP, 1), f32)
        for blk in range(m.shape[1] // TM):
            mb = m[:, blk * TM:(blk + 1) * TM]
            out.append(_dot(mb.astype(bf16), before) + off)
            off = off + jnp.sum(mb, axis=1, keepdims=True)
        return jnp.concatenate(out, axis=1) if len(out) > 1 else out[0]

    def select(a, cap, base):
        capf = float(cap)

        def step(i, thr_bits):
            cand = thr_bits | jnp.left_shift(jnp.int32(1), 30 - i)
            cnt = jnp.sum(jnp.where(a >= pltpu.bitcast(cand, f32), 1.0, 0.0), axis=1, keepdims=True)
            return jnp.where(cnt >= capf, cand, thr_bits)

        thr = pltpu.bitcast(lax.fori_loop(0, 31, step, jnp.zeros((N_EXP, 1), jnp.int32)), f32)
        gt = jnp.where(a > thr, 1.0, 0.0)
        eq = jnp.where(a == thr, 1.0, 0.0)
        need = capf - jnp.sum(gt, axis=1, keepdims=True)
        keep = gt + eq * jnp.where(prefix_count(eq) < need, 1.0, 0.0)
        return jnp.where(keep > 0.5, prefix_count(keep) + float(base), -1.0)

    if has_ctx:
        pos = jnp.concatenate([select(aff_ref[:, 0:N_CTX], CAP_CTX, CAP_LAT), select(aff_ref[:, N_CTX:], CAP_LAT, 0)], axis=1)
    else:
        pos = select(aff_ref[...], CAP_LAT, 0)
    n = pos.shape[1]
    slots = tok_ref.shape[0]
    pos_sc[...] = pos
    post_ref[...] = jnp.concatenate([pos, jnp.full((128 - N_EXP, n), -1.0, f32)], axis=0).T

    slot_id = lax.broadcasted_iota(jnp.int32, (slots, n), 0).astype(f32)
    tok_id = lax.broadcasted_iota(jnp.int32, (slots, n), 1).astype(f32) + (pl.program_id(0) * n).astype(f32)
    lane = lax.broadcasted_iota(jnp.int32, (slots, 128), 1)

    def invert(e, carry):
        tok, gate = carry
        hit = pos_sc[pl.ds(e, 1), :] == slot_id
        t_e = jnp.sum(jnp.where(hit, tok_id, 0.0), axis=1, keepdims=True)
        g_e = jnp.sum(jnp.where(hit, aff_ref[pl.ds(e, 1), :], 0.0), axis=1, keepdims=True)
        return jnp.where(lane == e, t_e, tok), jnp.where(lane == e, g_e, gate)

    tok, gate = lax.fori_loop(0, N_EXP, invert, (jnp.zeros((slots, 128), f32), jnp.zeros((slots, 128), f32)))
    tok_ref[...] = tok
    gate_ref[...] = gate


def _topk(aff, has_ctx):
    n = S if has_ctx else N_LAT
    slots = SLOTS if has_ctx else CAP_LAT
    return pl.pallas_call(
        functools.partial(_topk_kernel, has_ctx=has_ctx),
        grid=(BATCH,),
        in_specs=[pl.BlockSpec((N_EXP, n), lambda b: (0, b))],
        out_specs=[pl.BlockSpec((n, 128), lambda b: (b, 0)), pl.BlockSpec((slots, 128), lambda b: (b, 0)),
                   pl.BlockSpec((slots, 128), lambda b: (b, 0))],
        out_shape=[jax.ShapeDtypeStruct((BATCH * n, 128), f32), jax.ShapeDtypeStruct((BATCH * slots, 128), f32),
                   jax.ShapeDtypeStruct((BATCH * slots, 128), f32)],
        scratch_shapes=[pltpu.VMEM((N_EXP, n), f32)],
        compiler_params=_cp(("arbitrary",)),
        name="route_topk",
    )(aff)


TF = 512
N_UP = EXP_FF // TF
N_DOWN = D // TF
N_STEP = N_UP + N_DOWN
MOE_VMEM_LIMIT = 60 * 1024 * 1024


def _moe_kernel(idx_ref, h_hbm, wg_ref, wu_ref, wd_ref, gate_ref, y_ref, xraw, xb, hm_ref, sem):
    e = pl.program_id(0)
    s = pl.program_id(1)
    rows = xb.shape[0]
    per = rows // N_STEP

    def row_copy(expert, r):
        src = idx_ref[expert * rows + r]
        return pltpu.make_async_copy(h_hbm.at[pl.ds(src, 1), :], xraw.at[pl.ds(r, 1), :], sem.at[0])

    def all_rows():
        return pltpu.make_async_copy(h_hbm.at[pl.ds(0, rows), :], xraw, sem.at[0])

    def prefetch_share():
        for r in range(per):
            row_copy(e + 1, s * per + r).start()

    @pl.when(s == 0)
    def _():
        @pl.when(e == 0)
        def _():
            def first(r, carry):
                row_copy(0, r).start()
                return carry

            lax.fori_loop(0, rows, first, 0)

        all_rows().wait()
        xb[...] = xraw[...].astype(bf16)

    @pl.when(s < N_UP)
    def _():
        prefetch_share()
        x = xb[...]
        a = _dot(x, wg_ref[...].astype(bf16))
        u = _dot(x, wu_ref[...].astype(bf16))
        hm_ref[s] = (_silu(a) * u).astype(bf16)

    @pl.when(s >= N_UP)
    def _():
        prefetch_share()
        wd = wd_ref[...].astype(bf16)
        acc = _dot(hm_ref[0], wd[0:TF, :])
        for c in range(1, N_UP):
            acc += _dot(hm_ref[c], wd[c * TF:(c + 1) * TF, :])
        lane = lax.broadcasted_iota(jnp.int32, (rows, 128), 1)
        g = jnp.sum(jnp.where(lane == e, gate_ref[...], 0.0), axis=1, keepdims=True)
        y_ref[...] = (acc * g).astype(bf16)

    @pl.when((e == N_EXP - 1) & (s == N_STEP - 1))
    def _():
        all_rows().wait()


def _moe_ffn(tok, gate, h2, w_gate, w_up, w_down, layer):
    rows = tok.shape[0]
    idx = tok[:, :N_EXP].T.astype(jnp.int32).reshape(-1)
    idx = jnp.concatenate([idx, jnp.zeros((rows,), jnp.int32)])
    up = lambda s: jnp.minimum(s, N_UP - 1)
    down = lambda s: jnp.maximum(s - N_UP, 0)
    return pl.pallas_call(
        _moe_kernel,
        grid_spec=pltpu.PrefetchScalarGridSpec(
            num_scalar_prefetch=1,
            grid=(N_EXP, N_STEP),
            in_specs=[pl.BlockSpec(memory_space=pl.ANY),
                      pl.BlockSpec((None, None, D, TF), lambda e, s, idx: (layer, e, 0, up(s))),
                      pl.BlockSpec((None, None, D, TF), lambda e, s, idx: (layer, e, 0, up(s))),
                      pl.BlockSpec((None, None, EXP_FF, TF), lambda e, s, idx: (layer, e, 0, down(s))),
                      pl.BlockSpec((rows, 128), lambda e, s, idx: (0, 0))],
            out_specs=pl.BlockSpec((None, rows, TF), lambda e, s, idx: (e, 0, down(s))),
            scratch_shapes=[pltpu.VMEM((rows, D), f32), pltpu.VMEM((rows, D), bf16),
                            pltpu.VMEM((N_UP, rows, TF), bf16), pltpu.SemaphoreType.DMA((1,))]),
        out_shape=jax.ShapeDtypeStruct((N_EXP, rows, D), bf16),
        compiler_params=pltpu.CompilerParams(dimension_semantics=("arbitrary", "arbitrary"),
                                             vmem_limit_bytes=MOE_VMEM_LIMIT, disable_bounds_checks=True),
        name="moe_ffn",
    )(idx, h2, w_gate, w_up, w_down, gate)


TN_C = 1024


def _combine_kernel(post_ref, ys_ref, x_ref, mod_ref, *rest, has_ctx):
    gain_ref, o_ref = rest if len(rest) == 2 else (None, rest[0])
    tn = o_ref.shape[-1]
    t = pl.program_id(2) if has_ctx else pl.program_id(2) + 1
    pb = post_ref[...]

    def scatter(cap, base, ys):
        if cap % 128 == 0:
            slot = lax.broadcasted_iota(jnp.int32, (TM, cap), 1).astype(f32) + float(base)
            onehot = jnp.concatenate(
                [jnp.where(pb[:, e:e + 1] == slot, 1.0, 0.0).astype(bf16) for e in range(N_EXP)], axis=1)
        else:
            pc = pb - float(base)
            pc = jnp.where((pc >= 0.0) & (pc < float(cap)), pc, -1.0).astype(bf16)
            er = lax.broadcasted_iota(jnp.int32, (128, N_EXP * cap), 0)
            ec = lax.broadcasted_iota(jnp.int32, (128, N_EXP * cap), 1) // cap
            rep = jnp.where(er == ec, 1.0, 0.0).astype(bf16)
            slot = (lax.broadcasted_iota(jnp.int32, (TM, N_EXP * cap), 1) % cap).astype(f32)
            onehot = jnp.where(_dot(pc, rep) == slot, 1.0, 0.0).astype(bf16)
        x = x_ref[...] + mod_ref[5:6, :] * _dot(onehot, ys)
        o_ref[...] = x if gain_ref is None else _rms(x) * gain_ref[...]

    if has_ctx:
        @pl.when(t == 0)
        def _():
            scatter(CAP_CTX, CAP_LAT, ys_ref[:, CAP_LAT:SLOTS, :].reshape(N_EXP * CAP_CTX, tn))

    @pl.when(t > 0)
    def _():
        scatter(CAP_LAT, 0, ys_ref[:, 0:CAP_LAT, :].reshape(N_EXP * CAP_LAT, tn))


def _combine(post, ys, xa, mods, layer, has_ctx, final_gain=None):
    final = final_gain is not None
    assert not (final and has_ctx)
    slots = SLOTS if has_ctx else CAP_LAT
    tps = TILES if has_ctx else LAT_TILES
    tn = D if final else TN_C
    tile = lambda b, t: b * tps + t
    mod_row = (lambda b, t: jnp.where(t == 0, BATCH, b)) if has_ctx else (lambda b, t: b)
    extra_in, extra_specs = ((final_gain.reshape(1, D),), [pl.BlockSpec((1, D), lambda b, n, t: (0, 0))]) if final else ((), [])
    out = pl.pallas_call(
        functools.partial(_combine_kernel, has_ctx=has_ctx),
        grid=(BATCH, D // tn, tps),
        in_specs=[pl.BlockSpec((TM, 128), lambda b, n, t: (tile(b, t), 0)),
                  pl.BlockSpec((N_EXP, None, slots, tn), lambda b, n, t: (0, b, 0, n)),
                  pl.BlockSpec((TM, tn), lambda b, n, t: (tile(b, t), n)),
                  pl.BlockSpec((None, None, 6, tn), lambda b, n, t: (layer, mod_row(b, t), 0, n))] + extra_specs,
        out_specs=pl.BlockSpec((TM, tn), lambda b, n, t: (tile(b, t), n)),
        out_shape=jax.ShapeDtypeStruct((BATCH * tps * TM, D), f32),
        compiler_params=_cp(("arbitrary", "arbitrary", "arbitrary")),
        name="moe_combine",
    )(post, ys.reshape(N_EXP, BATCH, slots, D), xa, mods, *extra_in)
    return out.reshape(BATCH, N_LAT, D) if final else out


def _mla_weights(w_uq, w_ukv):
    half = MLA_ROPE // 2
    wq = w_uq.reshape(MLA_QR, MLA_H, MLA_NOPE + MLA_ROPE)
    nope = wq[:, :, :MLA_NOPE]
    rope = wq[:, :, MLA_NOPE:].reshape(MLA_QR, MLA_H, half, 2)
    ev, od = rope[..., 0], rope[..., 1]
    zpad = jnp.zeros((MLA_QR, MLA_H, HP - MLA_NOPE - MLA_ROPE), w_uq.dtype)
    q_main = jnp.concatenate([nope, ev, od, zpad], axis=-1).reshape(MLA_QR, MLA_H * HP)
    q_part = jnp.concatenate([jnp.zeros_like(nope), od, ev, zpad], axis=-1).reshape(MLA_QR, MLA_H * HP)
    wkv = w_ukv.reshape(MLA_KVR, MLA_H, MLA_NOPE + MLA_V)
    k_main = jnp.concatenate([wkv[:, :, :MLA_NOPE], jnp.zeros((MLA_KVR, MLA_H, HP - MLA_NOPE), w_ukv.dtype)], axis=-1)
    v_main = wkv[:, :, MLA_NOPE:]
    return (jnp.concatenate([q_main, q_part], axis=1).astype(bf16),
            jnp.concatenate([k_main.reshape(MLA_KVR, MLA_H * HP), v_main.reshape(MLA_KVR, MLA_H * MLA_V)], axis=1).astype(bf16))


def _rope_key_placement():
    half = MLA_ROPE // 2
    nq = MLA_H * HP
    e2 = np.zeros((128, 2 * nq), np.float32)
    for h in range(MLA_H):
        for i in range(half):
            ev, od = h * HP + MLA_NOPE + i, h * HP + MLA_NOPE + half + i
            e2[2 * i, ev] = e2[2 * i + 1, od] = 1.0
            e2[2 * i + 1, nq + ev] = e2[2 * i, nq + od] = 1.0
    return jnp.asarray(e2, bf16)


def _tables():
    rows = N_LAT // GRID_W
    row = np.repeat(np.arange(rows, dtype=np.float32), GRID_W)
    colp = np.tile(np.arange(GRID_W, dtype=np.float32), rows)
    n_freq = MLA_ROPE // 4
    nf32 = np.float32
    inv = np.power(nf32(ROPE_BASE), -np.arange(n_freq, dtype=nf32) / nf32(n_freq))
    ang = np.concatenate([row[:, None] * inv, colp[:, None] * inv], axis=-1)
    cos_a, sin_a = np.cos(ang), np.sin(ang)
    one = np.ones((N_LAT, MLA_NOPE), nf32)
    zpad = np.zeros((N_LAT, HP - MLA_NOPE - MLA_ROPE), nf32)
    tc_lat = np.concatenate([one, cos_a, cos_a, zpad], axis=1)
    ts_lat = np.concatenate([0 * one, -sin_a, sin_a, zpad], axis=1)
    tc_ctx = np.concatenate([np.ones((N_CTX, MLA_NOPE + MLA_ROPE), nf32), np.zeros((N_CTX, HP - MLA_NOPE - MLA_ROPE), nf32)], axis=1)
    tc = np.concatenate([tc_ctx, tc_lat], axis=0)
    ts = np.concatenate([np.zeros((N_CTX, HP), nf32), ts_lat], axis=0)

    inv_r = nf32(1.0) / np.power(nf32(ROPE_BASE), np.linspace(0.0, 1.0, RET_DK // 2, dtype=nf32))
    ang_r = np.arange(N_LAT, dtype=nf32)[:, None] * inv_r
    cos_r = np.concatenate([np.ones((N_CTX, RET_DK), nf32), np.repeat(np.cos(ang_r), 2, axis=1)], axis=0)
    sin_r = np.concatenate([np.zeros((N_CTX, RET_DK), nf32),
                            np.stack([-np.sin(ang_r), np.sin(ang_r)], axis=-1).reshape(N_LAT, RET_DK)], axis=0)

    def log_decay(direction):
        e = nf32(RET_EXP0 + direction) + nf32(2.0) * np.arange(RET_H, dtype=nf32)
        return np.repeat(np.log1p(-np.exp2(-e)), RET_DK)[None, :]

    steps = np.arange(1, RET_CHUNK + 1, dtype=nf32)[:, None]
    cumf = steps * log_decay(0.0)
    cumb = steps[::-1] * log_decay(1.0)
    return tc, ts, cos_r, sin_r, cumf, cumb


def kernel(x, c, ctx, c_ctx, ada_w, ada_b, w_in, mla_q_norm, mla_w_uq, mla_kv_norm, mla_w_ukv, gla_gate_w2,
           gla_gate_b, w_out, router_w, exp_w_gate, exp_w_up, exp_w_down, final_norm):
    resid = (x, ctx)
    cc = jnp.concatenate([c, c_ctx[None, :], jnp.zeros((8 - BATCH - 1, D), f32)], axis=0)
    mods = _modulation(cc, ada_w, ada_b).reshape(DEPTH, 8, 6, D)
    tc, ts, cos_r, sin_r, cumf, cumb = _tables()
    e2 = _rope_key_placement()
    w_in_bf = w_in.astype(bf16)
    kh = GLA_H * GLA_DK

    for l in range(DEPTH):
        wq2, wkv = _mla_weights(mla_w_uq[l], mla_w_ukv[l])
        w2p = jnp.zeros((128, 2 * kh), f32)
        w2p = w2p.at[64:64 + GLA_RANK, 0:kh].set(gla_gate_w2[l, 0]).at[64 + GLA_RANK:64 + 2 * GLA_RANK, kh:].set(gla_gate_w2[l, 1])
        b2 = gla_gate_b[l].reshape(1, 2 * kh)

        has_ctx = l < DEPTH - 1
        p = _in_proj(resid, mods, w_in_bf, l)
        q, k, vt = _mla_prep(p, tc, ts, mla_q_norm[l].reshape(1, -1), mla_kv_norm[l].reshape(1, -1), wq2, wkv, e2)
        ya = _attention(q, k, vt, has_ctx)
        yb = _gla(p, w2p.astype(bf16), b2)
        yc = _retention(p, cos_r, sin_r, cumf, cumb)
        xa, h2, aff = _out_proj(resid, ya, yb, yc, mods, w_out, router_w[l].T.astype(bf16), l, has_ctx)
        post, tok, gate = _topk(aff, has_ctx)
        ys = _moe_ffn(tok, gate, h2, exp_w_gate, exp_w_up, exp_w_down, l)
        resid = (_combine(post, ys, xa, mods, l, has_ctx, final_gain=None if has_ctx else final_norm),)
    return resid[0]
```

```python
import functools

import numpy as np
import jax
import jax.numpy as jnp
from jax import lax
from jax.experimental import pallas as pl
from jax.experimental.pallas import tpu as pltpu

f32 = jnp.float32
bf16 = jnp.bfloat16

D = 2048
BATCH = 4
N_LAT = 2048
N_CTX = 256
S = N_CTX + N_LAT
R = BATCH * S
DEPTH = 2
GRID_W = 64
EPS = 1e-6
LOG2E = 1.4426950408889634
ROPE_BASE = 10000.0
CHUNK = 64

MLA_H, MLA_QR, MLA_KVR, MLA_NOPE, MLA_ROPE, MLA_V = 8, 512, 256, 128, 64, 128
GLA_H, GLA_DK, GLA_DV, GLA_RANK, GLA_TAU = 4, 64, 128, 16, 16.0
RET_H, RET_DK, RET_DV, RET_EXP0 = 4, 128, 128, 5.0
N_EXP, EXP_FF, EC_CAP = 16, 2048, 2
CAP_LAT = EC_CAP * N_LAT // N_EXP
CAP_CTX = EC_CAP * N_CTX // N_EXP
SLOTS = CAP_LAT + CAP_CTX

TM = 256
TILES = S // TM
LAT_TILES = N_LAT // TM
HP = 256

C_RQ, C_RK, C_RV, C_RG = 0, 512, 1024, 1536
C_GV, C_GOG, C_CQ, C_CKV, C_GQ, C_GK, C_KRZ = 2048, 2560, 3072, 3584, 3840, 4096, 4352
NC = 4608
TN_IN = 1536

VMEM_LIMIT = 56 * 1024 * 1024


def _cp(sem):
    return pltpu.CompilerParams(dimension_semantics=sem, vmem_limit_bytes=VMEM_LIMIT)


def _nt(a, b):
    return lax.dot_general(a, b, (((1,), (1,)), ((), ())), preferred_element_type=f32)


def _tn(a, b):
    return lax.dot_general(a, b, (((0,), (0,)), ((), ())), preferred_element_type=f32)


def _dot(a, b):
    return jnp.dot(a, b, preferred_element_type=f32)


def _rms(x):
    return x * lax.rsqrt(jnp.mean(x * x, axis=-1, keepdims=True) + EPS)


def _silu(x):
    return x * (1.0 / (1.0 + jnp.exp(-x)))


def _mod_row(i):
    return jnp.where(i % TILES == 0, BATCH, i // TILES)


def _lat_first(i):
    return (i // TILES) * TILES + (i % TILES + TILES - 1) % TILES


def _mod_kernel(s_ref, w_ref, b_ref, o_ref):
    s = _silu(s_ref[...]).astype(bf16)
    o_ref[...] = _dot(s, w_ref[...].astype(bf16)) + b_ref[...]


def _modulation(cc, ada_w, ada_b):
    tn = 1024
    return pl.pallas_call(
        _mod_kernel,
        grid=(DEPTH, 6 * D // tn),
        in_specs=[pl.BlockSpec((8, D), lambda l, j: (0, 0)),
                  pl.BlockSpec((None, D, tn), lambda l, j: (l, 0, j)),
                  pl.BlockSpec((None, 1, tn), lambda l, j: (l, 0, j))],
        out_specs=pl.BlockSpec((None, 8, tn), lambda l, j: (l, 0, j)),
        out_shape=jax.ShapeDtypeStruct((DEPTH, 8, 6 * D), f32),
        compiler_params=_cp(("arbitrary", "arbitrary")),
        name="modulation",
    )(cc, ada_w, ada_b.reshape(DEPTH, 1, 6 * D))


def _tile_specs(arrs, tile_of):
    w = arrs[0].shape[-1]
    if arrs[0].ndim == 2:
        return [pl.BlockSpec((TM, w), lambda *g: (tile_of(*g), 0))]
    lat = pl.BlockSpec((None, TM, w), lambda *g: (tile_of(*g) // TILES, jnp.maximum(tile_of(*g) % TILES - 1, 0), 0))
    if len(arrs) == 1:
        return [lat]
    return [lat, pl.BlockSpec((None, TM, w), lambda *g: (tile_of(*g) // TILES, 0, 0))]


def _tile_value(refs, is_ctx, rows=slice(None)):
    if len(refs) == 1:
        return refs[0][rows, :]
    return jnp.where(is_ctx, refs[1][rows, :], refs[0][rows, :])


def _in_pieces():
    o = np.cumsum((0, MLA_QR, MLA_KVR, MLA_ROPE, 256, 256, 512, 32, 512, 512, 512, 512, 512))
    cq, ckv, kr, gq, gk, gv, gz, gog, ret = (int(v) for v in o[:9])
    return ((ret, 2048, C_RQ), (gv, 512, C_GV), (gog, 512, C_GOG), (cq, 512, C_CQ), (ckv, 256, C_CKV),
            (gq, 256, C_GQ), (gk, 256, C_GK), (kr, MLA_ROPE, C_KRZ), (gz, 2 * GLA_RANK, C_KRZ + MLA_ROPE))


IN_COLS = 4448
RB = 256


def _in_kernel(*refs, layer):
    *resid, mod_ref, w_hbm, o_ref, w_ref, stage, sem = refs

    @pl.when(pl.program_id(0) == 0)
    def _():
        n_blk = D // RB

        def fetch(i, slot):
            return pltpu.make_async_copy(w_hbm.at[layer, pl.ds(i * RB, RB), :], stage.at[slot], sem.at[slot])

        fetch(0, 0).start()
        for i in range(n_blk):
            slot = i % 2
            if i + 1 < n_blk:
                fetch(i + 1, 1 - slot).start()
            fetch(i, slot).wait()
            rows = slice(i * RB, (i + 1) * RB)
            for src, width, dst in _in_pieces():
                w_ref[rows, dst:dst + width] = stage[slot, :, src:src + width].astype(bf16)
            end = C_KRZ + MLA_ROPE + 2 * GLA_RANK
            w_ref[rows, end:NC] = jnp.zeros((RB, NC - end), bf16)

    x = _tile_value(resid, pl.program_id(0) % TILES == 0)
    h = (_rms(x) * (1.0 + mod_ref[1:2, :]) + mod_ref[0:1, :]).astype(bf16)
    for j in range(NC // TN_IN):
        o_ref[:, j * TN_IN:(j + 1) * TN_IN] = _dot(h, w_ref[:, j * TN_IN:(j + 1) * TN_IN]).astype(bf16)


def _in_proj(resid, mods, w_in, layer):
    return pl.pallas_call(
        functools.partial(_in_kernel, layer=layer),
        grid=(R // TM,),
        in_specs=_tile_specs(resid, lambda i: i) + [
            pl.BlockSpec((None, None, 6, D), lambda i: (layer, _mod_row(i), 0, 0)),
            pl.BlockSpec(memory_space=pl.ANY)],
        out_specs=pl.BlockSpec((TM, NC), lambda i: (i, 0)),
        out_shape=jax.ShapeDtypeStruct((R, NC), bf16),
        scratch_shapes=[pltpu.VMEM((D, NC), bf16), pltpu.VMEM((2, RB, IN_COLS), f32), pltpu.SemaphoreType.DMA((2,))],
        compiler_params=_cp(("arbitrary",)),
        name="in_proj",
    )(*resid, mods, w_in)


def _mla_prep_kernel(cq_ref, ckv_ref, krz_ref, tc_ref, ts_ref, qn_ref, kvn_ref, wq_ref, wkv_ref, e2_ref,
                     q_ref, k_ref, vt_ref):
    tc = tc_ref[...]
    ts = ts_ref[...]
    scale = (MLA_NOPE + MLA_ROPE) ** -0.5 * LOG2E
    hq = (_rms(cq_ref[...].astype(f32)) * qn_ref[...]).astype(bf16)
    q2 = _dot(hq, wq_ref[...])
    hkv = (_rms(ckv_ref[...].astype(f32)) * kvn_ref[...]).astype(bf16)
    kv = _dot(hkv, wkv_ref[...])
    kr2 = _dot(krz_ref[...], e2_ref[...])
    nq = MLA_H * HP
    for h in range(MLA_H):
        sl = slice(h * HP, (h + 1) * HP)
        sl2 = slice(nq + h * HP, nq + (h + 1) * HP)
        q_ref[:, sl] = ((q2[:, sl] * tc + q2[:, sl2] * ts) * scale).astype(bf16)
        k_ref[:, sl] = (kv[:, sl] + kr2[:, sl] * tc + kr2[:, sl2] * ts).astype(bf16)
    vt_ref[...] = kv[:, nq:].T.astype(bf16)


def _mla_prep(p, tc, ts, qn, kvn, wq2, wkv, e2):
    nq = MLA_H * HP
    nv = MLA_H * MLA_V
    const = lambda shape: pl.BlockSpec(shape, lambda i: (0, 0))
    return pl.pallas_call(
        _mla_prep_kernel,
        grid=(R // TM,),
        in_specs=[pl.BlockSpec((TM, MLA_QR), lambda i: (i, C_CQ // MLA_QR)),
                  pl.BlockSpec((TM, MLA_KVR), lambda i: (i, C_CKV // MLA_KVR)),
                  pl.BlockSpec((TM, 128), lambda i: (i, C_KRZ // 128)),
                  pl.BlockSpec((TM, HP), lambda i: (i % TILES, 0)),
                  pl.BlockSpec((TM, HP), lambda i: (i % TILES, 0)),
                  const((1, MLA_QR)), const((1, MLA_KVR)),
                  const((MLA_QR, 2 * nq)), const((MLA_KVR, nq + nv)), const((128, 2 * nq))],
        out_specs=[pl.BlockSpec((TM, nq), lambda i: (_lat_first(i), 0)),
                   pl.BlockSpec((TM, nq), lambda i: (i, 0)),
                   pl.BlockSpec((None, nv, TM), lambda i: (i // TILES, 0, i % TILES))],
        out_shape=[jax.ShapeDtypeStruct((R, nq), bf16), jax.ShapeDtypeStruct((R, nq), bf16),
                   jax.ShapeDtypeStruct((BATCH, nv, S), bf16)],
        compiler_params=_cp(("arbitrary",)),
        name="mla_prep",
    )(p, p, p, tc, ts, qn, kvn, wq2, wkv, e2)


TQ = 2048
ATT_HEADS_PER_STEP = 1


def _attn_body(q_ref, k_ref, vt_ref, o_ref, s_sc, n_chunks, n_heads):
    nq = q_ref.shape[0]
    qs = [q_ref[:, h * HP:(h + 1) * HP] for h in range(n_heads)]

    def scores(h, j, m):
        s = _nt(k_ref[j * TM:(j + 1) * TM, h * HP:(h + 1) * HP], qs[h])
        s_sc[h, j] = s
        cm = jnp.max(s, axis=0, keepdims=True)
        return cm if m is None else jnp.maximum(m, cm)

    m = [None] * n_heads
    for j in range(n_chunks):
        m[0] = scores(0, j, m[0])
    for h in range(n_heads):
        l = jnp.zeros((1, nq), f32)
        acc = jnp.zeros((MLA_V, nq), f32)
        for j in range(n_chunks):
            p = jnp.exp2(s_sc[h, j] - m[h])
            l = l + jnp.sum(p, axis=0, keepdims=True)
            acc = acc + _dot(vt_ref[h * MLA_V:(h + 1) * MLA_V, j * TM:(j + 1) * TM], p.astype(bf16))
            if h + 1 < n_heads:
                m[h + 1] = scores(h + 1, j, m[h + 1])
        o_ref[:, h * MLA_V:(h + 1) * MLA_V] = (acc * (1.0 / l)).T.astype(bf16)


def _attention(q, k, vt, has_ctx):
    nq = MLA_H * HP
    nv = MLA_H * MLA_V
    q3 = q.reshape(BATCH, S, nq)
    k3 = k.reshape(BATCH, S, nq)
    hs = ATT_HEADS_PER_STEP
    ya = pl.pallas_call(
        functools.partial(_attn_body, n_chunks=S // TM, n_heads=hs),
        grid=(BATCH, MLA_H // hs, N_LAT // TQ),
        in_specs=[pl.BlockSpec((None, TQ, hs * HP), lambda b, h, t: (b, t, h)),
                  pl.BlockSpec((None, S, hs * HP), lambda b, h, t: (b, 0, h)),
                  pl.BlockSpec((None, hs * MLA_V, S), lambda b, h, t: (b, h, 0))],
        out_specs=pl.BlockSpec((None, TQ, hs * MLA_V), lambda b, h, t: (b, t, h)),
        out_shape=jax.ShapeDtypeStruct((BATCH, N_LAT, nv), bf16),
        scratch_shapes=[pltpu.VMEM((hs, S // TM, TM, TQ), f32)],
        compiler_params=_cp(("arbitrary", "arbitrary", "arbitrary")),
        name="mla_attention",
    )(q3, k3, vt)
    if not has_ctx:
        return (ya,)
    ya_ctx = pl.pallas_call(
        functools.partial(_attn_body, n_chunks=1, n_heads=1),
        grid=(BATCH, MLA_H),
        in_specs=[pl.BlockSpec((None, N_CTX, HP), lambda b, h: (b, N_LAT // N_CTX, h)),
                  pl.BlockSpec((None, N_CTX, HP), lambda b, h: (b, 0, h)),
                  pl.BlockSpec((None, MLA_V, N_CTX), lambda b, h: (b, h, 0))],
        out_specs=pl.BlockSpec((None, N_CTX, MLA_V), lambda b, h: (b, 0, h)),
        out_shape=jax.ShapeDtypeStruct((BATCH, N_CTX, nv), bf16),
        scratch_shapes=[pltpu.VMEM((1, 1, TM, N_CTX), f32)],
        compiler_params=_cp(("arbitrary", "arbitrary")),
        name="mla_attention_ctx",
    )(q3, k3, vt)
    return (ya, ya_ctx)


RET_CHUNK = 128
SCAN_H = 4
SCAN_V = 128
SCAN_W = SCAN_H * SCAN_V


def _scan_consts(kh, chunk):
    dk = kh // SCAN_H
    row = lax.broadcasted_iota(jnp.int32, (chunk, SCAN_H * chunk), 0)
    col = lax.broadcasted_iota(jnp.int32, (chunk, SCAN_H * chunk), 1) % chunk
    incl = row >= col
    strict = col > row
    krow = lax.broadcasted_iota(jnp.int32, (SCAN_H * chunk, kh), 0) // chunk
    kcol = lax.broadcasted_iota(jnp.int32, (SCAN_H * chunk, kh), 1) // dk
    kmask = krow == kcol
    vrow = lax.broadcasted_iota(jnp.int32, (SCAN_H * chunk, SCAN_W), 0) // chunk
    vcol = lax.broadcasted_iota(jnp.int32, (SCAN_H * chunk, SCAN_W), 1) // SCAN_V
    vmask = vrow == vcol
    srow = lax.broadcasted_iota(jnp.int32, (SCAN_W, kh), 0) // SCAN_V
    scol = lax.broadcasted_iota(jnp.int32, (SCAN_W, kh), 1) // dk
    smask = srow == scol
    return incl, strict, kmask, vmask, smask


def _chunk_step(q, k, v, cum, cend, st_ref, amask, kmask, vmask, smask):
    qd = (q * jnp.exp(cum)).astype(bf16)
    ki = k * jnp.exp(-cum)
    kend = (k * jnp.exp(cend - cum)).astype(bf16)
    dec = jnp.exp(cend)
    kst = jnp.where(kmask, jnp.concatenate([ki] * SCAN_H, axis=0), 0.0).astype(bf16)
    att = jnp.where(amask, _nt(qd, kst), 0.0).astype(bf16)
    vbd = jnp.where(vmask, jnp.concatenate([v] * SCAN_H, axis=0), jnp.zeros((), bf16))
    st = st_ref[...]
    o = _dot(att, vbd) + _nt(qd, st.astype(bf16))
    st_ref[...] = st * dec + jnp.where(smask, _tn(v, kend), 0.0)
    return o


def _bwd_chunk(i, chunk):
    n_ctx, n_all = N_CTX // chunk, S // chunk
    return jnp.where(i < n_ctx, n_ctx - 1 - i, n_all + n_ctx - 1 - i)


def _scan_finish(of_sc, ob_sc, g_ref, y_ref):
    def fin(i, carry):
        r0 = pl.multiple_of(i * TM, TM)
        o = of_sc[pl.ds(r0, TM), :] + ob_sc[pl.ds(r0, TM), :]
        g = g_ref[pl.ds(r0, TM), :].astype(f32)
        for h in range(SCAN_H):
            sl = slice(h * SCAN_V, (h + 1) * SCAN_V)
            y_ref[pl.ds(r0, TM), sl] = (_rms(o[:, sl]) * _silu(g[:, sl])).astype(bf16)
        return carry

    lax.fori_loop(0, TILES, fin, 0)


def _gla_kernel(q_ref, k_ref, v_ref, krz_ref, og_ref, w2_ref, b2_ref, y_ref, cum_sc, of_sc, ob_sc, stf_sc, stb_sc):
    kh = GLA_H * GLA_DK
    ri = lax.broadcasted_iota(jnp.int32, (TM, TM), 0)
    ci = lax.broadcasted_iota(jnp.int32, (TM, TM), 1)
    same = (ri // CHUNK) == (ci // CHUNK)
    pre = jnp.where(same & (ci <= ri), 1.0, 0.0).astype(bf16)
    suf = jnp.where(same & (ci >= ri), 1.0, 0.0).astype(bf16)

    def exact_sum(m, x):
        hi = x.astype(bf16)
        r1 = x - hi.astype(f32)
        mid = r1.astype(bf16)
        lo = (r1 - mid.astype(f32)).astype(bf16)
        return _dot(m, hi) + _dot(m, mid) + _dot(m, lo)

    def gates(i, carry):
        r0 = pl.multiple_of(i * TM, TM)
        lg = _dot(krz_ref[pl.ds(r0, TM), :], w2_ref[...]) + b2_ref[...]
        la = (jnp.minimum(lg, 0.0) - jnp.log1p(jnp.exp(-jnp.abs(lg)))) * (1.0 / GLA_TAU)
        cum_sc[pl.ds(r0, TM), 0:kh] = exact_sum(pre, la[:, 0:kh])
        cum_sc[pl.ds(r0, TM), kh:2 * kh] = exact_sum(suf, la[:, kh:2 * kh])
        return carry

    lax.fori_loop(0, TILES, gates, 0, unroll=3)

    incl, strict, kmask, vmask, smask = _scan_consts(kh, CHUNK)
    stf_sc[...] = jnp.zeros_like(stf_sc)
    stb_sc[...] = jnp.zeros_like(stb_sc)
    qscale = GLA_DK ** -0.5

    def body(i, carry):
        rf = pl.multiple_of(i * CHUNK, CHUNK)
        cum = cum_sc[pl.ds(rf, CHUNK), 0:kh]
        of_sc[pl.ds(rf, CHUNK), :] = _chunk_step(
            q_ref[pl.ds(rf, CHUNK), :].astype(f32) * qscale, k_ref[pl.ds(rf, CHUNK), :].astype(f32),
            v_ref[pl.ds(rf, CHUNK), :], cum, cum[CHUNK - 1:CHUNK, :], stf_sc, incl, kmask, vmask, smask)
        rb = pl.multiple_of(_bwd_chunk(i, CHUNK) * CHUNK, CHUNK)
        rc = cum_sc[pl.ds(rb, CHUNK), kh:2 * kh]
        ob_sc[pl.ds(rb, CHUNK), :] = _chunk_step(
            q_ref[pl.ds(rb, CHUNK), :].astype(f32) * qscale, k_ref[pl.ds(rb, CHUNK), :].astype(f32),
            v_ref[pl.ds(rb, CHUNK), :], rc, rc[0:1, :], stb_sc, strict, kmask, vmask, smask)
        return carry

    lax.fori_loop(0, S // CHUNK, body, 0, unroll=6)
    _scan_finish(of_sc, ob_sc, og_ref, y_ref)


def _gla(p, w2p, b2):
    kh = GLA_H * GLA_DK
    p3 = p.reshape(BATCH, S, NC)
    col = lambda w, c: pl.BlockSpec((None, S, w), lambda b: (b, 0, c // w))
    return pl.pallas_call(
        _gla_kernel,
        grid=(BATCH,),
        in_specs=[col(kh, C_GQ), col(kh, C_GK), col(SCAN_W, C_GV), col(128, C_KRZ), col(SCAN_W, C_GOG),
                  pl.BlockSpec((128, 2 * kh), lambda b: (0, 0)), pl.BlockSpec((1, 2 * kh), lambda b: (0, 0))],
        out_specs=pl.BlockSpec((None, S, SCAN_W), lambda b: (b, 0, 0)),
        out_shape=jax.ShapeDtypeStruct((BATCH, S, SCAN_W), bf16),
        scratch_shapes=[pltpu.VMEM((S, 2 * kh), f32), pltpu.VMEM((S, SCAN_W), f32), pltpu.VMEM((S, SCAN_W), f32),
                        pltpu.VMEM((SCAN_W, kh), f32), pltpu.VMEM((SCAN_W, kh), f32)],
        compiler_params=_cp(("arbitrary",)),
        name="gla",
    )(p3, p3, p3, p3, p3, w2p, b2).reshape(R, SCAN_W)


def _ret_kernel(q_ref, k_ref, v_ref, g_ref, cos_ref, sin_ref, cumf_ref, cumb_ref, y_ref, of_sc, ob_sc, stf_sc, stb_sc):
    kh = RET_H * RET_DK
    incl, strict, kmask, vmask, smask = _scan_consts(kh, RET_CHUNK)
    stf_sc[...] = jnp.zeros_like(stf_sc)
    stb_sc[...] = jnp.zeros_like(stb_sc)
    kscale = RET_DK ** -0.5
    cumf = cumf_ref[...]
    cumb = cumb_ref[...]

    even = lax.broadcasted_iota(jnp.int32, (RET_CHUNK, RET_DK), 1) % 2 == 0

    def rotate(x, r0):
        cos = jnp.concatenate([cos_ref[pl.ds(r0, RET_CHUNK), :]] * RET_H, axis=1)
        sin = jnp.concatenate([sin_ref[pl.ds(r0, RET_CHUNK), :]] * RET_H, axis=1)
        parts = []
        for h in range(RET_H):
            xh = x[:, h * RET_DK:(h + 1) * RET_DK]
            parts.append(jnp.where(even, pltpu.roll(xh, RET_DK - 1, axis=1), pltpu.roll(xh, 1, axis=1)))
        return x * cos + jnp.concatenate(parts, axis=1) * sin

    def body(i, carry):
        rf = pl.multiple_of(i * RET_CHUNK, RET_CHUNK)
        of_sc[pl.ds(rf, RET_CHUNK), :] = _chunk_step(
            rotate(q_ref[pl.ds(rf, RET_CHUNK), :].astype(f32), rf),
            rotate(k_ref[pl.ds(rf, RET_CHUNK), :].astype(f32), rf) * kscale,
            v_ref[pl.ds(rf, RET_CHUNK), :], cumf, cumf[RET_CHUNK - 1:RET_CHUNK, :], stf_sc, incl, kmask, vmask, smask)
        rb = pl.multiple_of(_bwd_chunk(i, RET_CHUNK) * RET_CHUNK, RET_CHUNK)
        ob_sc[pl.ds(rb, RET_CHUNK), :] = _chunk_step(
            rotate(q_ref[pl.ds(rb, RET_CHUNK), :].astype(f32), rb),
            rotate(k_ref[pl.ds(rb, RET_CHUNK), :].astype(f32), rb) * kscale,
            v_ref[pl.ds(rb, RET_CHUNK), :], cumb, cumb[0:1, :], stb_sc, strict, kmask, vmask, smask)
        return carry

    lax.fori_loop(0, S // RET_CHUNK, body, 0, unroll=3)
    _scan_finish(of_sc, ob_sc, g_ref, y_ref)


def _retention(p, cos_r, sin_r, cumf, cumb):
    kh = RET_H * RET_DK
    p3 = p.reshape(BATCH, S, NC)
    col = lambda w, c: pl.BlockSpec((None, S, w), lambda b: (b, 0, c // w))
    const = lambda shape: pl.BlockSpec(shape, lambda b: (0, 0))
    return pl.pallas_call(
        _ret_kernel,
        grid=(BATCH,),
        in_specs=[col(kh, C_RQ), col(kh, C_RK), col(SCAN_W, C_RV), col(SCAN_W, C_RG),
                  const((S, RET_DK)), const((S, RET_DK)), const((RET_CHUNK, kh)), const((RET_CHUNK, kh))],
        out_specs=pl.BlockSpec((None, S, SCAN_W), lambda b: (b, 0, 0)),
        out_shape=jax.ShapeDtypeStruct((BATCH, S, SCAN_W), bf16),
        scratch_shapes=[pltpu.VMEM((S, SCAN_W), f32), pltpu.VMEM((S, SCAN_W), f32),
                        pltpu.VMEM((SCAN_W, kh), f32), pltpu.VMEM((SCAN_W, kh), f32)],
        compiler_params=_cp(("arbitrary",)),
        name="retention",
    )(p3, p3, p3, p3, cos_r, sin_r, cumf, cumb).reshape(R, SCAN_W)


def _lat_tile(g):
    return (g // LAT_TILES) * TILES + 1 + g % LAT_TILES


def _out_kernel(*refs, n_resid, has_ctx):
    resid, refs = refs[:n_resid], refs[n_resid:]
    *ya, yb_ref, yc_ref, mod_ref, wf_ref, rwt_ref, xo_ref, h2_ref, aff_ref, w_ref = refs
    is_ctx = (pl.program_id(0) % TILES == 0) if has_ctx else False

    @pl.when(pl.program_id(0) == 0)
    def _():
        w_ref[...] = wf_ref[...].astype(bf16)

    na = MLA_H * MLA_V
    half = TM // 2
    for r in range(2):
        rows = slice(r * half, (r + 1) * half)
        acc = _dot(_tile_value(ya, is_ctx, rows), w_ref[0:na, :])
        acc += _dot(yb_ref[rows, :], w_ref[na:na + SCAN_W, :])
        acc += _dot(yc_ref[rows, :], w_ref[na + SCAN_W:, :])
        x = _tile_value(resid, is_ctx, rows) + mod_ref[2:3, :] * acc
        xo_ref[rows, :] = x
        h = _rms(x) * (1.0 + mod_ref[4:5, :]) + mod_ref[3:4, :]
        h2_ref[rows, :] = h
        hb = h.astype(bf16)
        lg = _nt(rwt_ref[...], hb)
        e = jnp.exp(lg - jnp.max(lg, axis=0, keepdims=True))
        aff_ref[:, rows] = e / jnp.sum(e, axis=0, keepdims=True)


def _out_proj(resid, ya, yb, yc, mods, w_out, rwt, layer, has_ctx):
    tile = (lambda g: g) if has_ctx else _lat_tile
    n_tiles = R // TM if has_ctx else BATCH * LAT_TILES
    out_row = lambda w: pl.BlockSpec((TM, w), lambda g: (g, 0))
    return pl.pallas_call(
        functools.partial(_out_kernel, n_resid=len(resid), has_ctx=has_ctx),
        grid=(n_tiles,),
        in_specs=_tile_specs(resid, tile) + _tile_specs(ya, tile) + _tile_specs((yb,), tile) + _tile_specs((yc,), tile) + [
            pl.BlockSpec((None, None, 6, D), lambda g: (layer, _mod_row(tile(g)), 0, 0)),
            pl.BlockSpec((None, D, D), lambda g: (layer, 0, 0), pipeline_mode=pl.Buffered(1)),
            pl.BlockSpec((N_EXP, D), lambda g: (0, 0))],
        out_specs=[out_row(D), out_row(D), pl.BlockSpec((N_EXP, TM), lambda g: (0, g))],
        out_shape=[jax.ShapeDtypeStruct((n_tiles * TM, D), f32), jax.ShapeDtypeStruct((n_tiles * TM, D), f32),
                   jax.ShapeDtypeStruct((N_EXP, n_tiles * TM), f32)],
        scratch_shapes=[pltpu.VMEM((D, D), bf16)],
        compiler_params=_cp(("arbitrary",)),
        name="out_proj",
    )(*resid, *ya, yb, yc, mods, w_out, rwt)


def _topk_kernel(aff_ref, post_ref, tok_ref, gate_ref, pos_sc, *, has_ctx):
    ri = lax.broadcasted_iota(jnp.int32, (TM, TM), 0)
    ci = lax.broadcasted_iota(jnp.int32, (TM, TM), 1)
    before = jnp.where(ri < ci, 1.0, 0.0).astype(bf16)

    def prefix_count(m):
        out = []
        off = jnp.zeros((N_EXP, 1), f32)
        for blk in range(m.shape[1] // TM):
            mb = m[:, blk * TM:(blk + 1) * TM]
            out.append(_dot(mb.astype(bf16), before) + off)
            off = off + jnp.sum(mb, axis=1, keepdims=True)
        return jnp.concatenate(out, axis=1) if len(out) > 1 else out[0]

    def select(a, cap, base):
        capf = float(cap)

        def step(i, thr_bits):
            cand = thr_bits | jnp.left_shift(jnp.int32(1), 30 - i)
            cnt = jnp.sum(jnp.where(a >= pltpu.bitcast(cand, f32), 1.0, 0.0), axis=1, keepdims=True)
            return jnp.where(cnt >= capf, cand, thr_bits)

        thr = pltpu.bitcast(lax.fori_loop(0, 31, step, jnp.zeros((N_EXP, 1), jnp.int32)), f32)
        gt = jnp.where(a > thr, 1.0, 0.0)
        eq = jnp.where(a == thr, 1.0, 0.0)
        need = capf - jnp.sum(gt, axis=1, keepdims=True)
        keep = gt + eq * jnp.where(prefix_count(eq) < need, 1.0, 0.0)
        return jnp.where(keep > 0.5, prefix_count(keep) + float(base), -1.0)

    if has_ctx:
        pos = jnp.concatenate([select(aff_ref[:, 0:N_CTX], CAP_CTX, CAP_LAT), select(aff_ref[:, N_CTX:], CAP_LAT, 0)], axis=1)
    else:
        pos = select(aff_ref[...], CAP_LAT, 0)
    n = pos.shape[1]
    slots = tok_ref.shape[0]
    pos_sc[...] = pos
    post_ref[...] = jnp.concatenate([pos, jnp.full((128 - N_EXP, n), -1.0, f32)], axis=0).T

    slot_id = lax.broadcasted_iota(jnp.int32, (slots, n), 0).astype(f32)
    tok_id = lax.broadcasted_iota(jnp.int32, (slots, n), 1).astype(f32) + (pl.program_id(0) * n).astype(f32)
    lane = lax.broadcasted_iota(jnp.int32, (slots, 128), 1)

    def invert(e, carry):
        tok, gate = carry
        hit = pos_sc[pl.ds(e, 1), :] == slot_id
        t_e = jnp.sum(jnp.where(hit, tok_id, 0.0), axis=1, keepdims=True)
        g_e = jnp.sum(jnp.where(hit, aff_ref[pl.ds(e, 1), :], 0.0), axis=1, keepdims=True)
        return jnp.where(lane == e, t_e, tok), jnp.where(lane == e, g_e, gate)

    tok, gate = lax.fori_loop(0, N_EXP, invert, (jnp.zeros((slots, 128), f32), jnp.zeros((slots, 128), f32)))
    tok_ref[...] = tok
    gate_ref[...] = gate


def _topk(aff, has_ctx):
    n = S if has_ctx else N_LAT
    slots = SLOTS if has_ctx else CAP_LAT
    return pl.pallas_call(
        functools.partial(_topk_kernel, has_ctx=has_ctx),
        grid=(BATCH,),
        in_specs=[pl.BlockSpec((N_EXP, n), lambda b: (0, b))],
        out_specs=[pl.BlockSpec((n, 128), lambda b: (b, 0)), pl.BlockSpec((slots, 128), lambda b: (b, 0)),
                   pl.BlockSpec((slots, 128), lambda b: (b, 0))],
        out_shape=[jax.ShapeDtypeStruct((BATCH * n, 128), f32), jax.ShapeDtypeStruct((BATCH * slots, 128), f32),
                   jax.ShapeDtypeStruct((BATCH * slots, 128), f32)],
        scratch_shapes=[pltpu.VMEM((N_EXP, n), f32)],
        compiler_params=_cp(("arbitrary",)),
        name="route_topk",
    )(aff)


TF = 512
N_UP = EXP_FF // TF
N_DOWN = D // TF
N_STEP = N_UP + N_DOWN
MOE_VMEM_LIMIT = 60 * 1024 * 1024


def _moe_kernel(idx_ref, h_hbm, wg_ref, wu_ref, wd_ref, gate_ref, y_ref, xraw, xb, hm_ref, sem):
    e = pl.program_id(0)
    s = pl.program_id(1)
    rows = xb.shape[0]
    per = rows // N_STEP

    def row_copy(expert, r):
        src = idx_ref[expert * rows + r]
        return pltpu.make_async_copy(h_hbm.at[pl.ds(src, 1), :], xraw.at[pl.ds(r, 1), :], sem.at[0])

    def all_rows():
        return pltpu.make_async_copy(h_hbm.at[pl.ds(0, rows), :], xraw, sem.at[0])

    def prefetch_share():
        for r in range(per):
            row_copy(e + 1, s * per + r).start()

    @pl.when(s == 0)
    def _():
        @pl.when(e == 0)
        def _():
            def first(r, carry):
                row_copy(0, r).start()
                return carry

            lax.fori_loop(0, rows, first, 0)

        all_rows().wait()
        xb[...] = xraw[...].astype(bf16)

    @pl.when(s < N_UP)
    def _():
        prefetch_share()
        x = xb[...]
        a = _dot(x, wg_ref[...].astype(bf16))
        u = _dot(x, wu_ref[...].astype(bf16))
        hm_ref[s] = (_silu(a) * u).astype(bf16)

    @pl.when(s >= N_UP)
    def _():
        prefetch_share()
        wd = wd_ref[...].astype(bf16)
        acc = _dot(hm_ref[0], wd[0:TF, :])
        for c in range(1, N_UP):
            acc += _dot(hm_ref[c], wd[c * TF:(c + 1) * TF, :])
        lane = lax.broadcasted_iota(jnp.int32, (rows, 128), 1)
        g = jnp.sum(jnp.where(lane == e, gate_ref[...], 0.0), axis=1, keepdims=True)
        y_ref[...] = (acc * g).astype(bf16)

    @pl.when((e == N_EXP - 1) & (s == N_STEP - 1))
    def _():
        all_rows().wait()


def _moe_ffn(tok, gate, h2, w_gate, w_up, w_down, layer):
    rows = tok.shape[0]
    idx = tok[:, :N_EXP].T.astype(jnp.int32).reshape(-1)
    idx = jnp.concatenate([idx, jnp.zeros((rows,), jnp.int32)])
    up = lambda s: jnp.minimum(s, N_UP - 1)
    down = lambda s: jnp.maximum(s - N_UP, 0)
    return pl.pallas_call(
        _moe_kernel,
        grid_spec=pltpu.PrefetchScalarGridSpec(
            num_scalar_prefetch=1,
            grid=(N_EXP, N_STEP),
            in_specs=[pl.BlockSpec(memory_space=pl.ANY),
                      pl.BlockSpec((None, None, D, TF), lambda e, s, idx: (layer, e, 0, up(s))),
                      pl.BlockSpec((None, None, D, TF), lambda e, s, idx: (layer, e, 0, up(s))),
                      pl.BlockSpec((None, None, EXP_FF, TF), lambda e, s, idx: (layer, e, 0, down(s))),
                      pl.BlockSpec((rows, 128), lambda e, s, idx: (0, 0))],
            out_specs=pl.BlockSpec((None, rows, TF), lambda e, s, idx: (e, 0, down(s))),
            scratch_shapes=[pltpu.VMEM((rows, D), f32), pltpu.VMEM((rows, D), bf16),
                            pltpu.VMEM((N_UP, rows, TF), bf16), pltpu.SemaphoreType.DMA((1,))]),
        out_shape=jax.ShapeDtypeStruct((N_EXP, rows, D), bf16),
        compiler_params=pltpu.CompilerParams(dimension_semantics=("arbitrary", "arbitrary"),
                                             vmem_limit_bytes=MOE_VMEM_LIMIT, disable_bounds_checks=True),
        name="moe_ffn",
    )(idx, h2, w_gate, w_up, w_down, gate)


TN_C = 1024


def _combine_kernel(post_ref, ys_ref, x_ref, mod_ref, *rest, has_ctx):
    gain_ref, o_ref = rest if len(rest) == 2 else (None, rest[0])
    tn = o_ref.shape[-1]
    t = pl.program_id(2) if has_ctx else pl.program_id(2) + 1
    pb = post_ref[...]

    def scatter(cap, base, ys):
        if cap % 128 == 0:
            slot = lax.broadcasted_iota(jnp.int32, (TM, cap), 1).astype(f32) + float(base)
            onehot = jnp.concatenate(
                [jnp.where(pb[:, e:e + 1] == slot, 1.0, 0.0).astype(bf16) for e in range(N_EXP)], axis=1)
        else:
            pc = pb - float(base)
            pc = jnp.where((pc >= 0.0) & (pc < float(cap)), pc, -1.0).astype(bf16)
            er = lax.broadcasted_iota(jnp.int32, (128, N_EXP * cap), 0)
            ec = lax.broadcasted_iota(jnp.int32, (128, N_EXP * cap), 1) // cap
            rep = jnp.where(er == ec, 1.0, 0.0).astype(bf16)
            slot = (lax.broadcasted_iota(jnp.int32, (TM, N_EXP * cap), 1) % cap).astype(f32)
            onehot = jnp.where(_dot(pc, rep) == slot, 1.0, 0.0).astype(bf16)
        x = x_ref[...] + mod_ref[5:6, :] * _dot(onehot, ys)
        o_ref[...] = x if gain_ref is None else _rms(x) * gain_ref[...]

    if has_ctx:
        @pl.when(t == 0)
        def _():
            scatter(CAP_CTX, CAP_LAT, ys_ref[:, CAP_LAT:SLOTS, :].reshape(N_EXP * CAP_CTX, tn))

    @pl.when(t > 0)
    def _():
        scatter(CAP_LAT, 0, ys_ref[:, 0:CAP_LAT, :].reshape(N_EXP * CAP_LAT, tn))


def _combine(post, ys, xa, mods, layer, has_ctx, final_gain=None):
    final = final_gain is not None
    assert not (final and has_ctx)
    slots = SLOTS if has_ctx else CAP_LAT
    tps = TILES if has_ctx else LAT_TILES
    tn = D if final else TN_C
    tile = lambda b, t: b * tps + t
    mod_row = (lambda b, t: jnp.where(t == 0, BATCH, b)) if has_ctx else (lambda b, t: b)
    extra_in, extra_specs = ((final_gain.reshape(1, D),), [pl.BlockSpec((1, D), lambda b, n, t: (0, 0))]) if final else ((), [])
    out = pl.pallas_call(
        functools.partial(_combine_kernel, has_ctx=has_ctx),
        grid=(BATCH, D // tn, tps),
        in_specs=[pl.BlockSpec((TM, 128), lambda b, n, t: (tile(b, t), 0)),
                  pl.BlockSpec((N_EXP, None, slots, tn), lambda b, n, t: (0, b, 0, n)),
                  pl.BlockSpec((TM, tn), lambda b, n, t: (tile(b, t), n)),
                  pl.BlockSpec((None, None, 6, tn), lambda b, n, t: (layer, mod_row(b, t), 0, n))] + extra_specs,
        out_specs=pl.BlockSpec((TM, tn), lambda b, n, t: (tile(b, t), n)),
        out_shape=jax.ShapeDtypeStruct((BATCH * tps * TM, D), f32),
        compiler_params=_cp(("arbitrary", "arbitrary", "arbitrary")),
        name="moe_combine",
    )(post, ys.reshape(N_EXP, BATCH, slots, D), xa, mods, *extra_in)
    return out.reshape(BATCH, N_LAT, D) if final else out


def _mla_weights(w_uq, w_ukv):
    half = MLA_ROPE // 2
    wq = w_uq.reshape(MLA_QR, MLA_H, MLA_NOPE + MLA_ROPE)
    nope = wq[:, :, :MLA_NOPE]
    rope = wq[:, :, MLA_NOPE:].reshape(MLA_QR, MLA_H, half, 2)
    ev, od = rope[..., 0], rope[..., 1]
    zpad = jnp.zeros((MLA_QR, MLA_H, HP - MLA_NOPE - MLA_ROPE), w_uq.dtype)
    q_main = jnp.concatenate([nope, ev, od, zpad], axis=-1).reshape(MLA_QR, MLA_H * HP)
    q_part = jnp.concatenate([jnp.zeros_like(nope), od, ev, zpad], axis=-1).reshape(MLA_QR, MLA_H * HP)
    wkv = w_ukv.reshape(MLA_KVR, MLA_H, MLA_NOPE + MLA_V)
    k_main = jnp.concatenate([wkv[:, :, :MLA_NOPE], jnp.zeros((MLA_KVR, MLA_H, HP - MLA_NOPE), w_ukv.dtype)], axis=-1)
    v_main = wkv[:, :, MLA_NOPE:]
    return (jnp.concatenate([q_main, q_part], axis=1).astype(bf16),
            jnp.concatenate([k_main.reshape(MLA_KVR, MLA_H * HP), v_main.reshape(MLA_KVR, MLA_H * MLA_V)], axis=1).astype(bf16))


def _rope_key_placement():
    half = MLA_ROPE // 2
    nq = MLA_H * HP
    e2 = np.zeros((128, 2 * nq), np.float32)
    for h in range(MLA_H):
        for i in range(half):
            ev, od = h * HP + MLA_NOPE + i, h * HP + MLA_NOPE + half + i
            e2[2 * i, ev] = e2[2 * i + 1, od] = 1.0
            e2[2 * i + 1, nq + ev] = e2[2 * i, nq + od] = 1.0
    return jnp.asarray(e2, bf16)


def _tables():
    rows = N_LAT // GRID_W
    row = np.repeat(np.arange(rows, dtype=np.float32), GRID_W)
    colp = np.tile(np.arange(GRID_W, dtype=np.float32), rows)
    n_freq = MLA_ROPE // 4
    nf32 = np.float32
    inv = np.power(nf32(ROPE_BASE), -np.arange(n_freq, dtype=nf32) / nf32(n_freq))
    ang = np.concatenate([row[:, None] * inv, colp[:, None] * inv], axis=-1)
    cos_a, sin_a = np.cos(ang), np.sin(ang)
    one = np.ones((N_LAT, MLA_NOPE), nf32)
    zpad = np.zeros((N_LAT, HP - MLA_NOPE - MLA_ROPE), nf32)
    tc_lat = np.concatenate([one, cos_a, cos_a, zpad], axis=1)
    ts_lat = np.concatenate([0 * one, -sin_a, sin_a, zpad], axis=1)
    tc_ctx = np.concatenate([np.ones((N_CTX, MLA_NOPE + MLA_ROPE), nf32), np.zeros((N_CTX, HP - MLA_NOPE - MLA_ROPE), nf32)], axis=1)
    tc = np.concatenate([tc_ctx, tc_lat], axis=0)
    ts = np.concatenate([np.zeros((N_CTX, HP), nf32), ts_lat], axis=0)

    inv_r = nf32(1.0) / np.power(nf32(ROPE_BASE), np.linspace(0.0, 1.0, RET_DK // 2, dtype=nf32))
    ang_r = np.arange(N_LAT, dtype=nf32)[:, None] * inv_r
    cos_r = np.concatenate([np.ones((N_CTX, RET_DK), nf32), np.repeat(np.cos(ang_r), 2, axis=1)], axis=0)
    sin_r = np.concatenate([np.zeros((N_CTX, RET_DK), nf32),
                            np.stack([-np.sin(ang_r), np.sin(ang_r)], axis=-1).reshape(N_LAT, RET_DK)], axis=0)

    def log_decay(direction):
        e = nf32(RET_EXP0 + direction) + nf32(2.0) * np.arange(RET_H, dtype=nf32)
        return np.repeat(np.log1p(-np.exp2(-e)), RET_DK)[None, :]

    steps = np.arange(1, RET_CHUNK + 1, dtype=nf32)[:, None]
    cumf = steps * log_decay(0.0)
    cumb = steps[::-1] * log_decay(1.0)
    return tc, ts, cos_r, sin_r, cumf, cumb


def kernel(x, c, ctx, c_ctx, ada_w, ada_b, w_in, mla_q_norm, mla_w_uq, mla_kv_norm, mla_w_ukv, gla_gate_w2,
           gla_gate_b, w_out, router_w, exp_w_gate, exp_w_up, exp_w_down, final_norm):
    resid = (x, ctx)
    cc = jnp.concatenate([c, c_ctx[None, :], jnp.zeros((8 - BATCH - 1, D), f32)], axis=0)
    mods = _modulation(cc, ada_w, ada_b).reshape(DEPTH, 8, 6, D)
    tc, ts, cos_r, sin_r, cumf, cumb = _tables()
    e2 = _rope_key_placement()
    kh = GLA_H * GLA_DK

    for l in range(DEPTH):
        wq2, wkv = _mla_weights(mla_w_uq[l], mla_w_ukv[l])
        w2p = jnp.zeros((128, 2 * kh), f32)
        w2p = w2p.at[64:64 + GLA_RANK, 0:kh].set(gla_gate_w2[l, 0]).at[64 + GLA_RANK:64 + 2 * GLA_RANK, kh:].set(gla_gate_w2[l, 1])
        b2 = gla_gate_b[l].reshape(1, 2 * kh)

        has_ctx = l < DEPTH - 1
        p = _in_proj(resid, mods, w_in, l)
        q, k, vt = _mla_prep(p, tc, ts, mla_q_norm[l].reshape(1, -1), mla_kv_norm[l].reshape(1, -1), wq2, wkv, e2)
        ya = _attention(q, k, vt, has_ctx)
        yb = _gla(p, w2p.astype(bf16), b2)
        yc = _retention(p, cos_r, sin_r, cumf, cumb)
        xa, h2, aff = _out_proj(resid, ya, yb, yc, mods, w_out, router_w[l].T.astype(bf16), l, has_ctx)
        post, tok, gate = _topk(aff, has_ctx)
        ys = _moe_ffn(tok, gate, h2, exp_w_gate, exp_w_up, exp_w_down, l)
        resid = (_combine(post, ys, xa, mods, l, has_ctx, final_gain=None if has_ctx else final_norm),)
    return resid[0]
```

```python
import functools

import numpy as np
import jax
import jax.numpy as jnp
from jax import lax
from jax.experimental import pallas as pl
from jax.experimental.pallas import tpu as pltpu

f32 = jnp.float32
bf16 = jnp.bfloat16

D = 2048
BATCH = 4
N_LAT = 2048
N_CTX = 256
S = N_CTX + N_LAT
R = BATCH * S
DEPTH = 2
GRID_W = 64
EPS = 1e-6
LOG2E = 1.4426950408889634
ROPE_BASE = 10000.0
CHUNK = 64

MLA_H, MLA_QR, MLA_KVR, MLA_NOPE, MLA_ROPE, MLA_V = 8, 512, 256, 128, 64, 128
GLA_H, GLA_DK, GLA_DV, GLA_RANK, GLA_TAU = 4, 64, 128, 16, 16.0
RET_H, RET_DK, RET_DV, RET_EXP0 = 4, 128, 128, 5.0
N_EXP, EXP_FF, EC_CAP = 16, 2048, 2
CAP_LAT = EC_CAP * N_LAT // N_EXP
CAP_CTX = EC_CAP * N_CTX // N_EXP
SLOTS = CAP_LAT + CAP_CTX

TM = 256
TILES = S // TM
LAT_TILES = N_LAT // TM
HP = 256

C_RQ, C_RK, C_RV, C_RG = 0, 512, 1024, 1536
C_GV, C_GOG, C_CQ, C_CKV, C_GQ, C_GK, C_KRZ = 2048, 2560, 3072, 3584, 3840, 4096, 4352
NC = 4608
TN_IN = 1536

VMEM_LIMIT = 56 * 1024 * 1024


def _cp(sem):
    return pltpu.CompilerParams(dimension_semantics=sem, vmem_limit_bytes=VMEM_LIMIT)


def _nt(a, b):
    return lax.dot_general(a, b, (((1,), (1,)), ((), ())), preferred_element_type=f32)


def _tn(a, b):
    return lax.dot_general(a, b, (((0,), (0,)), ((), ())), preferred_element_type=f32)


def _dot(a, b):
    return jnp.dot(a, b, preferred_element_type=f32)


def _rms(x):
    return x * lax.rsqrt(jnp.mean(x * x, axis=-1, keepdims=True) + EPS)


def _silu(x):
    return x * (1.0 / (1.0 + jnp.exp(-x)))


def _mod_row(i):
    return jnp.where(i % TILES == 0, BATCH, i // TILES)


def _lat_first(i):
    return (i // TILES) * TILES + (i % TILES + TILES - 1) % TILES


def _mod_kernel(s_ref, w_ref, b_ref, o_ref):
    s = _silu(s_ref[...]).astype(bf16)
    o_ref[...] = _dot(s, w_ref[...].astype(bf16)) + b_ref[...]


def _modulation(cc, ada_w, ada_b):
    tn = 1024
    return pl.pallas_call(
        _mod_kernel,
        grid=(DEPTH, 6 * D // tn),
        in_specs=[pl.BlockSpec((8, D), lambda l, j: (0, 0)),
                  pl.BlockSpec((None, D, tn), lambda l, j: (l, 0, j)),
                  pl.BlockSpec((None, 1, tn), lambda l, j: (l, 0, j))],
        out_specs=pl.BlockSpec((None, 8, tn), lambda l, j: (l, 0, j)),
        out_shape=jax.ShapeDtypeStruct((DEPTH, 8, 6 * D), f32),
        compiler_params=_cp(("arbitrary", "arbitrary")),
        name="modulation",
    )(cc, ada_w, ada_b.reshape(DEPTH, 1, 6 * D))


def _tile_specs(arrs, tile_of):
    w = arrs[0].shape[-1]
    if arrs[0].ndim == 2:
        return [pl.BlockSpec((TM, w), lambda *g: (tile_of(*g), 0))]
    lat = pl.BlockSpec((None, TM, w), lambda *g: (tile_of(*g) // TILES, jnp.maximum(tile_of(*g) % TILES - 1, 0), 0))
    if len(arrs) == 1:
        return [lat]
    return [lat, pl.BlockSpec((None, TM, w), lambda *g: (tile_of(*g) // TILES, 0, 0))]


def _tile_value(refs, is_ctx, rows=slice(None)):
    if len(refs) == 1:
        return refs[0][rows, :]
    return jnp.where(is_ctx, refs[1][rows, :], refs[0][rows, :])


def _in_pieces():
    o = np.cumsum((0, MLA_QR, MLA_KVR, MLA_ROPE, 256, 256, 512, 32, 512, 512, 512, 512, 512))
    cq, ckv, kr, gq, gk, gv, gz, gog, ret = (int(v) for v in o[:9])
    return ((ret, 2048, C_RQ), (gv, 512, C_GV), (gog, 512, C_GOG), (cq, 512, C_CQ), (ckv, 256, C_CKV),
            (gq, 256, C_GQ), (gk, 256, C_GK), (kr, MLA_ROPE, C_KRZ), (gz, 2 * GLA_RANK, C_KRZ + MLA_ROPE))


IN_COLS = 4448
RB = 256


def _in_kernel(*refs):
    *resid, mod_ref, ws_ref, o_ref, w_ref = refs

    @pl.when(pl.program_id(0) == 0)
    def _():
        def move(i, carry):
            r0 = pl.multiple_of(i * RB, RB)
            for src, width, dst in _in_pieces():
                w_ref[pl.ds(r0, RB), dst:dst + width] = ws_ref[pl.ds(r0, RB), src:src + width]
            end = C_KRZ + MLA_ROPE + 2 * GLA_RANK
            w_ref[pl.ds(r0, RB), end:NC] = jnp.zeros((RB, NC - end), bf16)
            return carry

        lax.fori_loop(0, D // RB, move, 0)

    x = _tile_value(resid, pl.program_id(0) % TILES == 0)
    h = (_rms(x) * (1.0 + mod_ref[1:2, :]) + mod_ref[0:1, :]).astype(bf16)
    for j in range(NC // TN_IN):
        o_ref[:, j * TN_IN:(j + 1) * TN_IN] = _dot(h, w_ref[:, j * TN_IN:(j + 1) * TN_IN]).astype(bf16)


def _in_proj(resid, mods, w_in_bf, layer):
    return pl.pallas_call(
        _in_kernel,
        grid=(R // TM,),
        in_specs=_tile_specs(resid, lambda i: i) + [
            pl.BlockSpec((None, None, 6, D), lambda i: (layer, _mod_row(i), 0, 0)),
            pl.BlockSpec((None, D, IN_COLS), lambda i: (layer, 0, 0), pipeline_mode=pl.Buffered(1))],
        out_specs=pl.BlockSpec((TM, NC), lambda i: (i, 0)),
        out_shape=jax.ShapeDtypeStruct((R, NC), bf16),
        scratch_shapes=[pltpu.VMEM((D, NC), bf16)],
        compiler_params=pltpu.CompilerParams(dimension_semantics=("arbitrary",), vmem_limit_bytes=60 * 1024 * 1024),
        name="in_proj",
    )(*resid, mods, w_in_bf)


def _mla_prep_kernel(cq_ref, ckv_ref, krz_ref, tc_ref, ts_ref, qn_ref, kvn_ref, wq_ref, wkv_ref, e2_ref,
                     q_ref, k_ref, vt_ref):
    tc = tc_ref[...]
    ts = ts_ref[...]
    scale = (MLA_NOPE + MLA_ROPE) ** -0.5 * LOG2E
    hq = (_rms(cq_ref[...].astype(f32)) * qn_ref[...]).astype(bf16)
    q2 = _dot(hq, wq_ref[...])
    hkv = (_rms(ckv_ref[...].astype(f32)) * kvn_ref[...]).astype(bf16)
    kv = _dot(hkv, wkv_ref[...])
    kr2 = _dot(krz_ref[...], e2_ref[...])
    nq = MLA_H * HP
    for h in range(MLA_H):
        sl = slice(h * HP, (h + 1) * HP)
        sl2 = slice(nq + h * HP, nq + (h + 1) * HP)
        q_ref[:, sl] = ((q2[:, sl] * tc + q2[:, sl2] * ts) * scale).astype(bf16)
        k_ref[:, sl] = (kv[:, sl] + kr2[:, sl] * tc + kr2[:, sl2] * ts).astype(bf16)
    vt_ref[...] = kv[:, nq:].T.astype(bf16)


def _mla_prep(p, tc, ts, qn, kvn, wq2, wkv, e2):
    nq = MLA_H * HP
    nv = MLA_H * MLA_V
    const = lambda shape: pl.BlockSpec(shape, lambda i: (0, 0))
    return pl.pallas_call(
        _mla_prep_kernel,
        grid=(R // TM,),
        in_specs=[pl.BlockSpec((TM, MLA_QR), lambda i: (i, C_CQ // MLA_QR)),
                  pl.BlockSpec((TM, MLA_KVR), lambda i: (i, C_CKV // MLA_KVR)),
                  pl.BlockSpec((TM, 128), lambda i: (i, C_KRZ // 128)),
                  pl.BlockSpec((TM, HP), lambda i: (i % TILES, 0)),
                  pl.BlockSpec((TM, HP), lambda i: (i % TILES, 0)),
                  const((1, MLA_QR)), const((1, MLA_KVR)),
                  const((MLA_QR, 2 * nq)), const((MLA_KVR, nq + nv)), const((128, 2 * nq))],
        out_specs=[pl.BlockSpec((TM, nq), lambda i: (_lat_first(i), 0)),
                   pl.BlockSpec((TM, nq), lambda i: (i, 0)),
                   pl.BlockSpec((None, nv, TM), lambda i: (i // TILES, 0, i % TILES))],
        out_shape=[jax.ShapeDtypeStruct((R, nq), bf16), jax.ShapeDtypeStruct((R, nq), bf16),
                   jax.ShapeDtypeStruct((BATCH, nv, S), bf16)],
        compiler_params=_cp(("arbitrary",)),
        name="mla_prep",
    )(p, p, p, tc, ts, qn, kvn, wq2, wkv, e2)


TQ = 2048
ATT_HEADS_PER_STEP = 1


def _attn_body(q_ref, k_ref, vt_ref, o_ref, s_sc, n_chunks, n_heads):
    nq = q_ref.shape[0]
    qs = [q_ref[:, h * HP:(h + 1) * HP] for h in range(n_heads)]

    def scores(h, j, m):
        s = _nt(k_ref[j * TM:(j + 1) * TM, h * HP:(h + 1) * HP], qs[h])
        s_sc[h, j] = s
        cm = jnp.max(s, axis=0, keepdims=True)
        return cm if m is None else jnp.maximum(m, cm)

    m = [None] * n_heads
    for j in range(n_chunks):
        m[0] = scores(0, j, m[0])
    for h in range(n_heads):
        l = jnp.zeros((1, nq), f32)
        acc = jnp.zeros((MLA_V, nq), f32)
        for j in range(n_chunks):
            p = jnp.exp2(s_sc[h, j] - m[h])
            l = l + jnp.sum(p, axis=0, keepdims=True)
            acc = acc + _dot(vt_ref[h * MLA_V:(h + 1) * MLA_V, j * TM:(j + 1) * TM], p.astype(bf16))
            if h + 1 < n_heads:
                m[h + 1] = scores(h + 1, j, m[h + 1])
        o_ref[:, h * MLA_V:(h + 1) * MLA_V] = (acc * (1.0 / l)).T.astype(bf16)


def _attention(q, k, vt, has_ctx):
    nq = MLA_H * HP
    nv = MLA_H * MLA_V
    q3 = q.reshape(BATCH, S, nq)
    k3 = k.reshape(BATCH, S, nq)
    hs = ATT_HEADS_PER_STEP
    ya = pl.pallas_call(
        functools.partial(_attn_body, n_chunks=S // TM, n_heads=hs),
        grid=(BATCH, MLA_H // hs, N_LAT // TQ),
        in_specs=[pl.BlockSpec((None, TQ, hs * HP), lambda b, h, t: (b, t, h)),
                  pl.BlockSpec((None, S, hs * HP), lambda b, h, t: (b, 0, h)),
                  pl.BlockSpec((None, hs * MLA_V, S), lambda b, h, t: (b, h, 0))],
        out_specs=pl.BlockSpec((None, TQ, hs * MLA_V), lambda b, h, t: (b, t, h)),
        out_shape=jax.ShapeDtypeStruct((BATCH, N_LAT, nv), bf16),
        scratch_shapes=[pltpu.VMEM((hs, S // TM, TM, TQ), f32)],
        compiler_params=_cp(("arbitrary", "arbitrary", "arbitrary")),
        name="mla_attention",
    )(q3, k3, vt)
    if not has_ctx:
        return (ya,)
    ya_ctx = pl.pallas_call(
        functools.partial(_attn_body, n_chunks=1, n_heads=1),
        grid=(BATCH, MLA_H),
        in_specs=[pl.BlockSpec((None, N_CTX, HP), lambda b, h: (b, N_LAT // N_CTX, h)),
                  pl.BlockSpec((None, N_CTX, HP), lambda b, h: (b, 0, h)),
                  pl.BlockSpec((None, MLA_V, N_CTX), lambda b, h: (b, h, 0))],
        out_specs=pl.BlockSpec((None, N_CTX, MLA_V), lambda b, h: (b, 0, h)),
        out_shape=jax.ShapeDtypeStruct((BATCH, N_CTX, nv), bf16),
        scratch_shapes=[pltpu.VMEM((1, 1, TM, N_CTX), f32)],
        compiler_params=_cp(("arbitrary", "arbitrary")),
        name="mla_attention_ctx",
    )(q3, k3, vt)
    return (ya, ya_ctx)


RET_CHUNK = 128
SCAN_H = 4
SCAN_V = 128
SCAN_W = SCAN_H * SCAN_V


def _scan_consts(kh, chunk):
    dk = kh // SCAN_H
    row = lax.broadcasted_iota(jnp.int32, (chunk, SCAN_H * chunk), 0)
    col = lax.broadcasted_iota(jnp.int32, (chunk, SCAN_H * chunk), 1) % chunk
    incl = row >= col
    strict = col > row
    krow = lax.broadcasted_iota(jnp.int32, (SCAN_H * chunk, kh), 0) // chunk
    kcol = lax.broadcasted_iota(jnp.int32, (SCAN_H * chunk, kh), 1) // dk
    kmask = krow == kcol
    vrow = lax.broadcasted_iota(jnp.int32, (SCAN_H * chunk, SCAN_W), 0) // chunk
    vcol = lax.broadcasted_iota(jnp.int32, (SCAN_H * chunk, SCAN_W), 1) // SCAN_V
    vmask = vrow == vcol
    srow = lax.broadcasted_iota(jnp.int32, (SCAN_W, kh), 0) // SCAN_V
    scol = lax.broadcasted_iota(jnp.int32, (SCAN_W, kh), 1) // dk
    smask = srow == scol
    return incl, strict, kmask, vmask, smask


def _chunk_step(q, k, v, cum, cend, st_ref, amask, kmask, vmask, smask):
    qd = (q * jnp.exp(cum)).astype(bf16)
    ki = k * jnp.exp(-cum)
    kend = (k * jnp.exp(cend - cum)).astype(bf16)
    dec = jnp.exp(cend)
    kst = jnp.where(kmask, jnp.concatenate([ki] * SCAN_H, axis=0), 0.0).astype(bf16)
    att = jnp.where(amask, _nt(qd, kst), 0.0).astype(bf16)
    vbd = jnp.where(vmask, jnp.concatenate([v] * SCAN_H, axis=0), jnp.zeros((), bf16))
    st = st_ref[...]
    o = _dot(att, vbd) + _nt(qd, st.astype(bf16))
    st_ref[...] = st * dec + jnp.where(smask, _tn(v, kend), 0.0)
    return o


def _bwd_chunk(i, chunk):
    n_ctx, n_all = N_CTX // chunk, S // chunk
    return jnp.where(i < n_ctx, n_ctx - 1 - i, n_all + n_ctx - 1 - i)


def _scan_finish(of_sc, ob_sc, g_ref, y_ref):
    def fin(i, carry):
        r0 = pl.multiple_of(i * TM, TM)
        o = of_sc[pl.ds(r0, TM), :] + ob_sc[pl.ds(r0, TM), :]
        g = g_ref[pl.ds(r0, TM), :].astype(f32)
        for h in range(SCAN_H):
            sl = slice(h * SCAN_V, (h + 1) * SCAN_V)
            y_ref[pl.ds(r0, TM), sl] = (_rms(o[:, sl]) * _silu(g[:, sl])).astype(bf16)
        return carry

    lax.fori_loop(0, TILES, fin, 0)


def _gla_kernel(q_ref, k_ref, v_ref, krz_ref, og_ref, w2_ref, b2_ref, y_ref, cum_sc, of_sc, ob_sc, stf_sc, stb_sc):
    kh = GLA_H * GLA_DK
    ri = lax.broadcasted_iota(jnp.int32, (TM, TM), 0)
    ci = lax.broadcasted_iota(jnp.int32, (TM, TM), 1)
    same = (ri // CHUNK) == (ci // CHUNK)
    pre = jnp.where(same & (ci <= ri), 1.0, 0.0).astype(bf16)
    suf = jnp.where(same & (ci >= ri), 1.0, 0.0).astype(bf16)

    def exact_sum(m, x):
        hi = x.astype(bf16)
        r1 = x - hi.astype(f32)
        mid = r1.astype(bf16)
        lo = (r1 - mid.astype(f32)).astype(bf16)
        return _dot(m, hi) + _dot(m, mid) + _dot(m, lo)

    def gates(i, carry):
        r0 = pl.multiple_of(i * TM, TM)
        lg = _dot(krz_ref[pl.ds(r0, TM), :], w2_ref[...]) + b2_ref[...]
        la = (jnp.minimum(lg, 0.0) - jnp.log1p(jnp.exp(-jnp.abs(lg)))) * (1.0 / GLA_TAU)
        cum_sc[pl.ds(r0, TM), 0:kh] = exact_sum(pre, la[:, 0:kh])
        cum_sc[pl.ds(r0, TM), kh:2 * kh] = exact_sum(suf, la[:, kh:2 * kh])
        return carry

    lax.fori_loop(0, TILES, gates, 0, unroll=3)

    incl, strict, kmask, vmask, smask = _scan_consts(kh, CHUNK)
    stf_sc[...] = jnp.zeros_like(stf_sc)
    stb_sc[...] = jnp.zeros_like(stb_sc)
    qscale = GLA_DK ** -0.5

    def body(i, carry):
        rf = pl.multiple_of(i * CHUNK, CHUNK)
        cum = cum_sc[pl.ds(rf, CHUNK), 0:kh]
        of_sc[pl.ds(rf, CHUNK), :] = _chunk_step(
            q_ref[pl.ds(rf, CHUNK), :].astype(f32) * qscale, k_ref[pl.ds(rf, CHUNK), :].astype(f32),
            v_ref[pl.ds(rf, CHUNK), :], cum, cum[CHUNK - 1:CHUNK, :], stf_sc, incl, kmask, vmask, smask)
        rb = pl.multiple_of(_bwd_chunk(i, CHUNK) * CHUNK, CHUNK)
        rc = cum_sc[pl.ds(rb, CHUNK), kh:2 * kh]
        ob_sc[pl.ds(rb, CHUNK), :] = _chunk_step(
            q_ref[pl.ds(rb, CHUNK), :].astype(f32) * qscale, k_ref[pl.ds(rb, CHUNK), :].astype(f32),
            v_ref[pl.ds(rb, CHUNK), :], rc, rc[0:1, :], stb_sc, strict, kmask, vmask, smask)
        return carry

    lax.fori_loop(0, S // CHUNK, body, 0, unroll=6)
    _scan_finish(of_sc, ob_sc, og_ref, y_ref)


def _gla(p, w2p, b2):
    kh = GLA_H * GLA_DK
    p3 = p.reshape(BATCH, S, NC)
    col = lambda w, c: pl.BlockSpec((None, S, w), lambda b: (b, 0, c // w))
    return pl.pallas_call(
        _gla_kernel,
        grid=(BATCH,),
        in_specs=[col(kh, C_GQ), col(kh, C_GK), col(SCAN_W, C_GV), col(128, C_KRZ), col(SCAN_W, C_GOG),
                  pl.BlockSpec((128, 2 * kh), lambda b: (0, 0)), pl.BlockSpec((1, 2 * kh), lambda b: (0, 0))],
        out_specs=pl.BlockSpec((None, S, SCAN_W), lambda b: (b, 0, 0)),
        out_shape=jax.ShapeDtypeStruct((BATCH, S, SCAN_W), bf16),
        scratch_shapes=[pltpu.VMEM((S, 2 * kh), f32), pltpu.VMEM((S, SCAN_W), f32), pltpu.VMEM((S, SCAN_W), f32),
                        pltpu.VMEM((SCAN_W, kh), f32), pltpu.VMEM((SCAN_W, kh), f32)],
        compiler_params=_cp(("arbitrary",)),
        name="gla",
    )(p3, p3, p3, p3, p3, w2p, b2).reshape(R, SCAN_W)


def _ret_kernel(q_ref, k_ref, v_ref, g_ref, cos_ref, sin_ref, cumf_ref, cumb_ref, y_ref, of_sc, ob_sc, stf_sc, stb_sc):
    kh = RET_H * RET_DK
    incl, strict, kmask, vmask, smask = _scan_consts(kh, RET_CHUNK)
    stf_sc[...] = jnp.zeros_like(stf_sc)
    stb_sc[...] = jnp.zeros_like(stb_sc)
    kscale = RET_DK ** -0.5
    cumf = cumf_ref[...]
    cumb = cumb_ref[...]

    even = lax.broadcasted_iota(jnp.int32, (RET_CHUNK, RET_DK), 1) % 2 == 0

    def rotate(x, r0):
        cos = jnp.concatenate([cos_ref[pl.ds(r0, RET_CHUNK), :]] * RET_H, axis=1)
        sin = jnp.concatenate([sin_ref[pl.ds(r0, RET_CHUNK), :]] * RET_H, axis=1)
        parts = []
        for h in range(RET_H):
            xh = x[:, h * RET_DK:(h + 1) * RET_DK]
            parts.append(jnp.where(even, pltpu.roll(xh, RET_DK - 1, axis=1), pltpu.roll(xh, 1, axis=1)))
        return x * cos + jnp.concatenate(parts, axis=1) * sin

    def body(i, carry):
        rf = pl.multiple_of(i * RET_CHUNK, RET_CHUNK)
        of_sc[pl.ds(rf, RET_CHUNK), :] = _chunk_step(
            rotate(q_ref[pl.ds(rf, RET_CHUNK), :].astype(f32), rf),
            rotate(k_ref[pl.ds(rf, RET_CHUNK), :].astype(f32), rf) * kscale,
            v_ref[pl.ds(rf, RET_CHUNK), :], cumf, cumf[RET_CHUNK - 1:RET_CHUNK, :], stf_sc, incl, kmask, vmask, smask)
        rb = pl.multiple_of(_bwd_chunk(i, RET_CHUNK) * RET_CHUNK, RET_CHUNK)
        ob_sc[pl.ds(rb, RET_CHUNK), :] = _chunk_step(
            rotate(q_ref[pl.ds(rb, RET_CHUNK), :].astype(f32), rb),
            rotate(k_ref[pl.ds(rb, RET_CHUNK), :].astype(f32), rb) * kscale,
            v_ref[pl.ds(rb, RET_CHUNK), :], cumb, cumb[0:1, :], stb_sc, strict, kmask, vmask, smask)
        return carry

    lax.fori_loop(0, S // RET_CHUNK, body, 0, unroll=3)
    _scan_finish(of_sc, ob_sc, g_ref, y_ref)


def _retention(p, cos_r, sin_r, cumf, cumb):
    kh = RET_H * RET_DK
    p3 = p.reshape(BATCH, S, NC)
    col = lambda w, c: pl.BlockSpec((None, S, w), lambda b: (b, 0, c // w))
    const = lambda shape: pl.BlockSpec(shape, lambda b: (0, 0))
    return pl.pallas_call(
        _ret_kernel,
        grid=(BATCH,),
        in_specs=[col(kh, C_RQ), col(kh, C_RK), col(SCAN_W, C_RV), col(SCAN_W, C_RG),
                  const((S, RET_DK)), const((S, RET_DK)), const((RET_CHUNK, kh)), const((RET_CHUNK, kh))],
        out_specs=pl.BlockSpec((None, S, SCAN_W), lambda b: (b, 0, 0)),
        out_shape=jax.ShapeDtypeStruct((BATCH, S, SCAN_W), bf16),
        scratch_shapes=[pltpu.VMEM((S, SCAN_W), f32), pltpu.VMEM((S, SCAN_W), f32),
                        pltpu.VMEM((SCAN_W, kh), f32), pltpu.VMEM((SCAN_W, kh), f32)],
        compiler_params=_cp(("arbitrary",)),
        name="retention",
    )(p3, p3, p3, p3, cos_r, sin_r, cumf, cumb).reshape(R, SCAN_W)


def _lat_tile(g):
    return (g // LAT_TILES) * TILES + 1 + g % LAT_TILES


def _out_kernel(*refs, n_resid, has_ctx):
    resid, refs = refs[:n_resid], refs[n_resid:]
    *ya, yb_ref, yc_ref, mod_ref, wf_ref, rwt_ref, xo_ref, h2_ref, aff_ref, w_ref = refs
    is_ctx = (pl.program_id(0) % TILES == 0) if has_ctx else False

    @pl.when(pl.program_id(0) == 0)
    def _():
        w_ref[...] = wf_ref[...].astype(bf16)

    na = MLA_H * MLA_V
    half = TM // 2
    for r in range(2):
        rows = slice(r * half, (r + 1) * half)
        acc = _dot(_tile_value(ya, is_ctx, rows), w_ref[0:na, :])
        acc += _dot(yb_ref[rows, :], w_ref[na:na + SCAN_W, :])
        acc += _dot(yc_ref[rows, :], w_ref[na + SCAN_W:, :])
        x = _tile_value(resid, is_ctx, rows) + mod_ref[2:3, :] * acc
        xo_ref[rows, :] = x
        h = _rms(x) * (1.0 + mod_ref[4:5, :]) + mod_ref[3:4, :]
        h2_ref[rows, :] = h
        hb = h.astype(bf16)
        lg = _nt(rwt_ref[...], hb)
        e = jnp.exp(lg - jnp.max(lg, axis=0, keepdims=True))
        aff_ref[:, rows] = e / jnp.sum(e, axis=0, keepdims=True)


def _out_proj(resid, ya, yb, yc, mods, w_out, rwt, layer, has_ctx):
    tile = (lambda g: g) if has_ctx else _lat_tile
    n_tiles = R // TM if has_ctx else BATCH * LAT_TILES
    out_row = lambda w: pl.BlockSpec((TM, w), lambda g: (g, 0))
    return pl.pallas_call(
        functools.partial(_out_kernel, n_resid=len(resid), has_ctx=has_ctx),
        grid=(n_tiles,),
        in_specs=_tile_specs(resid, tile) + _tile_specs(ya, tile) + _tile_specs((yb,), tile) + _tile_specs((yc,), tile) + [
            pl.BlockSpec((None, None, 6, D), lambda g: (layer, _mod_row(tile(g)), 0, 0)),
            pl.BlockSpec((None, D, D), lambda g: (layer, 0, 0), pipeline_mode=pl.Buffered(1)),
            pl.BlockSpec((N_EXP, D), lambda g: (0, 0))],
        out_specs=[out_row(D), out_row(D), pl.BlockSpec((N_EXP, TM), lambda g: (0, g))],
        out_shape=[jax.ShapeDtypeStruct((n_tiles * TM, D), f32), jax.ShapeDtypeStruct((n_tiles * TM, D), f32),
                   jax.ShapeDtypeStruct((N_EXP, n_tiles * TM), f32)],
        scratch_shapes=[pltpu.VMEM((D, D), bf16)],
        compiler_params=_cp(("arbitrary",)),
        name="out_proj",
    )(*resid, *ya, yb, yc, mods, w_out, rwt)


def _topk_kernel(aff_ref, post_ref, tok_ref, gate_ref, pos_sc, *, has_ctx):
    ri = lax.broadcasted_iota(jnp.int32, (TM, TM), 0)
    ci = lax.broadcasted_iota(jnp.int32, (TM, TM), 1)
    before = jnp.where(ri < ci, 1.0, 0.0).astype(bf16)

    def prefix_count(m):
        out = []
        off = jnp.zeros((N_EXP, 1), f32)
        for blk in range(m.shape[1] // TM):
            mb = m[:, blk * TM:(blk + 1) * TM]
            out.append(_dot(mb.astype(bf16), before) + off)
            off = off + jnp.sum(mb, axis=1, keepdims=True)
        return jnp.concatenate(out, axis=1) if len(out) > 1 else out[0]

    def select(a, cap, base):
        capf = float(cap)

        def step(i, thr_bits):
            cand = thr_bits | jnp.left_shift(jnp.int32(1), 30 - i)
            cnt = jnp.sum(jnp.where(a >= pltpu.bitcast(cand, f32), 1.0, 0.0), axis=1, keepdims=True)
            return jnp.where(cnt >= capf, cand, thr_bits)

        thr = pltpu.bitcast(lax.fori_loop(0, 31, step, jnp.zeros((N_EXP, 1), jnp.int32)), f32)
        gt = jnp.where(a > thr, 1.0, 0.0)
        eq = jnp.where(a == thr, 1.0, 0.0)
        need = capf - jnp.sum(gt, axis=1, keepdims=True)
        keep = gt + eq * jnp.where(prefix_count(eq) < need, 1.0, 0.0)
        return jnp.where(keep > 0.5, prefix_count(keep) + float(base), -1.0)

    if has_ctx:
        pos = jnp.concatenate([select(aff_ref[:, 0:N_CTX], CAP_CTX, CAP_LAT), select(aff_ref[:, N_CTX:], CAP_LAT, 0)], axis=1)
    else:
        pos = select(aff_ref[...], CAP_LAT, 0)
    n = pos.shape[1]
    slots = tok_ref.shape[0]
    pos_sc[...] = pos
    post_ref[...] = jnp.concatenate([pos, jnp.full((128 - N_EXP, n), -1.0, f32)], axis=0).T

    slot_id = lax.broadcasted_iota(jnp.int32, (slots, n), 0).astype(f32)
    tok_id = lax.broadcasted_iota(jnp.int32, (slots, n), 1).astype(f32) + (pl.program_id(0) * n).astype(f32)
    lane = lax.broadcasted_iota(jnp.int32, (slots, 128), 1)

    def invert(e, carry):
        tok, gate = carry
        hit = pos_sc[pl.ds(e, 1), :] == slot_id
        t_e = jnp.sum(jnp.where(hit, tok_id, 0.0), axis=1, keepdims=True)
        g_e = jnp.sum(jnp.where(hit, aff_ref[pl.ds(e, 1), :], 0.0), axis=1, keepdims=True)
        return jnp.where(lane == e, t_e, tok), jnp.where(lane == e, g_e, gate)

    tok, gate = lax.fori_loop(0, N_EXP, invert, (jnp.zeros((slots, 128), f32), jnp.zeros((slots, 128), f32)))
    tok_ref[...] = tok
    gate_ref[...] = gate


def _topk(aff, has_ctx):
    n = S if has_ctx else N_LAT
    slots = SLOTS if has_ctx else CAP_LAT
    return pl.pallas_call(
        functools.partial(_topk_kernel, has_ctx=has_ctx),
        grid=(BATCH,),
        in_specs=[pl.BlockSpec((N_EXP, n), lambda b: (0, b))],
        out_specs=[pl.BlockSpec((n, 128), lambda b: (b, 0)), pl.BlockSpec((slots, 128), lambda b: (b, 0)),
                   pl.BlockSpec((slots, 128), lambda b: (b, 0))],
        out_shape=[jax.ShapeDtypeStruct((BATCH * n, 128), f32), jax.ShapeDtypeStruct((BATCH * slots, 128), f32),
                   jax.ShapeDtypeStruct((BATCH * slots, 128), f32)],
        scratch_shapes=[pltpu.VMEM((N_EXP, n), f32)],
        compiler_params=_cp(("arbitrary",)),
        name="route_topk",
    )(aff)


TF = 512
N_UP = EXP_FF // TF
N_DOWN = D // TF
N_STEP = N_UP + N_DOWN
MOE_VMEM_LIMIT = 60 * 1024 * 1024


def _moe_kernel(idx_ref, h_hbm, wg_ref, wu_ref, wd_ref, gate_ref, y_ref, xraw, xb, hm_ref, sem):
    e = pl.program_id(0)
    s = pl.program_id(1)
    rows = xb.shape[0]
    per = rows // N_STEP

    def row_copy(expert, r):
        src = idx_ref[expert * rows + r]
        return pltpu.make_async_copy(h_hbm.at[pl.ds(src, 1), :], xraw.at[pl.ds(r, 1), :], sem.at[0])

    def all_rows():
        return pltpu.make_async_copy(h_hbm.at[pl.ds(0, rows), :], xraw, sem.at[0])

    def prefetch_share():
        for r in range(per):
            row_copy(e + 1, s * per + r).start()

    @pl.when(s == 0)
    def _():
        @pl.when(e == 0)
        def _():
            def first(r, carry):
                row_copy(0, r).start()
                return carry

            lax.fori_loop(0, rows, first, 0)

        all_rows().wait()
        xb[...] = xraw[...].astype(bf16)

    @pl.when(s < N_UP)
    def _():
        prefetch_share()
        x = xb[...]
        a = _dot(x, wg_ref[...].astype(bf16))
        u = _dot(x, wu_ref[...].astype(bf16))
        hm_ref[s] = (_silu(a) * u).astype(bf16)

    @pl.when(s >= N_UP)
    def _():
        prefetch_share()
        wd = wd_ref[...].astype(bf16)
        acc = _dot(hm_ref[0], wd[0:TF, :])
        for c in range(1, N_UP):
            acc += _dot(hm_ref[c], wd[c * TF:(c + 1) * TF, :])
        lane = lax.broadcasted_iota(jnp.int32, (rows, 128), 1)
        g = jnp.sum(jnp.where(lane == e, gate_ref[...], 0.0), axis=1, keepdims=True)
        y_ref[...] = (acc * g).astype(bf16)

    @pl.when((e == N_EXP - 1) & (s == N_STEP - 1))
    def _():
        all_rows().wait()


def _moe_ffn(tok, gate, h2, w_gate, w_up, w_down, layer):
    rows = tok.shape[0]
    idx = tok[:, :N_EXP].T.astype(jnp.int32).reshape(-1)
    idx = jnp.concatenate([idx, jnp.zeros((rows,), jnp.int32)])
    up = lambda s: jnp.minimum(s, N_UP - 1)
    down = lambda s: jnp.maximum(s - N_UP, 0)
    return pl.pallas_call(
        _moe_kernel,
        grid_spec=pltpu.PrefetchScalarGridSpec(
            num_scalar_prefetch=1,
            grid=(N_EXP, N_STEP),
            in_specs=[pl.BlockSpec(memory_space=pl.ANY),
                      pl.BlockSpec((None, None, D, TF), lambda e, s, idx: (layer, e, 0, up(s))),
                      pl.BlockSpec((None, None, D, TF), lambda e, s, idx: (layer, e, 0, up(s))),
                      pl.BlockSpec((None, None, EXP_FF, TF), lambda e, s, idx: (layer, e, 0, down(s))),
                      pl.BlockSpec((rows, 128), lambda e, s, idx: (0, 0))],
            out_specs=pl.BlockSpec((None, rows, TF), lambda e, s, idx: (e, 0, down(s))),
            scratch_shapes=[pltpu.VMEM((rows, D), f32), pltpu.VMEM((rows, D), bf16),
                            pltpu.VMEM((N_UP, rows, TF), bf16), pltpu.SemaphoreType.DMA((1,))]),
        out_shape=jax.ShapeDtypeStruct((N_EXP, rows, D), bf16),
        compiler_params=pltpu.CompilerParams(dimension_semantics=("arbitrary", "arbitrary"),
                                             vmem_limit_bytes=MOE_VMEM_LIMIT, disable_bounds_checks=True),
        name="moe_ffn",
    )(idx, h2, w_gate, w_up, w_down, gate)


TN_C = 1024


def _combine_kernel(post_ref, ys_ref, x_ref, mod_ref, *rest, has_ctx):
    gain_ref, o_ref = rest if len(rest) == 2 else (None, rest[0])
    tn = o_ref.shape[-1]
    t = pl.program_id(2) if has_ctx else pl.program_id(2) + 1
    pb = post_ref[...]

    def scatter(cap, base, ys):
        if cap % 128 == 0:
            slot = lax.broadcasted_iota(jnp.int32, (TM, cap), 1).astype(f32) + float(base)
            onehot = jnp.concatenate(
                [jnp.where(pb[:, e:e + 1] == slot, 1.0, 0.0).astype(bf16) for e in range(N_EXP)], axis=1)
        else:
            pc = pb - float(base)
            pc = jnp.where((pc >= 0.0) & (pc < float(cap)), pc, -1.0).astype(bf16)
            er = lax.broadcasted_iota(jnp.int32, (128, N_EXP * cap), 0)
            ec = lax.broadcasted_iota(jnp.int32, (128, N_EXP * cap), 1) // cap
            rep = jnp.where(er == ec, 1.0, 0.0).astype(bf16)
            slot = (lax.broadcasted_iota(jnp.int32, (TM, N_EXP * cap), 1) % cap).astype(f32)
            onehot = jnp.where(_dot(pc, rep) == slot, 1.0, 0.0).astype(bf16)
        x = x_ref[...] + mod_ref[5:6, :] * _dot(onehot, ys)
        o_ref[...] = x if gain_ref is None else _rms(x) * gain_ref[...]

    if has_ctx:
        @pl.when(t == 0)
        def _():
            scatter(CAP_CTX, CAP_LAT, ys_ref[:, CAP_LAT:SLOTS, :].reshape(N_EXP * CAP_CTX, tn))

    @pl.when(t > 0)
    def _():
        scatter(CAP_LAT, 0, ys_ref[:, 0:CAP_LAT, :].reshape(N_EXP * CAP_LAT, tn))


def _combine(post, ys, xa, mods, layer, has_ctx, final_gain=None):
    final = final_gain is not None
    assert not (final and has_ctx)
    slots = SLOTS if has_ctx else CAP_LAT
    tps = TILES if has_ctx else LAT_TILES
    tn = D if final else TN_C
    tile = lambda b, t: b * tps + t
    mod_row = (lambda b, t: jnp.where(t == 0, BATCH, b)) if has_ctx else (lambda b, t: b)
    extra_in, extra_specs = ((final_gain.reshape(1, D),), [pl.BlockSpec((1, D), lambda b, n, t: (0, 0))]) if final else ((), [])
    out = pl.pallas_call(
        functools.partial(_combine_kernel, has_ctx=has_ctx),
        grid=(BATCH, D // tn, tps),
        in_specs=[pl.BlockSpec((TM, 128), lambda b, n, t: (tile(b, t), 0)),
                  pl.BlockSpec((N_EXP, None, slots, tn), lambda b, n, t: (0, b, 0, n)),
                  pl.BlockSpec((TM, tn), lambda b, n, t: (tile(b, t), n)),
                  pl.BlockSpec((None, None, 6, tn), lambda b, n, t: (layer, mod_row(b, t), 0, n))] + extra_specs,
        out_specs=pl.BlockSpec((TM, tn), lambda b, n, t: (tile(b, t), n)),
        out_shape=jax.ShapeDtypeStruct((BATCH * tps * TM, D), f32),
        compiler_params=_cp(("arbitrary", "arbitrary", "arbitrary")),
        name="moe_combine",
    )(post, ys.reshape(N_EXP, BATCH, slots, D), xa, mods, *extra_in)
    return out.reshape(BATCH, N_LAT, D) if final else out


def _mla_weights(w_uq, w_ukv):
    half = MLA_ROPE // 2
    wq = w_uq.reshape(MLA_QR, MLA_H, MLA_NOPE + MLA_ROPE)
    nope = wq[:, :, :MLA_NOPE]
    rope = wq[:, :, MLA_NOPE:].reshape(MLA_QR, MLA_H, half, 2)
    ev, od = rope[..., 0], rope[..., 1]
    zpad = jnp.zeros((MLA_QR, MLA_H, HP - MLA_NOPE - MLA_ROPE), w_uq.dtype)
    q_main = jnp.concatenate([nope, ev, od, zpad], axis=-1).reshape(MLA_QR, MLA_H * HP)
    q_part = jnp.concatenate([jnp.zeros_like(nope), od, ev, zpad], axis=-1).reshape(MLA_QR, MLA_H * HP)
    wkv = w_ukv.reshape(MLA_KVR, MLA_H, MLA_NOPE + MLA_V)
    k_main = jnp.concatenate([wkv[:, :, :MLA_NOPE], jnp.zeros((MLA_KVR, MLA_H, HP - MLA_NOPE), w_ukv.dtype)], axis=-1)
    v_main = wkv[:, :, MLA_NOPE:]
    return (jnp.concatenate([q_main, q_part], axis=1).astype(bf16),
            jnp.concatenate([k_main.reshape(MLA_KVR, MLA_H * HP), v_main.reshape(MLA_KVR, MLA_H * MLA_V)], axis=1).astype(bf16))


def _rope_key_placement():
    half = MLA_ROPE // 2
    nq = MLA_H * HP
    e2 = np.zeros((128, 2 * nq), np.float32)
    for h in range(MLA_H):
        for i in range(half):
            ev, od = h * HP + MLA_NOPE + i, h * HP + MLA_NOPE + half + i
            e2[2 * i, ev] = e2[2 * i + 1, od] = 1.0
            e2[2 * i + 1, nq + ev] = e2[2 * i, nq + od] = 1.0
    return jnp.asarray(e2, bf16)


def _tables():
    rows = N_LAT // GRID_W
    row = np.repeat(np.arange(rows, dtype=np.float32), GRID_W)
    colp = np.tile(np.arange(GRID_W, dtype=np.float32), rows)
    n_freq = MLA_ROPE // 4
    nf32 = np.float32
    inv = np.power(nf32(ROPE_BASE), -np.arange(n_freq, dtype=nf32) / nf32(n_freq))
    ang = np.concatenate([row[:, None] * inv, colp[:, None] * inv], axis=-1)
    cos_a, sin_a = np.cos(ang), np.sin(ang)
    one = np.ones((N_LAT, MLA_NOPE), nf32)
    zpad = np.zeros((N_LAT, HP - MLA_NOPE - MLA_ROPE), nf32)
    tc_lat = np.concatenate([one, cos_a, cos_a, zpad], axis=1)
    ts_lat = np.concatenate([0 * one, -sin_a, sin_a, zpad], axis=1)
    tc_ctx = np.concatenate([np.ones((N_CTX, MLA_NOPE + MLA_ROPE), nf32), np.zeros((N_CTX, HP - MLA_NOPE - MLA_ROPE), nf32)], axis=1)
    tc = np.concatenate([tc_ctx, tc_lat], axis=0)
    ts = np.concatenate([np.zeros((N_CTX, HP), nf32), ts_lat], axis=0)

    inv_r = nf32(1.0) / np.power(nf32(ROPE_BASE), np.linspace(0.0, 1.0, RET_DK // 2, dtype=nf32))
    ang_r = np.arange(N_LAT, dtype=nf32)[:, None] * inv_r
    cos_r = np.concatenate([np.ones((N_CTX, RET_DK), nf32), np.repeat(np.cos(ang_r), 2, axis=1)], axis=0)
    sin_r = np.concatenate([np.zeros((N_CTX, RET_DK), nf32),
                            np.stack([-np.sin(ang_r), np.sin(ang_r)], axis=-1).reshape(N_LAT, RET_DK)], axis=0)

    def log_decay(direction):
        e = nf32(RET_EXP0 + direction) + nf32(2.0) * np.arange(RET_H, dtype=nf32)
        return np.repeat(np.log1p(-np.exp2(-e)), RET_DK)[None, :]

    steps = np.arange(1, RET_CHUNK + 1, dtype=nf32)[:, None]
    cumf = steps * log_decay(0.0)
    cumb = steps[::-1] * log_decay(1.0)
    return tc, ts, cos_r, sin_r, cumf, cumb


def kernel(x, c, ctx, c_ctx, ada_w, ada_b, w_in, mla_q_norm, mla_w_uq, mla_kv_norm, mla_w_ukv, gla_gate_w2,
           gla_gate_b, w_out, router_w, exp_w_gate, exp_w_up, exp_w_down, final_norm):
    resid = (x, ctx)
    cc = jnp.concatenate([c, c_ctx[None, :], jnp.zeros((8 - BATCH - 1, D), f32)], axis=0)
    mods = _modulation(cc, ada_w, ada_b).reshape(DEPTH, 8, 6, D)
    tc, ts, cos_r, sin_r, cumf, cumb = _tables()
    e2 = _rope_key_placement()
    w_in_bf = w_in.astype(bf16)
    kh = GLA_H * GLA_DK

    for l in range(DEPTH):
        wq2, wkv = _mla_weights(mla_w_uq[l], mla_w_ukv[l])
        w2p = jnp.zeros((128, 2 * kh), f32)
        w2p = w2p.at[64:64 + GLA_RANK, 0:kh].set(gla_gate_w2[l, 0]).at[64 + GLA_RANK:64 + 2 * GLA_RANK, kh:].set(gla_gate_w2[l, 1])
        b2 = gla_gate_b[l].reshape(1, 2 * kh)

        has_ctx = l < DEPTH - 1
        p = _in_proj(resid, mods, w_in_bf, l)
        q, k, vt = _mla_prep(p, tc, ts, mla_q_norm[l].reshape(1, -1), mla_kv_norm[l].reshape(1, -1), wq2, wkv, e2)
        ya = _attention(q, k, vt, has_ctx)
        yb = _gla(p, w2p.astype(bf16), b2)
        yc = _retention(p, cos_r, sin_r, cumf, cumb)
        xa, h2, aff = _out_proj(resid, ya, yb, yc, mods, w_out, router_w[l].T.astype(bf16), l, has_ctx)
        post, tok, gate = _topk(aff, has_ctx)
        ys = _moe_ffn(tok, gate, h2, exp_w_gate, exp_w_up, exp_w_down, l)
        resid = (_combine(post, ys, xa, mods, l, has_ctx, final_gain=None if has_ctx else final_norm),)
    return resid[0]
```
